```python
import jax
import jax.numpy as jnp
from jax import lax
import numpy as np

D_MODEL = 1024
BATCH = 4
SEQ = 8192
DEPTH = 2

GRID_W = 64
CTX_LEN = 256
NA_HEADS = 8
NA_HEAD_DIM = 64
NA_WIN_ROWS = 8
NA_WIN_COLS = 16
RET_HEADS = 4
RET_DK = 64
RET_DV = 128
RET_CHUNK = 128
RET_MIN_EXP = 5.0
RET_MAX_EXP = 12.0
ROPE_BASE = 10000.0
FOURIER_GROUPS = 8
FFN_DIM = 2816
N_EXPERTS = 8
TOP_K = 2
EXPERT_DIM = 3584
MOE_BLOCK = 256
NORM_EPS = 1e-6
NA_WIDTH = NA_HEADS * NA_HEAD_DIM
RET_QK_WIDTH = RET_HEADS * RET_DK
RET_V_WIDTH = RET_HEADS * RET_DV
Q_SIDE_WIDTH = NA_WIDTH + RET_QK_WIDTH + RET_V_WIDTH
KV_SIDE_WIDTH = 2 * NA_WIDTH + RET_QK_WIDTH + RET_V_WIDTH
IN_WIDTH = Q_SIDE_WIDTH + KV_SIDE_WIDTH
MIX_WIDTH = NA_WIDTH + RET_V_WIDTH

kernel_name = 'hybrid_natten_retention_fnet_moe_dit'


def _split(t, sizes):
    return jnp.split(t, [int(s) for s in np.cumsum(sizes)[:-1]], axis=-1)


def rms_normalize(x):
    xf = x.astype(jnp.float32)
    return xf * lax.rsqrt(jnp.mean(xf * xf, axis=-1, keepdims=True) + NORM_EPS)


def rmsnorm(x, gain):
    return (rms_normalize(x) * gain.astype(jnp.float32)).astype(x.dtype)


def ada_chunks(cond, w_ada, b_ada, n_chunks):
    d = w_ada.shape[0]
    m = jax.nn.silu(cond) @ w_ada[:, :n_chunks * d] + b_ada[:n_chunks * d]
    if m.ndim == 2:
        m = m[:, None, :]
    return jnp.split(m, n_chunks, axis=-1)


def modulate(x, shift, scale):
    return x * (1.0 + scale) + shift


def heads(t, n_heads, dim):
    return t.reshape(t.shape[0], t.shape[1], n_heads, dim)


def axial_rope(n, dim):
    pos = jnp.arange(n)
    row = (pos // GRID_W).astype(jnp.float32)
    col = (pos % GRID_W).astype(jnp.float32)
    n_freq = dim // 4
    inv_freq = ROPE_BASE ** (-jnp.arange(n_freq, dtype=jnp.float32) / n_freq)
    ang = jnp.concatenate([row[:, None] * inv_freq, col[:, None] * inv_freq], axis=-1)
    return jnp.cos(ang), jnp.sin(ang)


def apply_rope(x, cos, sin):
    b, n, h, d = x.shape
    xp = x.astype(jnp.float32).reshape(b, n, h, d // 2, 2)
    x1, x2 = xp[..., 0], xp[..., 1]
    cs, sn = cos[None, :, None, :], sin[None, :, None, :]
    return jnp.stack([x1 * cs - x2 * sn, x1 * sn + x2 * cs], axis=-1).reshape(b, n, h, d).astype(x.dtype)


def neighbourhood_attention(q, k, v, k_ctx, v_ctx, rpb):
    b, n, h, hd = q.shape
    rows = n // GRID_W
    kr = min(NA_WIN_ROWS, rows)
    kc = NA_WIN_COLS
    col = np.arange(GRID_W)
    col_start = np.clip(col - kc // 2, 0, GRID_W - kc)
    col_idx = col_start[:, None] + np.arange(kc)[None, :]
    col_off = col_idx - col[:, None] + (kc - 1)

    def to_grid(t):
        return t.reshape(b, rows, GRID_W, h, hd).transpose(0, 3, 1, 2, 4)

    qg, kg, vg = to_grid(q), to_grid(k), to_grid(v)
    kx = k_ctx.transpose(0, 2, 1, 3)
    vx = v_ctx.transpose(0, 2, 1, 3)
    scale = hd ** -0.5
    n_nb = kr * kc

    def row_block(r):
        r0 = jnp.clip(r - kr // 2, 0, rows - kr)
        q_r = lax.dynamic_index_in_dim(qg, r, axis=2, keepdims=False)
        k_strip = lax.dynamic_slice_in_dim(kg, r0, kr, axis=2)
        v_strip = lax.dynamic_slice_in_dim(vg, r0, kr, axis=2)
        k_nb = k_strip[:, :, :, col_idx]
        v_nb = v_strip[:, :, :, col_idx]
        row_off = r0 - r + jnp.arange(kr) + (NA_WIN_ROWS - 1)
        bias = rpb[:, row_off[:, None, None], col_off[None, :, :]].transpose(0, 2, 1, 3)
        s_nb = jnp.einsum('bhqd,bhiqjd->bhqij', q_r, k_nb) * scale + bias[None]
        s_cx = jnp.einsum('bhqd,bhcd->bhqc', q_r, kx) * scale
        s = jnp.concatenate([s_nb.reshape(b, h, GRID_W, n_nb), s_cx], axis=-1).astype(jnp.float32)
        p = jax.nn.softmax(s, axis=-1).astype(v.dtype)
        p_nb = p[..., :n_nb].reshape(b, h, GRID_W, kr, kc)
        p_cx = p[..., n_nb:]
        return (jnp.einsum('bhqij,bhiqjd->bhqd', p_nb, v_nb)
                + jnp.einsum('bhqc,bhcd->bhqd', p_cx, vx))

    out = lax.map(row_block, jnp.arange(rows))
    return out.transpose(1, 0, 3, 2, 4).reshape(b, n, h * hd)


def retention_chunkwise(q, k, v, log_g, s0):
    b, n, h, dk = q.shape
    dv = v.shape[-1]
    nc = n // RET_CHUNK

    def chunks(t):
        return t.reshape(b, nc, RET_CHUNK, h, t.shape[-1]).transpose(1, 0, 3, 2, 4)

    idx = jnp.arange(RET_CHUNK, dtype=jnp.float32)
    diff = idx[:, None] - idx[None, :]
    decay_mask = jnp.where(diff >= 0, jnp.exp(jnp.maximum(diff, 0.0)[None] * log_g[:, None, None]), 0.0)
    xi = jnp.exp((idx + 1.0)[None, :] * log_g[:, None])[..., None]
    zeta = jnp.exp((RET_CHUNK - 1.0 - idx)[None, :] * log_g[:, None])[..., None]
    g_chunk = jnp.exp(RET_CHUNK * log_g)[:, None, None]

    def step(state, inp):
        qi, ki, vi = inp
        inner = jnp.einsum('bhid,bhjd->bhij', qi, ki) * decay_mask
        o = jnp.einsum('bhij,bhjv->bhiv', inner, vi) + jnp.einsum('bhid,bhdv->bhiv', qi, state) * xi
        state = state * g_chunk + jnp.einsum('bhjd,bhjv->bhdv', ki * zeta, vi)
        return state, o

    _, o = lax.scan(step, s0, (chunks(q), chunks(k), chunks(v)))
    return o.transpose(1, 0, 3, 2, 4).reshape(b, n, h, dv)


def bidirectional_retention(q, k, v, k_ctx, v_ctx, a_fwd, a_bwd):
    f32 = jnp.float32
    q, k, v = q.astype(f32), k.astype(f32), v.astype(f32)
    k_ctx, v_ctx = k_ctx.astype(f32), v_ctx.astype(f32)
    lg_f = jax.nn.log_sigmoid(a_fwd.astype(f32))
    lg_b = jax.nn.log_sigmoid(a_bwd.astype(f32))
    n_ctx = k_ctx.shape[1]
    pos = jnp.arange(n_ctx, dtype=f32)
    w_f = jnp.exp((n_ctx - 1.0 - pos)[None, :] * lg_f[:, None])
    w_b = jnp.exp(pos[None, :] * lg_b[:, None])
    s_f = jnp.einsum('blhd,hl,blhv->bhdv', k_ctx, w_f, v_ctx)
    s_b = jnp.einsum('blhd,hl,blhv->bhdv', k_ctx, w_b, v_ctx)
    o_f = retention_chunkwise(q, k, v, lg_f, s_f)
    o_b = retention_chunkwise(q[:, ::-1], k[:, ::-1], v[:, ::-1], lg_b, s_b)[:, ::-1]
    return o_f + o_b


def swiglu(h, w_gate_up, w_down):
    g, u = jnp.split(h @ w_gate_up, 2, axis=-1)
    return (jax.nn.silu(g) * u) @ w_down


def fourier_mix(h):
    b, n, d = h.shape
    hg = h.astype(jnp.float32).reshape(b, n, FOURIER_GROUPS, d // FOURIER_GROUPS)
    f = jnp.fft.fft2(hg, axes=(1, 3), norm='ortho').real
    return f.reshape(b, n, d).astype(h.dtype)


def moe_swiglu(h, w_router, w_gate_up, w_down):
    t, d = h.shape
    logits = (h @ w_router).astype(jnp.float32)
    top_logit, top_idx = lax.top_k(logits, TOP_K)
    gates = jax.nn.softmax(top_logit, axis=-1)
    n_assign = t * TOP_K
    expert = top_idx.reshape(n_assign)
    token = jnp.arange(n_assign, dtype=jnp.int32) // TOP_K
    gate = gates.reshape(n_assign)
    order = jnp.argsort(expert)
    e_sorted = expert[order]
    counts = jnp.bincount(expert, length=N_EXPERTS)
    padded = (counts + MOE_BLOCK - 1) // MOE_BLOCK * MOE_BLOCK
    pad_end = jnp.cumsum(padded)
    pad_start = pad_end - padded
    start = jnp.cumsum(counts) - counts
    slot = pad_start[e_sorted] + jnp.arange(n_assign) - start[e_sorted]
    n_blocks = (n_assign + N_EXPERTS * (MOE_BLOCK - 1) + MOE_BLOCK - 1) // MOE_BLOCK
    n_slots = n_blocks * MOE_BLOCK
    slot_token = jnp.full((n_slots,), t, jnp.int32).at[slot].set(token[order])
    slot_gate = jnp.zeros((n_slots,), jnp.float32).at[slot].set(gate[order])
    block_expert = jnp.minimum(
        jnp.searchsorted(pad_end, jnp.arange(n_blocks) * MOE_BLOCK, side='right'), N_EXPERTS - 1)
    h_pad = jnp.concatenate([h, jnp.zeros((1, d), h.dtype)], axis=0)
    xb = h_pad[slot_token].reshape(n_blocks, MOE_BLOCK, d)

    def expert_block(args):
        xblk, e = args
        return swiglu(xblk, w_gate_up[e], w_down[e])

    yb = lax.map(expert_block, (xb, block_expert)).reshape(n_slots, d)
    y = yb * slot_gate[:, None].astype(yb.dtype)
    return jax.ops.segment_sum(y, slot_token, num_segments=t + 1)[:t]


def even_layer(x, c, ctx, c_ctx, norm_mix, norm_ffn, w_ada, b_ada, w_in, q_gain, k_gain, rpb,
               ret_decay_fwd, ret_decay_bwd, w_out, w_gate_up, w_down):
    b, n, d = x.shape
    sh_m, sc_m, g_m, sh_f, sc_f, g_f = ada_chunks(c, w_ada, b_ada, 6)
    cx_sh, cx_sc = ada_chunks(c_ctx, w_ada, b_ada, 2)
    h = modulate(rmsnorm(x, norm_mix), sh_m, sc_m)
    hx = modulate(rmsnorm(ctx, norm_mix), cx_sh, cx_sc)
    proj = h @ w_in
    proj_x = hx @ w_in[:, Q_SIDE_WIDTH:]
    na_q, ret_q, ret_g, na_k, na_v, ret_k, ret_v = _split(
        proj, (NA_WIDTH, RET_QK_WIDTH, RET_V_WIDTH, NA_WIDTH, NA_WIDTH, RET_QK_WIDTH, RET_V_WIDTH))
    cx_na_k, cx_na_v, cx_ret_k, cx_ret_v = _split(
        proj_x, (NA_WIDTH, NA_WIDTH, RET_QK_WIDTH, RET_V_WIDTH))

    qa = rmsnorm(heads(na_q, NA_HEADS, NA_HEAD_DIM), q_gain)
    ka = rmsnorm(heads(na_k, NA_HEADS, NA_HEAD_DIM), k_gain)
    kxa = rmsnorm(heads(cx_na_k, NA_HEADS, NA_HEAD_DIM), k_gain)
    o_na = neighbourhood_attention(qa, ka, heads(na_v, NA_HEADS, NA_HEAD_DIM),
                                   kxa, heads(cx_na_v, NA_HEADS, NA_HEAD_DIM), rpb)

    cos, sin = axial_rope(n, RET_DK)
    qr = apply_rope(heads(ret_q, RET_HEADS, RET_DK), cos, sin)
    kr = apply_rope(heads(ret_k, RET_HEADS, RET_DK), cos, sin) * (RET_DK ** -0.5)
    kxr = heads(cx_ret_k, RET_HEADS, RET_DK) * (RET_DK ** -0.5)
    o_ret = bidirectional_retention(qr, kr, heads(ret_v, RET_HEADS, RET_DV),
                                    kxr, heads(cx_ret_v, RET_HEADS, RET_DV), ret_decay_fwd, ret_decay_bwd)
    o_ret = rms_normalize(o_ret).reshape(b, n, RET_V_WIDTH).astype(x.dtype) * jax.nn.silu(ret_g)

    mix = jnp.concatenate([o_na.astype(x.dtype), o_ret], axis=-1) @ w_out
    x = x + g_m * mix
    h2 = modulate(rmsnorm(x, norm_ffn), sh_f, sc_f)
    return x + g_f * swiglu(h2, w_gate_up, w_down)


def odd_layer(x, c, norm_mix, norm_ffn, w_ada, b_ada, w_fourier, w_router, w_exp_gate_up, w_exp_down):
    b, n, d = x.shape
    sh_m, sc_m, g_m, sh_f, sc_f, g_f = ada_chunks(c, w_ada, b_ada, 6)
    h = modulate(rmsnorm(x, norm_mix), sh_m, sc_m)
    x = x + g_m * (fourier_mix(h) @ w_fourier)
    h2 = modulate(rmsnorm(x, norm_ffn), sh_f, sc_f)
    y = moe_swiglu(h2.reshape(b * n, d), w_router, w_exp_gate_up, w_exp_down).reshape(b, n, d)
    return x + g_f * y


def setup_inputs(seed: int = 0) -> dict:
    key = jax.random.key(seed)
    keys = jax.random.split(key, 32)
    d = D_MODEL

    def nrm(i, shape, scale):
        return scale * jax.random.normal(keys[i], shape, jnp.float32)

    ret_init = jnp.asarray(np.log(2.0 ** np.linspace(RET_MIN_EXP, RET_MAX_EXP, RET_HEADS) - 1.0), jnp.float32)
    return {
        'x': nrm(0, (BATCH, SEQ, d), 1.0),
        'c': nrm(1, (BATCH, d), 1.0),
        'ctx': nrm(2, (BATCH, CTX_LEN, d), 1.0),
        'c_ctx': nrm(3, (d,), 1.0),
        'l0_norm_mix': 1.0 + nrm(4, (d,), 0.05),
        'l0_norm_ffn': 1.0 + nrm(5, (d,), 0.05),
        'l0_w_ada': nrm(6, (d, 6 * d), 0.5 * d ** -0.5),
        'l0_b_ada': nrm(7, (6 * d,), 0.02),
        'l0_w_in': nrm(8, (d, IN_WIDTH), d ** -0.5),
        'l0_q_gain': 1.0 + nrm(9, (NA_HEAD_DIM,), 0.05),
        'l0_k_gain': 1.0 + nrm(10, (NA_HEAD_DIM,), 0.05),
        'l0_rpb': nrm(11, (NA_HEADS, 2 * NA_WIN_ROWS - 1, 2 * NA_WIN_COLS - 1), 0.2),
        'l0_ret_decay_fwd': ret_init + nrm(12, (RET_HEADS,), 0.1),
        'l0_ret_decay_bwd': ret_init + nrm(13, (RET_HEADS,), 0.1),
        'l0_w_out': nrm(14, (MIX_WIDTH, d), MIX_WIDTH ** -0.5),
        'l0_w_gate_up': nrm(15, (d, 2 * FFN_DIM), d ** -0.5),
        'l0_w_down': nrm(16, (FFN_DIM, d), FFN_DIM ** -0.5),
        'l1_norm_mix': 1.0 + nrm(17, (d,), 0.05),
        'l1_norm_ffn': 1.0 + nrm(18, (d,), 0.05),
        'l1_w_ada': nrm(19, (d, 6 * d), 0.5 * d ** -0.5),
        'l1_b_ada': nrm(20, (6 * d,), 0.02),
        'l1_w_fourier': nrm(21, (d, d), d ** -0.5),
        'l1_w_router': nrm(22, (d, N_EXPERTS), d ** -0.5),
        'l1_w_exp_gate_up': nrm(23, (N_EXPERTS, d, 2 * EXPERT_DIM), d ** -0.5),
        'l1_w_exp_down': nrm(24, (N_EXPERTS, EXPERT_DIM, d), EXPERT_DIM ** -0.5),
    }


def reference(x, c, ctx, c_ctx,
              l0_norm_mix, l0_norm_ffn, l0_w_ada, l0_b_ada, l0_w_in, l0_q_gain, l0_k_gain, l0_rpb,
              l0_ret_decay_fwd, l0_ret_decay_bwd, l0_w_out, l0_w_gate_up, l0_w_down,
              l1_norm_mix, l1_norm_ffn, l1_w_ada, l1_b_ada, l1_w_fourier, l1_w_router,
              l1_w_exp_gate_up, l1_w_exp_down):
    layers = (
        (l0_norm_mix, l0_norm_ffn, l0_w_ada, l0_b_ada, l0_w_in, l0_q_gain, l0_k_gain, l0_rpb,
         l0_ret_decay_fwd, l0_ret_decay_bwd, l0_w_out, l0_w_gate_up, l0_w_down),
        (l1_norm_mix, l1_norm_ffn, l1_w_ada, l1_b_ada, l1_w_fourier, l1_w_router,
         l1_w_exp_gate_up, l1_w_exp_down),
    )
    for layer in range(DEPTH):
        if layer % 2 == 0:
            x = even_layer(x, c, ctx, c_ctx, *layers[layer])
        else:
            x = odd_layer(x, c, *layers[layer])
    return x
```

```python
import functools

import numpy as np
import jax
import jax.numpy as jnp
from jax import lax
from jax.experimental import pallas as pl
from jax.experimental.pallas import tpu as pltpu

F32 = jnp.float32
BF16 = jnp.bfloat16
HIGHEST = lax.Precision.HIGHEST

GRID_W = 64
NA_HEADS = 8
NA_HEAD_DIM = 64
NA_WIN_ROWS = 8
NA_WIN_COLS = 16
RET_HEADS = 4
RET_DK = 64
RET_DV = 128
ROPE_BASE = 10000.0
FOURIER_GROUPS = 8
TOP_K = 2
NORM_EPS = 1e-6

NA_WIDTH = NA_HEADS * NA_HEAD_DIM
RET_QK_WIDTH = RET_HEADS * RET_DK
RET_V_WIDTH = RET_HEADS * RET_DV

LANES = 128
SUBLANES = 8
VMEM_LIMIT = 48 * 1024 * 1024

NEG_BIG = -1e30

INPROJ_ROWS = 512
NA_QROWS = 8
NA_KROWS = 16
RET_CHUNK = 512
FFN_ROWS = 1024
FFT_N2 = 64
MOE_BLOCK = 1024
ROWCOPY_CHUNK = 2048


def _cparams(*sem):
    return pltpu.CompilerParams(dimension_semantics=sem, vmem_limit_bytes=VMEM_LIMIT)


def _sigmoid(x):
    return 1.0 / (1.0 + jnp.exp(-x))


def _dot(a, b):
    return jnp.dot(a, b, preferred_element_type=F32)


def _dot_nt(a, b):
    return lax.dot_general(a, b, (((1,), (1,)), ((), ())), preferred_element_type=F32)


def _dot_tn(a, b):
    return lax.dot_general(a, b, (((0,), (0,)), ((), ())), preferred_element_type=F32)


def _rms_mod(x, gain, shift, scale):
    ms = jnp.mean(x * x, axis=-1, keepdims=True)
    return (x * lax.rsqrt(ms + NORM_EPS) * gain) * (1.0 + scale) + shift


def _ada_kernel(c_ref, w_ref, b_ref, o_ref):
    c = c_ref[...]
    s = c * _sigmoid(c)
    o_ref[...] = jnp.dot(s, w_ref[...], precision=HIGHEST, preferred_element_type=F32) + b_ref[...]


def _ada(cond, w_ada, b_ada):
    d, n = w_ada.shape
    cond8 = jnp.zeros((SUBLANES, d), F32).at[:cond.shape[0]].set(cond)
    tn = 1024
    return pl.pallas_call(
        _ada_kernel,
        grid=(n // tn,),
        in_specs=[pl.BlockSpec((SUBLANES, d), lambda j: (0, 0)),
                  pl.BlockSpec((d, tn), lambda j: (0, j)),
                  pl.BlockSpec((1, tn), lambda j: (0, j))],
        out_specs=pl.BlockSpec((SUBLANES, tn), lambda j: (0, j)),
        out_shape=jax.ShapeDtypeStruct((SUBLANES, n), F32),
        compiler_params=_cparams("arbitrary"),
        name="ada",
    )(cond8, w_ada, b_ada.reshape(1, n))


_O_NAQ = 0
_O_RQ = _O_NAQ + NA_WIDTH
_O_RG = _O_RQ + RET_QK_WIDTH
_O_NAK = _O_RG + RET_V_WIDTH
_O_NAV = _O_NAK + NA_WIDTH
_O_RK = _O_NAV + NA_WIDTH
_O_RV = _O_RK + RET_QK_WIDTH
_IN_WIDTH = _O_RV + RET_V_WIDTH


def _inproj_kernel(x_ref, mod_ref, gain_ref, w_ref, qg_ref, kg_ref, gmat_ref, cos_ref, sin_ref,
                   naq_ref, rq_ref, rg_ref, nak_ref, nav_ref, rk_ref, rv_ref):
    h = _rms_mod(x_ref[...], gain_ref[...], mod_ref[0, 0:1, :], mod_ref[0, 1:2, :]).astype(BF16)
    p = _dot(h, w_ref[...])
    gmat = gmat_ref[...]

    def headnorm(t, g):
        ms = _dot((t * t).astype(BF16), gmat)
        return t * lax.rsqrt(ms + NORM_EPS) * g

    cos = cos_ref[...]
    sin = sin_ref[...]
    lane = lax.broadcasted_iota(jnp.int32, (1, RET_QK_WIDTH), 1)
    first_half = (lane % RET_DK) < (RET_DK // 2)

    def rope(t):
        partner = jnp.where(first_half,
                            pltpu.roll(t, RET_QK_WIDTH - RET_DK // 2, axis=1),
                            pltpu.roll(t, RET_DK // 2, axis=1))
        return t * cos + partner * sin

    naq_ref[...] = (headnorm(p[:, _O_NAQ:_O_RQ], qg_ref[...]) * (NA_HEAD_DIM ** -0.5)).astype(BF16)
    rq_ref[...] = rope(p[:, _O_RQ:_O_RG]).astype(BF16)
    g = p[:, _O_RG:_O_NAK]
    rg_ref[...] = (g * _sigmoid(g)).astype(BF16)
    nak_ref[...] = headnorm(p[:, _O_NAK:_O_NAV], kg_ref[...]).astype(BF16)
    nav_ref[...] = p[:, _O_NAV:_O_RK].astype(BF16)
    rk_ref[...] = (rope(p[:, _O_RK:_O_RV]) * (RET_DK ** -0.5)).astype(BF16)
    rv_ref[...] = p[:, _O_RV:_IN_WIDTH].astype(BF16)


def _inproj(x2d, mods, gain, w_bf, qg, kg, gmat, cos, sin, tm, tiles_per_mod, pos_tiles):
    t, d = x2d.shape
    widths = (NA_WIDTH, RET_QK_WIDTH, RET_V_WIDTH, NA_WIDTH, NA_WIDTH, RET_QK_WIDTH, RET_V_WIDTH)
    row = lambda i: (i, 0)
    const = lambda i: (0, 0)
    return pl.pallas_call(
        _inproj_kernel,
        grid=(t // tm,),
        in_specs=[pl.BlockSpec((tm, d), row),
                  pl.BlockSpec((1, 6, d), lambda i: (i // tiles_per_mod, 0, 0)),
                  pl.BlockSpec((1, d), const),
                  pl.BlockSpec((d, _IN_WIDTH), const),
                  pl.BlockSpec((1, NA_WIDTH), const),
                  pl.BlockSpec((1, NA_WIDTH), const),
                  pl.BlockSpec((NA_WIDTH, NA_WIDTH), const),
                  pl.BlockSpec((tm, RET_QK_WIDTH), lambda i: (i % pos_tiles, 0)),
                  pl.BlockSpec((tm, RET_QK_WIDTH), lambda i: (i % pos_tiles, 0))],
        out_specs=[pl.BlockSpec((tm, w), row) for w in widths],
        out_shape=[jax.ShapeDtypeStruct((t, w), BF16) for w in widths],
        compiler_params=_cparams("arbitrary"),
        name="inproj",
    )(x2d, mods, gain, w_bf, qg, kg, gmat, cos, sin)


def _na_kernel(q_ref, k0, k1, k2, k3, v0, v1, v2, v3, kx_ref, vx_ref, bias_ref, o_ref):
    q2 = q_ref[...]
    lane = lax.broadcasted_iota(jnp.int32, (1, LANES), 1)
    ks = (k0, k1, k2, k3)
    vs = (v0, v1, v2, v3)
    kw = k0.shape[0]
    outs = []
    for hh in range(2):
        sel = (lane // NA_HEAD_DIM) == hh
        qm = jnp.where(sel, q2, jnp.zeros_like(q2))
        s = [_dot_nt(qm, ks[j][...]) + bias_ref[0, hh, :, j * kw:(j + 1) * kw].astype(F32)
             for j in range(4)]
        s.append(_dot_nt(qm, kx_ref[...]))
        m = s[0]
        for t in s[1:-1]:
            m = jnp.maximum(m, t)
        m = jnp.maximum(jnp.max(m, axis=-1, keepdims=True), jnp.max(s[-1], axis=-1, keepdims=True))
        den = jnp.zeros_like(m)
        acc = jnp.zeros((q2.shape[0], LANES), F32)
        vals = vs + (vx_ref,)
        for j in range(5):
            p = jnp.exp(s[j] - m)
            den = den + jnp.sum(p, axis=-1, keepdims=True)
            acc = acc + _dot(p.astype(BF16), vals[j][...])
        outs.append(acc / den)
    o_ref[...] = jnp.where((lane // NA_HEAD_DIM) == 0, outs[0], outs[1]).astype(o_ref.dtype)


def _na_bias_table(rpb, rows):
    h = rpb.shape[0]
    nqb = rows // NA_QROWS
    row_sel = np.zeros((3, NA_QROWS, NA_KROWS, 2 * NA_WIN_ROWS - 1), np.float32)
    row_ok = np.zeros((3, NA_QROWS, NA_KROWS), bool)
    for v, qb in enumerate((0, min(1, nqb - 1), nqb - 1)):
        start = int(np.clip(NA_QROWS * qb - NA_WIN_ROWS // 2, 0, rows - NA_KROWS))
        for rq in range(NA_QROWS):
            r = NA_QROWS * qb + rq
            r0 = int(np.clip(r - NA_WIN_ROWS // 2, 0, rows - NA_WIN_ROWS))
            for rk in range(NA_KROWS):
                kr = start + rk
                if r0 <= kr < r0 + NA_WIN_ROWS:
                    row_ok[v, rq, rk] = True
                    row_sel[v, rq, rk, kr - r + NA_WIN_ROWS - 1] = 1.0
    col_sel = np.zeros((GRID_W, GRID_W, 2 * NA_WIN_COLS - 1), np.float32)
    col_ok = np.zeros((GRID_W, GRID_W), bool)
    for c in range(GRID_W):
        c0 = int(np.clip(c - NA_WIN_COLS // 2, 0, GRID_W - NA_WIN_COLS))
        for kc in range(c0, c0 + NA_WIN_COLS):
            col_ok[c, kc] = True
            col_sel[c, kc, kc - c + NA_WIN_COLS - 1] = 1.0
    t = jnp.einsum('vqkr,hrc->vhqkc', jnp.asarray(row_sel), rpb.astype(F32), precision=HIGHEST)
    t = jnp.einsum('vhqkc,xyc->vhqxky', t, jnp.asarray(col_sel), precision=HIGHEST)
    ok = row_ok[:, None, :, None, :, None] & col_ok[None, None, None, :, None, :]
    t = jnp.where(jnp.asarray(ok), t, NEG_BIG)
    return t.reshape(3, h, NA_QROWS * GRID_W, NA_KROWS * GRID_W).astype(BF16)


def _na_attention(naq, nak, nav, cx_nak, cx_nav, bias, batch, n, ctx_len):
    rows = n // GRID_W
    qn = NA_QROWS * GRID_W
    kw = NA_KROWS * GRID_W // 4
    nqb = rows // NA_QROWS
    kblocks = n // kw
    npairs = NA_WIDTH // LANES

    def kmap(j):
        def f(hp, qb, b):
            start = jnp.clip(2 * qb - 1, 0, kblocks - 4)
            return (b * kblocks + start + j, hp)
        return f

    def bias_map(hp, qb, b):
        v = jnp.where(qb == 0, 0, jnp.where(qb == nqb - 1, 2, 1))
        return (v, hp, 0, 0)

    kv_specs = [pl.BlockSpec((kw, LANES), kmap(j)) for j in range(4)]
    return pl.pallas_call(
        _na_kernel,
        grid=(npairs, nqb, batch),
        in_specs=[pl.BlockSpec((qn, LANES), lambda hp, qb, b: (b * nqb + qb, hp))]
                 + kv_specs + kv_specs
                 + [pl.BlockSpec((ctx_len, LANES), lambda hp, qb, b: (b, hp)),
                    pl.BlockSpec((ctx_len, LANES), lambda hp, qb, b: (b, hp)),
                    pl.BlockSpec((1, 2, qn, 4 * kw), bias_map)],
        out_specs=pl.BlockSpec((qn, LANES), lambda hp, qb, b: (b * nqb + qb, hp)),
        out_shape=jax.ShapeDtypeStruct((batch * n, NA_WIDTH), BF16),
        compiler_params=_cparams("arbitrary", "arbitrary", "arbitrary"),
        name="na_attention",
    )(naq, nak, nak, nak, nak, nav, nav, nav, nav, cx_nak, cx_nav, bias)


def _ret_kernel(reverse, lg_ref, q_ref, k_ref, v_ref, kx_ref, vx_ref, *rest):
    if reverse:
        of_ref, gate_ref, o_ref, s_ref, d_ref, xi_ref, zeta_ref = rest
    else:
        o_ref, s_ref, d_ref, xi_ref, zeta_ref = rest
    c = q_ref.shape[0]
    first = (pl.program_id(0) == 0) & (pl.program_id(1) == 0)
    lane = lax.broadcasted_iota(jnp.int32, (1, LANES), 1)

    @pl.when(first)
    def _():
        i = lax.broadcasted_iota(jnp.int32, (c, c), 0)
        j = lax.broadcasted_iota(jnp.int32, (c, c), 1)
        diff = (j - i) if reverse else (i - j)
        difff = jnp.maximum(diff, 0).astype(F32)
        r = lax.broadcasted_iota(jnp.int32, (c, LANES), 0).astype(F32)
        for h in range(RET_HEADS):
            lg = lg_ref[h]
            d_ref[h] = jnp.where(diff >= 0, jnp.exp(difff * lg), 0.0)
            xi_ref[h] = jnp.exp(((c - r) if reverse else (r + 1.0)) * lg)
        for p in range(RET_HEADS // 2):
            lgp = jnp.where(lane < RET_DK, lg_ref[2 * p], lg_ref[2 * p + 1])
            zeta_ref[p] = jnp.exp((r if reverse else (c - 1.0 - r)) * lgp)

    @pl.when(pl.program_id(1) == 0)
    def _():
        l = kx_ref.shape[0]
        r = lax.broadcasted_iota(jnp.int32, (l, LANES), 0).astype(F32)
        for p in range(RET_HEADS // 2):
            lgp = jnp.where(lane < RET_DK, lg_ref[2 * p], lg_ref[2 * p + 1])
            zc = jnp.exp((r if reverse else (l - 1.0 - r)) * lgp)
            kz = kx_ref[:, p * LANES:(p + 1) * LANES].astype(F32) * zc
            for hh in range(2):
                h = 2 * p + hh
                km = jnp.where((lane // RET_DK) == hh, kz, 0.0).astype(BF16)
                s_ref[h] = _dot_tn(km, vx_ref[:, h * RET_DV:(h + 1) * RET_DV])

    outs = []
    for p in range(RET_HEADS // 2):
        q2 = q_ref[:, p * LANES:(p + 1) * LANES]
        k2 = k_ref[:, p * LANES:(p + 1) * LANES]
        kz = k2.astype(F32) * zeta_ref[p]
        for hh in range(2):
            h = 2 * p + hh
            sel = (lane // RET_DK) == hh
            qm = jnp.where(sel, q2, jnp.zeros_like(q2))
            km = jnp.where(sel, k2, jnp.zeros_like(k2))
            vh = v_ref[:, h * RET_DV:(h + 1) * RET_DV]
            inner = (_dot_nt(qm, km) * d_ref[h]).astype(BF16)
            state = s_ref[h]
            o = _dot(inner, vh) + _dot(qm, state.astype(BF16)) * xi_ref[h]
            g_chunk = jnp.exp(jnp.full((1, LANES), c, F32) * lg_ref[h])
            s_ref[h] = state * g_chunk + _dot_tn(jnp.where(sel, kz, 0.0).astype(BF16), vh)
            outs.append(o)
    if reverse:
        for h in range(RET_HEADS):
            tot = outs[h] + of_ref[:, h * RET_DV:(h + 1) * RET_DV]
            ms = jnp.mean(tot * tot, axis=-1, keepdims=True)
            gate = gate_ref[:, h * RET_DV:(h + 1) * RET_DV].astype(F32)
            o_ref[:, h * RET_DV:(h + 1) * RET_DV] = (tot * lax.rsqrt(ms + NORM_EPS) * gate).astype(o_ref.dtype)
    else:
        for h in range(RET_HEADS):
            o_ref[:, h * RET_DV:(h + 1) * RET_DV] = outs[h]


def _retention_pass(reverse, log_g, rq, rk, rv, cx_rk, cx_rv, batch, n, ctx_len, o_fwd=None, gate=None):
    c = min(RET_CHUNK, n)
    nc = n // c
    if reverse:
        seq = lambda b, i: (b * nc + nc - 1 - i, 0)
    else:
        seq = lambda b, i: (b * nc + i, 0)
    ctx = lambda b, i: (b, 0)
    in_specs = [pl.BlockSpec(memory_space=pltpu.SMEM),
                pl.BlockSpec((c, RET_QK_WIDTH), seq),
                pl.BlockSpec((c, RET_QK_WIDTH), seq),
                pl.BlockSpec((c, RET_V_WIDTH), seq),
                pl.BlockSpec((ctx_len, RET_QK_WIDTH), ctx),
                pl.BlockSpec((ctx_len, RET_V_WIDTH), ctx)]
    args = [log_g, rq, rk, rv, cx_rk, cx_rv]
    if reverse:
        in_specs += [pl.BlockSpec((c, RET_V_WIDTH), seq), pl.BlockSpec((c, RET_V_WIDTH), seq)]
        args += [o_fwd, gate]
    return pl.pallas_call(
        functools.partial(_ret_kernel, reverse),
        grid=(batch, nc),
        in_specs=in_specs,
        out_specs=pl.BlockSpec((c, RET_V_WIDTH), seq),
        out_shape=jax.ShapeDtypeStruct((batch * n, RET_V_WIDTH), BF16 if reverse else F32),
        scratch_shapes=[pltpu.VMEM((RET_HEADS, LANES, RET_DV), F32),
                        pltpu.VMEM((RET_HEADS, c, c), F32),
                        pltpu.VMEM((RET_HEADS, c, RET_DV), F32),
                        pltpu.VMEM((RET_HEADS // 2, c, LANES), F32)],
        compiler_params=_cparams("arbitrary", "arbitrary"),
        name="retention_bwd" if reverse else "retention_fwd",
    )(*args)


def _outproj_kernel(x_ref, ona_ref, oret_ref, w_ref, mod_ref, gain_ref, x1_ref, h_ref):
    mix = _dot(ona_ref[...], w_ref[0:NA_WIDTH, :]) + _dot(oret_ref[...], w_ref[NA_WIDTH:, :])
    x1 = x_ref[...] + mod_ref[0, 2:3, :] * mix
    x1_ref[...] = x1
    h_ref[...] = _rms_mod(x1, gain_ref[...], mod_ref[0, 3:4, :], mod_ref[0, 4:5, :]).astype(BF16)


def _outproj(x2d, ona, oret, w_out_bf, mods, gain, tm, tiles_per_batch):
    t, d = x2d.shape
    row = lambda i: (i, 0)
    const = lambda i: (0, 0)
    return pl.pallas_call(
        _outproj_kernel,
        grid=(t // tm,),
        in_specs=[pl.BlockSpec((tm, d), row),
                  pl.BlockSpec((tm, NA_WIDTH), row),
                  pl.BlockSpec((tm, RET_V_WIDTH), row),
                  pl.BlockSpec((NA_WIDTH + RET_V_WIDTH, d), const),
                  pl.BlockSpec((1, 6, d), lambda i: (i // tiles_per_batch, 0, 0)),
                  pl.BlockSpec((1, d), const)],
        out_specs=[pl.BlockSpec((tm, d), row), pl.BlockSpec((tm, d), row)],
        out_shape=[jax.ShapeDtypeStruct((t, d), F32), jax.ShapeDtypeStruct((t, d), BF16)],
        compiler_params=_cparams("arbitrary"),
        name="outproj",
    )(x2d, ona, oret, w_out_bf, mods, gain)


def _swiglu_kernel(h_ref, x1_ref, wg_ref, wu_ref, wd_ref, mod_ref, o_ref, acc_ref):
    j = pl.program_id(1)

    @pl.when(j == 0)
    def _():
        acc_ref[...] = jnp.zeros_like(acc_ref)

    h = h_ref[...]
    g = _dot(h, wg_ref[...])
    u = _dot(h, wu_ref[...])
    a = (g * _sigmoid(g) * u).astype(BF16)
    acc_ref[...] += _dot(a, wd_ref[...])

    @pl.when(j == pl.num_programs(1) - 1)
    def _():
        o_ref[...] = x1_ref[...] + mod_ref[0, 5:6, :] * acc_ref[...]


def _swiglu(h_bf, x1, w_gate_up_bf, w_down_bf, mods, tm, tiles_per_batch):
    t, d = x1.shape
    f = w_down_bf.shape[0]
    nf = 2 if (f // 2) % LANES == 0 else 1
    tf = f // nf
    return pl.pallas_call(
        _swiglu_kernel,
        grid=(t // tm, nf),
        in_specs=[pl.BlockSpec((tm, d), lambda i, j: (i, 0)),
                  pl.BlockSpec((tm, d), lambda i, j: (i, 0)),
                  pl.BlockSpec((d, tf), lambda i, j: (0, j)),
                  pl.BlockSpec((d, tf), lambda i, j: (0, nf + j)),
                  pl.BlockSpec((tf, d), lambda i, j: (j, 0)),
                  pl.BlockSpec((1, 6, d), lambda i, j: (i // tiles_per_batch, 0, 0))],
        out_specs=pl.BlockSpec((tm, d), lambda i, j: (i, 0)),
        out_shape=jax.ShapeDtypeStruct((t, d), F32),
        scratch_shapes=[pltpu.VMEM((tm, d), F32)],
        compiler_params=_cparams("arbitrary", "arbitrary"),
        name="swiglu",
    )(h_bf, x1, w_gate_up_bf, w_gate_up_bf, w_down_bf, mods)


def _dft_tables(n, d):
    n2 = FFT_N2
    n1 = n // n2
    s = SUBLANES
    eye = np.eye(s)
    ang1 = 2.0 * np.pi * (np.outer(np.arange(n1), np.arange(n1)) % n1) / n1
    f1 = np.stack([np.cos(ang1), -np.sin(ang1)]) / np.sqrt(n)
    f1k = np.einsum('ckn,jl->ckjnl', f1, eye).reshape(2 * n1 * s, n1 * s)
    angt = 2.0 * np.pi * (np.outer(np.arange(n1), np.arange(n2)) % n) / n
    tw = np.stack([np.cos(angt), -np.sin(angt)])
    tw = tw.reshape(2, n1, n2 // s, s).transpose(0, 2, 1, 3).reshape(2, n2 // s, n1 * s, 1)
    tw = np.broadcast_to(tw, (2, n2 // s, n1 * s, LANES))
    ang2 = 2.0 * np.pi * (np.outer(np.arange(n2), np.arange(n2)) % n2) / n2
    cr, ci = np.cos(ang2), -np.sin(ang2)
    m2 = np.array([[cr, -ci], [ci, cr]])
    m2k = np.einsum('abkn,jl->akjbln', m2, eye).reshape(2 * n2 * s, 2 * s * n2)
    gw = d // FOURIER_GROUPS
    angc = 2.0 * np.pi * (np.outer(np.arange(gw), np.arange(gw)) % gw) / gw
    cs = np.concatenate([np.cos(angc), np.sin(angc)], axis=0) / np.sqrt(gw)
    return (f1k.astype(np.float32), tw.astype(np.float32), m2k.astype(np.float32),
            cs.astype(np.float32))


def _fold_kernel(cs_ref, w_ref, o_ref):
    gw = w_ref.shape[0]
    o_ref[0] = jnp.dot(cs_ref[0:gw, :], w_ref[...], precision=HIGHEST, preferred_element_type=F32)
    o_ref[1] = jnp.dot(cs_ref[gw:, :], w_ref[...], precision=HIGHEST, preferred_element_type=F32)


def _fold_channel_dft(cs, w_fourier):
    d = w_fourier.shape[0]
    gw = d // FOURIER_GROUPS
    return pl.pallas_call(
        _fold_kernel,
        grid=(FOURIER_GROUPS,),
        in_specs=[pl.BlockSpec((2 * gw, gw), lambda g: (0, 0)),
                  pl.BlockSpec((gw, d), lambda g: (g, 0))],
        out_specs=pl.BlockSpec((2, gw, d), lambda g: (0, g, 0)),
        out_shape=jax.ShapeDtypeStruct((2, d, d), F32),
        compiler_params=_cparams("arbitrary"),
        name="fold_channel_dft",
    )(cs, w_fourier)


def _fft1_kernel(x_ref, mod_ref, gain_ref, f_ref, tw_ref, o_ref):
    _, n1, s, d = x_ref.shape
    x = x_ref[...].reshape(n1 * s, d)
    h = _rms_mod(x, gain_ref[...], mod_ref[0, 0:1, :], mod_ref[0, 1:2, :]).astype(BF16)
    a = _dot(f_ref[...], h)
    ar = a[:n1 * s]
    ai = a[n1 * s:]
    reps = d // LANES
    twr = jnp.tile(tw_ref[0, 0], (1, reps))
    twi = jnp.tile(tw_ref[1, 0], (1, reps))
    o_ref[0, 0] = (ar * twr - ai * twi).reshape(n1, s, d)
    o_ref[0, 1] = (ar * twi + ai * twr).reshape(n1, s, d)


def _fft_stage1(x2, mods, gain, f1k_bf, tw, batch, n):
    d = x2.shape[-1]
    n2 = FFT_N2
    n1 = n // n2
    s = SUBLANES
    xv = x2.reshape(batch, n1, n2, d)
    return pl.pallas_call(
        _fft1_kernel,
        grid=(n2 // s, batch),
        in_specs=[pl.BlockSpec((1, n1, s, d), lambda g, b: (b, 0, g, 0)),
                  pl.BlockSpec((1, 6, d), lambda g, b: (b, 0, 0)),
                  pl.BlockSpec((1, d), lambda g, b: (0, 0)),
                  pl.BlockSpec((2 * n1 * s, n1 * s), lambda g, b: (0, 0)),
                  pl.BlockSpec((2, 1, n1 * s, LANES), lambda g, b: (0, g, 0, 0))],
        out_specs=pl.BlockSpec((1, 2, n1, s, d), lambda g, b: (b, 0, 0, g, 0)),
        out_shape=jax.ShapeDtypeStruct((batch, 2, n1, n2, d), F32),
        compiler_params=_cparams("arbitrary", "arbitrary"),
        name="fft_stage1",
    )(xv, mods, gain, f1k_bf, tw)


def _fft2_kernel(n_experts, b_ref, x_ref, m_ref, wf_ref, mod_ref, gain_ref, wr_ref,
                 x3_ref, h2_ref, route_ref):
    _, _, s, n2, d = b_ref.shape
    rows = n2 * s
    bmat = b_ref[...].reshape(2 * s * n2, d).astype(BF16)
    z = _dot(m_ref[...], bmat)
    y = _dot(z[:rows].astype(BF16), wf_ref[0]) + _dot(z[rows:].astype(BF16), wf_ref[1])
    x3 = x_ref[...].reshape(rows, d) + mod_ref[0, 2:3, :] * y
    x3_ref[...] = x3.reshape(x3_ref.shape)
    h2 = _rms_mod(x3, gain_ref[...], mod_ref[0, 3:4, :], mod_ref[0, 4:5, :])
    h2_ref[...] = h2.reshape(h2_ref.shape)
    logits = jnp.dot(h2, wr_ref[...], precision=HIGHEST, preferred_element_type=F32)
    lane = lax.broadcasted_iota(jnp.int32, logits.shape, 1).astype(F32)
    lg = jnp.where(lane < n_experts, logits, -jnp.inf)
    m1 = jnp.max(lg, axis=-1, keepdims=True)
    i1 = jnp.min(jnp.where(lg == m1, lane, float(LANES)), axis=-1, keepdims=True)
    lg2 = jnp.where(lane == i1, -jnp.inf, lg)
    m2 = jnp.max(lg2, axis=-1, keepdims=True)
    i2 = jnp.min(jnp.where(lg2 == m2, lane, float(LANES)), axis=-1, keepdims=True)
    e = jnp.exp(m2 - m1)
    g1 = 1.0 / (1.0 + e)
    g2 = e / (1.0 + e)
    route = jnp.where(lane == 0, g1, jnp.where(lane == 1, g2, jnp.where(lane == 2, i1,
                                                                       jnp.where(lane == 3, i2, 0.0))))
    route_ref[...] = route.reshape(route_ref.shape)


def _fft_stage2(bmid, x2, m2k_bf, wf_bf, mods, gain, wr_pad, n_experts, batch, n):
    d = x2.shape[-1]
    n2 = FFT_N2
    n1 = n // n2
    s = SUBLANES
    xv = x2.reshape(batch, n2, n1, d)
    pos = lambda g, b: (b, 0, g, 0)
    const2 = lambda g, b: (0, 0)
    x3, h2, route = pl.pallas_call(
        functools.partial(_fft2_kernel, n_experts),
        grid=(n1 // s, batch),
        in_specs=[pl.BlockSpec((1, 2, s, n2, d), lambda g, b: (b, 0, g, 0, 0)),
                  pl.BlockSpec((1, n2, s, d), pos),
                  pl.BlockSpec((2 * n2 * s, 2 * n2 * s), const2),
                  pl.BlockSpec((2, d, d), lambda g, b: (0, 0, 0)),
                  pl.BlockSpec((1, 6, d), lambda g, b: (b, 0, 0)),
                  pl.BlockSpec((1, d), const2),
                  pl.BlockSpec((d, LANES), const2)],
        out_specs=[pl.BlockSpec((1, n2, s, d), pos),
                   pl.BlockSpec((1, n2, s, d), pos),
                   pl.BlockSpec((1, n2, s, LANES), pos)],
        out_shape=[jax.ShapeDtypeStruct((batch, n2, n1, d), F32),
                   jax.ShapeDtypeStruct((batch, n2, n1, d), F32),
                   jax.ShapeDtypeStruct((batch, n2, n1, LANES), F32)],
        compiler_params=_cparams("arbitrary", "arbitrary"),
        name="fft_stage2",
    )(bmid, xv, m2k_bf, wf_bf, mods, gain, wr_pad)
    t = batch * n
    return x3.reshape(t, d), h2.reshape(t, d), route.reshape(t, LANES)


def _rowcopy_kernel(chunk, sidx_ref, didx_ref, src_ref, *rest):
    dst_ref, sem = rest[-2], rest[-1]
    base = pl.program_id(0) * chunk

    def body(i, carry):
        s = sidx_ref[base + i]
        t = didx_ref[base + i]
        pltpu.make_async_copy(src_ref.at[pl.ds(s, 1)], dst_ref.at[pl.ds(t, 1)], sem).start()
        return carry

    lax.fori_loop(0, chunk, body, 0, unroll=8)

    def drain(i, carry):
        pltpu.make_async_copy(src_ref.at[pl.ds(0, 1)], dst_ref.at[pl.ds(0, 1)], sem).wait()
        return carry

    lax.fori_loop(0, chunk, drain, 0, unroll=8)


def _rowcopy(src, sidx, didx, n_dst, dst_init=None):
    n = sidx.shape[0]
    chunk = min(ROWCOPY_CHUNK, n)
    d = src.shape[1]
    in_specs = [pl.BlockSpec(memory_space=pl.ANY)]
    args = [sidx, didx, src]
    aliases = {}
    if dst_init is not None:
        in_specs.append(pl.BlockSpec(memory_space=pl.ANY))
        args.append(dst_init)
        aliases = {3: 0}
    return pl.pallas_call(
        functools.partial(_rowcopy_kernel, chunk),
        grid_spec=pltpu.PrefetchScalarGridSpec(
            num_scalar_prefetch=2,
            grid=(n // chunk,),
            in_specs=in_specs,
            out_specs=pl.BlockSpec(memory_space=pl.ANY),
            scratch_shapes=[pltpu.SemaphoreType.DMA(())]),
        out_shape=jax.ShapeDtypeStruct((n_dst, d), src.dtype),
        input_output_aliases=aliases,
        compiler_params=pltpu.CompilerParams(dimension_semantics=("arbitrary",), has_side_effects=True),
        name="rowcopy",
    )(*args)


def _expert_kernel(bexp_ref, nused_ref, x_ref, wg_ref, wu_ref, wd_ref, o_ref, xb_ref, acc_ref):
    i = pl.program_id(0)
    j = pl.program_id(1)
    last = pl.num_programs(1) - 1
    used = i < nused_ref[0]

    @pl.when(used)
    def _():
        @pl.when(j == 0)
        def _():
            xb_ref[...] = x_ref[...].astype(BF16)
            acc_ref[...] = jnp.zeros_like(acc_ref)

        xb = xb_ref[...]
        g = _dot(xb, wg_ref[0])
        u = _dot(xb, wu_ref[0])
        a = (g * _sigmoid(g) * u).astype(BF16)
        acc_ref[...] += _dot(a, wd_ref[0])

        @pl.when(j == last)
        def _():
            o_ref[...] = acc_ref[...]

    @pl.when(jnp.logical_not(used) & (j == last))
    def _():
        o_ref[...] = jnp.zeros_like(o_ref)


def _expert_ffn(xs, w_gu_bf, w_dn_bf, block_expert, n_used):
    n_slots, d = xs.shape
    e, f, _ = w_dn_bf.shape
    nb = n_slots // MOE_BLOCK
    nf = 4 if (f // 4) % LANES == 0 else 1
    tf = f // nf

    def xmap(i, j, be, nu):
        return (jnp.minimum(i, nu[0] - 1), 0)

    def jeff(i, j, nu):
        return jnp.where(i < nu[0], j, nf - 1)

    return pl.pallas_call(
        _expert_kernel,
        grid_spec=pltpu.PrefetchScalarGridSpec(
            num_scalar_prefetch=2,
            grid=(nb, nf),
            in_specs=[pl.BlockSpec((MOE_BLOCK, d), xmap),
                      pl.BlockSpec((1, d, tf), lambda i, j, be, nu: (be[i], 0, jeff(i, j, nu))),
                      pl.BlockSpec((1, d, tf), lambda i, j, be, nu: (be[i], 0, nf + jeff(i, j, nu))),
                      pl.BlockSpec((1, tf, d), lambda i, j, be, nu: (be[i], jeff(i, j, nu), 0))],
            out_specs=pl.BlockSpec((MOE_BLOCK, d), lambda i, j, be, nu: (i, 0)),
            scratch_shapes=[pltpu.VMEM((MOE_BLOCK, d), BF16), pltpu.VMEM((MOE_BLOCK, d), F32)]),
        out_shape=jax.ShapeDtypeStruct((n_slots, d), F32),
        compiler_params=_cparams("arbitrary", "arbitrary"),
        name="expert_ffn",
    )(block_expert, n_used, xs, w_gu_bf, w_gu_bf, w_dn_bf)


def _combine_kernel(x_ref, y0_ref, y1_ref, route_ref, mod_ref, o_ref):
    g0 = route_ref[:, 0:1]
    g1 = route_ref[:, 1:2]
    o_ref[...] = x_ref[...] + mod_ref[0, 5:6, :] * (g0 * y0_ref[...] + g1 * y1_ref[...])


def _combine(x3, ycat, route, mods, tm, tiles_per_batch):
    t, d = x3.shape
    nt = t // tm
    row = lambda i: (i, 0)
    return pl.pallas_call(
        _combine_kernel,
        grid=(nt,),
        in_specs=[pl.BlockSpec((tm, d), row),
                  pl.BlockSpec((tm, d), row),
                  pl.BlockSpec((tm, d), lambda i: (nt + i, 0)),
                  pl.BlockSpec((tm, LANES), row),
                  pl.BlockSpec((1, 6, d), lambda i: (i // tiles_per_batch, 0, 0))],
        out_specs=pl.BlockSpec((tm, d), row),
        out_shape=jax.ShapeDtypeStruct((t, d), F32),
        compiler_params=_cparams("arbitrary"),
        name="combine",
    )(x3, ycat, ycat, route, mods)


def _route_slots(route, n_experts):
    t = route.shape[0]
    expert = route[:, 2:2 + TOP_K].astype(jnp.int32).reshape(t * TOP_K)
    onehot = (expert[:, None] == jnp.arange(n_experts, dtype=jnp.int32)[None, :]).astype(jnp.int32)
    csum = jnp.cumsum(onehot, axis=0)
    counts = csum[-1]
    padded = (counts + MOE_BLOCK - 1) // MOE_BLOCK * MOE_BLOCK
    pad_end = jnp.cumsum(padded)
    pad_start = pad_end - padded
    slot = jnp.sum(onehot * (pad_start[None, :] + csum - 1), axis=1).astype(jnp.int32)
    n_slots = t * TOP_K + n_experts * MOE_BLOCK
    nb = n_slots // MOE_BLOCK
    block_expert = jnp.minimum(
        jnp.searchsorted(pad_end, jnp.arange(nb, dtype=jnp.int32) * MOE_BLOCK, side='right'),
        n_experts - 1).astype(jnp.int32)
    n_used = (pad_end[-1:] // MOE_BLOCK).astype(jnp.int32)
    return slot, block_expert, n_used, n_slots


def _rope_tables(n):
    pos = np.arange(n)
    row = (pos // GRID_W).astype(np.float32)
    col = (pos % GRID_W).astype(np.float32)
    n_freq = RET_DK // 4
    inv_freq = (np.float32(ROPE_BASE) ** (-np.arange(n_freq, dtype=np.float32) / n_freq)).astype(np.float32)
    ang = np.concatenate([row[:, None] * inv_freq, col[:, None] * inv_freq], axis=-1)
    cos = np.cos(ang.astype(np.float64))
    sin = np.sin(ang.astype(np.float64))
    cos_h = np.concatenate([cos, cos], axis=-1)
    sin_h = np.concatenate([-sin, sin], axis=-1)
    return (np.tile(cos_h, (1, RET_HEADS)).astype(np.float32),
            np.tile(sin_h, (1, RET_HEADS)).astype(np.float32))


def _permute_in_proj(w_in):
    perm = np.arange(_IN_WIDTH)
    half = np.concatenate([np.arange(0, RET_DK, 2), np.arange(1, RET_DK, 2)])
    for off in (_O_RQ, _O_RK):
        for h in range(RET_HEADS):
            perm[off + h * RET_DK: off + (h + 1) * RET_DK] = off + h * RET_DK + half
    return w_in[:, perm]


def kernel(x, c, ctx, c_ctx, l0_norm_mix, l0_norm_ffn, l0_w_ada, l0_b_ada, l0_w_in, l0_q_gain, l0_k_gain, l0_rpb, l0_ret_decay_fwd, l0_ret_decay_bwd, l0_w_out, l0_w_gate_up, l0_w_down, l1_norm_mix, l1_norm_ffn, l1_w_ada, l1_b_ada, l1_w_fourier, l1_w_router, l1_w_exp_gate_up, l1_w_exp_down):
    batch, n, d = x.shape
    ctx_len = ctx.shape[1]
    t = batch * n
    rows = n // GRID_W
    n_experts = l1_w_router.shape[1]
    assert n % (NA_QROWS * GRID_W) == 0 and rows >= NA_KROWS and n % FFT_N2 == 0
    assert l0_w_in.shape[1] == _IN_WIDTH and d % LANES == 0

    x2d = x.reshape(t, d)
    ctx2d = ctx.reshape(batch * ctx_len, d)

    ada0 = _ada(jnp.concatenate([c, c_ctx[None, :]], axis=0), l0_w_ada, l0_b_ada)
    mods0 = ada0[:batch].reshape(batch, 6, d)
    mods_ctx = ada0[batch:batch + 1].reshape(1, 6, d)
    mods1 = _ada(c, l1_w_ada, l1_b_ada)[:batch].reshape(batch, 6, d)

    w_in_bf = _permute_in_proj(l0_w_in).astype(BF16)
    qg = jnp.tile(l0_q_gain.astype(F32), NA_HEADS).reshape(1, NA_WIDTH)
    kg = jnp.tile(l0_k_gain.astype(F32), NA_HEADS).reshape(1, NA_WIDTH)
    gmat = jnp.asarray(np.kron(np.eye(NA_HEADS), np.full((NA_HEAD_DIM, NA_HEAD_DIM), 1.0 / NA_HEAD_DIM)),
                       BF16)
    cos_np, sin_np = _rope_tables(n)
    gain_mix0 = l0_norm_mix.reshape(1, d)
    tm = min(INPROJ_ROWS, n)
    naq, rq, rg, nak, nav, rk, rv = _inproj(
        x2d, mods0, gain_mix0, w_in_bf, qg, kg, gmat, jnp.asarray(cos_np), jnp.asarray(sin_np),
        tm, n // tm, n // tm)
    ctx_rows = batch * ctx_len
    _, _, _, cx_nak, cx_nav, cx_rk, cx_rv = _inproj(
        ctx2d, mods_ctx, gain_mix0, w_in_bf, qg, kg, gmat,
        jnp.ones((ctx_len, RET_QK_WIDTH), F32), jnp.zeros((ctx_len, RET_QK_WIDTH), F32),
        ctx_len, ctx_rows // ctx_len, 1)

    bias = _na_bias_table(l0_rpb, rows)
    o_na = _na_attention(naq, nak, nav, cx_nak, cx_nav, bias, batch, n, ctx_len)

    lg_f = jax.nn.log_sigmoid(l0_ret_decay_fwd.astype(F32))
    lg_b = jax.nn.log_sigmoid(l0_ret_decay_bwd.astype(F32))
    o_f = _retention_pass(False, lg_f, rq, rk, rv, cx_rk, cx_rv, batch, n, ctx_len)
    o_ret = _retention_pass(True, lg_b, rq, rk, rv, cx_rk, cx_rv, batch, n, ctx_len, o_fwd=o_f, gate=rg)

    x1, h_ffn = _outproj(x2d, o_na, o_ret, l0_w_out.astype(BF16), mods0, l0_norm_ffn.reshape(1, d),
                         tm, n // tm)
    tf_rows = min(FFN_ROWS, n)
    x2 = _swiglu(h_ffn, x1, l0_w_gate_up.astype(BF16), l0_w_down.astype(BF16), mods0,
                 tf_rows, n // tf_rows)

    f1k, tw, m2k, cs = _dft_tables(n, d)
    wf = _fold_channel_dft(jnp.asarray(cs), l1_w_fourier).astype(BF16)
    bmid = _fft_stage1(x2, mods1, l1_norm_mix.reshape(1, d), jnp.asarray(f1k, BF16), jnp.asarray(tw),
                       batch, n)
    wr_pad = jnp.zeros((d, LANES), F32).at[:, :n_experts].set(l1_w_router)
    x3, h2, route = _fft_stage2(bmid, x2, jnp.asarray(m2k, BF16), wf, mods1, l1_norm_ffn.reshape(1, d),
                                wr_pad, n_experts, batch, n)

    slot, block_expert, n_used, n_slots = _route_slots(route, n_experts)
    token = jnp.arange(t * TOP_K, dtype=jnp.int32) // TOP_K
    xs = _rowcopy(h2, token, slot, n_slots, dst_init=jnp.zeros((n_slots, d), F32))
    ys = _expert_ffn(xs, l1_w_exp_gate_up.astype(BF16), l1_w_exp_down.astype(BF16), block_expert, n_used)
    slot_km = slot.reshape(t, TOP_K).T.reshape(t * TOP_K)
    ycat = _rowcopy(ys, slot_km, jnp.arange(t * TOP_K, dtype=jnp.int32), t * TOP_K)
    out = _combine(x3, ycat, route, mods1, tm, n // tm)
    return out.reshape(batch, n, d)
```

```python
import functools

import numpy as np
import jax
import jax.numpy as jnp
from jax import lax
from jax.experimental import pallas as pl
from jax.experimental.pallas import tpu as pltpu

F32 = jnp.float32
BF16 = jnp.bfloat16
HIGHEST = lax.Precision.HIGHEST

GRID_W = 64
NA_HEADS = 8
NA_HEAD_DIM = 64
NA_WIN_ROWS = 8
NA_WIN_COLS = 16
RET_HEADS = 4
RET_DK = 64
RET_DV = 128
ROPE_BASE = 10000.0
FOURIER_GROUPS = 8
TOP_K = 2
NORM_EPS = 1e-6

NA_WIDTH = NA_HEADS * NA_HEAD_DIM
RET_QK_WIDTH = RET_HEADS * RET_DK
RET_V_WIDTH = RET_HEADS * RET_DV

LANES = 128
SUBLANES = 8
VMEM_LIMIT = 48 * 1024 * 1024

NEG_BIG = -1e30

INPROJ_ROWS = 512
NA_QROWS = 8
NA_KROWS = 16
RET_CHUNK = 512
FFN_ROWS = 1024
FFT_N2 = 64
MOE_BLOCK = 1024
ROWCOPY_CHUNK = 2048


def _cparams(*sem):
    return pltpu.CompilerParams(dimension_semantics=sem, vmem_limit_bytes=VMEM_LIMIT)


def _sigmoid(x):
    return 1.0 / (1.0 + jnp.exp(-x))


def _dot(a, b):
    return jnp.dot(a, b, preferred_element_type=F32)


def _dot_nt(a, b):
    return lax.dot_general(a, b, (((1,), (1,)), ((), ())), preferred_element_type=F32)


def _dot_tn(a, b):
    return lax.dot_general(a, b, (((0,), (0,)), ((), ())), preferred_element_type=F32)


def _rms_mod(x, gain, shift, scale):
    ms = jnp.mean(x * x, axis=-1, keepdims=True)
    return (x * lax.rsqrt(ms + NORM_EPS) * gain) * (1.0 + scale) + shift


def _ada_kernel(c_ref, w_ref, b_ref, o_ref):
    c = c_ref[...]
    s = c * _sigmoid(c)
    o_ref[...] = jnp.dot(s, w_ref[...], precision=HIGHEST, preferred_element_type=F32) + b_ref[...]


def _ada(cond, w_ada, b_ada):
    d, n = w_ada.shape
    cond8 = jnp.zeros((SUBLANES, d), F32).at[:cond.shape[0]].set(cond)
    tn = 1024
    return pl.pallas_call(
        _ada_kernel,
        grid=(n // tn,),
        in_specs=[pl.BlockSpec((SUBLANES, d), lambda j: (0, 0)),
                  pl.BlockSpec((d, tn), lambda j: (0, j)),
                  pl.BlockSpec((1, tn), lambda j: (0, j))],
        out_specs=pl.BlockSpec((SUBLANES, tn), lambda j: (0, j)),
        out_shape=jax.ShapeDtypeStruct((SUBLANES, n), F32),
        compiler_params=_cparams("arbitrary"),
        name="ada",
    )(cond8, w_ada, b_ada.reshape(1, n))


_O_NAQ = 0
_O_RQ = _O_NAQ + NA_WIDTH
_O_RG = _O_RQ + RET_QK_WIDTH
_O_NAK = _O_RG + RET_V_WIDTH
_O_NAV = _O_NAK + NA_WIDTH
_O_RK = _O_NAV + NA_WIDTH
_O_RV = _O_RK + RET_QK_WIDTH
_IN_WIDTH = _O_RV + RET_V_WIDTH


def _inproj_kernel(x_ref, mod_ref, gain_ref, w_ref, qg_ref, kg_ref, gmat_ref, cos_ref, sin_ref,
                   naq_ref, rq_ref, rg_ref, nak_ref, nav_ref, rk_ref, rv_ref):
    h = _rms_mod(x_ref[...], gain_ref[...], mod_ref[0, 0:1, :], mod_ref[0, 1:2, :]).astype(BF16)
    p = _dot(h, w_ref[...])
    gmat = gmat_ref[...]

    def headnorm(t, g):
        ms = _dot((t * t).astype(BF16), gmat)
        return t * lax.rsqrt(ms + NORM_EPS) * g

    cos = cos_ref[...]
    sin = sin_ref[...]
    lane = lax.broadcasted_iota(jnp.int32, (1, RET_QK_WIDTH), 1)
    first_half = (lane % RET_DK) < (RET_DK // 2)

    def rope(t):
        partner = jnp.where(first_half,
                            pltpu.roll(t, RET_QK_WIDTH - RET_DK // 2, axis=1),
                            pltpu.roll(t, RET_DK // 2, axis=1))
        return t * cos + partner * sin

    naq_ref[...] = (headnorm(p[:, _O_NAQ:_O_RQ], qg_ref[...]) * (NA_HEAD_DIM ** -0.5)).astype(BF16)
    rq_ref[...] = rope(p[:, _O_RQ:_O_RG]).astype(BF16)
    g = p[:, _O_RG:_O_NAK]
    rg_ref[...] = (g * _sigmoid(g)).astype(BF16)
    nak_ref[...] = headnorm(p[:, _O_NAK:_O_NAV], kg_ref[...]).astype(BF16)
    nav_ref[...] = p[:, _O_NAV:_O_RK].astype(BF16)
    rk_ref[...] = (rope(p[:, _O_RK:_O_RV]) * (RET_DK ** -0.5)).astype(BF16)
    rv_ref[...] = p[:, _O_RV:_IN_WIDTH].astype(BF16)


def _inproj(x2d, mods, gain, w_bf, qg, kg, gmat, cos, sin, tm, tiles_per_mod, pos_tiles):
    t, d = x2d.shape
    widths = (NA_WIDTH, RET_QK_WIDTH, RET_V_WIDTH, NA_WIDTH, NA_WIDTH, RET_QK_WIDTH, RET_V_WIDTH)
    row = lambda i: (i, 0)
    const = lambda i: (0, 0)
    return pl.pallas_call(
        _inproj_kernel,
        grid=(t // tm,),
        in_specs=[pl.BlockSpec((tm, d), row),
                  pl.BlockSpec((1, 6, d), lambda i: (i // tiles_per_mod, 0, 0)),
                  pl.BlockSpec((1, d), const),
                  pl.BlockSpec((d, _IN_WIDTH), const),
                  pl.BlockSpec((1, NA_WIDTH), const),
                  pl.BlockSpec((1, NA_WIDTH), const),
                  pl.BlockSpec((NA_WIDTH, NA_WIDTH), const),
                  pl.BlockSpec((tm, RET_QK_WIDTH), lambda i: (i % pos_tiles, 0)),
                  pl.BlockSpec((tm, RET_QK_WIDTH), lambda i: (i % pos_tiles, 0))],
        out_specs=[pl.BlockSpec((tm, w), row) for w in widths],
        out_shape=[jax.ShapeDtypeStruct((t, w), BF16) for w in widths],
        compiler_params=_cparams("arbitrary"),
        name="inproj",
    )(x2d, mods, gain, w_bf, qg, kg, gmat, cos, sin)


def _na_kernel(q_ref, k0, k1, k2, k3, v0, v1, v2, v3, kx_ref, vx_ref, bias_ref, o_ref):
    q2 = q_ref[...]
    lane = lax.broadcasted_iota(jnp.int32, (1, LANES), 1)
    ks = (k0, k1, k2, k3)
    vs = (v0, v1, v2, v3)
    kw = k0.shape[0]
    outs = []
    for hh in range(2):
        sel = (lane // NA_HEAD_DIM) == hh
        qm = jnp.where(sel, q2, jnp.zeros_like(q2))
        s = [_dot_nt(qm, ks[j][...]) + bias_ref[0, hh, :, j * kw:(j + 1) * kw].astype(F32)
             for j in range(4)]
        s.append(_dot_nt(qm, kx_ref[...]))
        m = s[0]
        for t in s[1:-1]:
            m = jnp.maximum(m, t)
        m = jnp.maximum(jnp.max(m, axis=-1, keepdims=True), jnp.max(s[-1], axis=-1, keepdims=True))
        den = jnp.zeros_like(m)
        acc = jnp.zeros((q2.shape[0], LANES), F32)
        vals = vs + (vx_ref,)
        for j in range(5):
            p = jnp.exp(s[j] - m)
            den = den + jnp.sum(p, axis=-1, keepdims=True)
            acc = acc + _dot(p.astype(BF16), vals[j][...])
        outs.append(acc / den)
    o_ref[...] = jnp.where((lane // NA_HEAD_DIM) == 0, outs[0], outs[1]).astype(o_ref.dtype)


def _na_bias_table(rpb, rows):
    h = rpb.shape[0]
    nqb = rows // NA_QROWS
    row_sel = np.zeros((3, NA_QROWS, NA_KROWS, 2 * NA_WIN_ROWS - 1), np.float32)
    row_ok = np.zeros((3, NA_QROWS, NA_KROWS), bool)
    for v, qb in enumerate((0, min(1, nqb - 1), nqb - 1)):
        start = int(np.clip(NA_QROWS * qb - NA_WIN_ROWS // 2, 0, rows - NA_KROWS))
        for rq in range(NA_QROWS):
            r = NA_QROWS * qb + rq
            r0 = int(np.clip(r - NA_WIN_ROWS // 2, 0, rows - NA_WIN_ROWS))
            for rk in range(NA_KROWS):
                kr = start + rk
                if r0 <= kr < r0 + NA_WIN_ROWS:
                    row_ok[v, rq, rk] = True
                    row_sel[v, rq, rk, kr - r + NA_WIN_ROWS - 1] = 1.0
    col_sel = np.zeros((GRID_W, GRID_W, 2 * NA_WIN_COLS - 1), np.float32)
    col_ok = np.zeros((GRID_W, GRID_W), bool)
    for c in range(GRID_W):
        c0 = int(np.clip(c - NA_WIN_COLS // 2, 0, GRID_W - NA_WIN_COLS))
        for kc in range(c0, c0 + NA_WIN_COLS):
            col_ok[c, kc] = True
            col_sel[c, kc, kc - c + NA_WIN_COLS - 1] = 1.0
    t = jnp.einsum('vqkr,hrc->vhqkc', jnp.asarray(row_sel), rpb.astype(F32), precision=HIGHEST)
    t = jnp.einsum('vhqkc,xyc->vhqxky', t, jnp.asarray(col_sel), precision=HIGHEST)
    ok = row_ok[:, None, :, None, :, None] & col_ok[None, None, None, :, None, :]
    t = jnp.where(jnp.asarray(ok), t, NEG_BIG)
    return t.reshape(3, h, NA_QROWS * GRID_W, NA_KROWS * GRID_W).astype(BF16)


def _na_attention(naq, nak, nav, cx_nak, cx_nav, bias, batch, n, ctx_len):
    rows = n // GRID_W
    qn = NA_QROWS * GRID_W
    kw = NA_KROWS * GRID_W // 4
    nqb = rows // NA_QROWS
    kblocks = n // kw
    npairs = NA_WIDTH // LANES

    def kmap(j):
        def f(hp, qb, b):
            start = jnp.clip(2 * qb - 1, 0, kblocks - 4)
            return (b * kblocks + start + j, hp)
        return f

    def bias_map(hp, qb, b):
        v = jnp.where(qb == 0, 0, jnp.where(qb == nqb - 1, 2, 1))
        return (v, hp, 0, 0)

    kv_specs = [pl.BlockSpec((kw, LANES), kmap(j)) for j in range(4)]
    return pl.pallas_call(
        _na_kernel,
        grid=(npairs, nqb, batch),
        in_specs=[pl.BlockSpec((qn, LANES), lambda hp, qb, b: (b * nqb + qb, hp))]
                 + kv_specs + kv_specs
                 + [pl.BlockSpec((ctx_len, LANES), lambda hp, qb, b: (b, hp)),
                    pl.BlockSpec((ctx_len, LANES), lambda hp, qb, b: (b, hp)),
                    pl.BlockSpec((1, 2, qn, 4 * kw), bias_map)],
        out_specs=pl.BlockSpec((qn, LANES), lambda hp, qb, b: (b * nqb + qb, hp)),
        out_shape=jax.ShapeDtypeStruct((batch * n, NA_WIDTH), BF16),
        compiler_params=_cparams("arbitrary", "arbitrary", "arbitrary"),
        name="na_attention",
    )(naq, nak, nak, nak, nak, nav, nav, nav, nav, cx_nak, cx_nav, bias)


def _ret_kernel(reverse, lg_ref, q_ref, k_ref, v_ref, kx_ref, vx_ref, *rest):
    if reverse:
        of_ref, gate_ref, o_ref, s_ref, d_ref, xi_ref, zeta_ref = rest
    else:
        o_ref, s_ref, d_ref, xi_ref, zeta_ref = rest
    c = q_ref.shape[0]
    first = (pl.program_id(0) == 0) & (pl.program_id(1) == 0)
    lane = lax.broadcasted_iota(jnp.int32, (1, LANES), 1)

    @pl.when(first)
    def _():
        i = lax.broadcasted_iota(jnp.int32, (c, c), 0)
        j = lax.broadcasted_iota(jnp.int32, (c, c), 1)
        diff = (j - i) if reverse else (i - j)
        difff = jnp.maximum(diff, 0).astype(F32)
        r = lax.broadcasted_iota(jnp.int32, (c, LANES), 0).astype(F32)
        for h in range(RET_HEADS):
            lg = lg_ref[h]
            d_ref[h] = jnp.where(diff >= 0, jnp.exp(difff * lg), 0.0)
            xi_ref[h] = jnp.exp(((c - r) if reverse else (r + 1.0)) * lg)
        for p in range(RET_HEADS // 2):
            lgp = jnp.where(lane < RET_DK, lg_ref[2 * p], lg_ref[2 * p + 1])
            zeta_ref[p] = jnp.exp((r if reverse else (c - 1.0 - r)) * lgp)

    @pl.when(pl.program_id(1) == 0)
    def _():
        l = kx_ref.shape[0]
        r = lax.broadcasted_iota(jnp.int32, (l, LANES), 0).astype(F32)
        for p in range(RET_HEADS // 2):
            lgp = jnp.where(lane < RET_DK, lg_ref[2 * p], lg_ref[2 * p + 1])
            zc = jnp.exp((r if reverse else (l - 1.0 - r)) * lgp)
            kz = kx_ref[:, p * LANES:(p + 1) * LANES].astype(F32) * zc
            for hh in range(2):
                h = 2 * p + hh
                km = jnp.where((lane // RET_DK) == hh, kz, 0.0).astype(BF16)
                s_ref[h] = _dot_tn(km, vx_ref[:, h * RET_DV:(h + 1) * RET_DV])

    outs = []
    for p in range(RET_HEADS // 2):
        q2 = q_ref[:, p * LANES:(p + 1) * LANES]
        k2 = k_ref[:, p * LANES:(p + 1) * LANES]
        kz = k2.astype(F32) * zeta_ref[p]
        for hh in range(2):
            h = 2 * p + hh
            sel = (lane // RET_DK) == hh
            qm = jnp.where(sel, q2, jnp.zeros_like(q2))
            km = jnp.where(sel, k2, jnp.zeros_like(k2))
            vh = v_ref[:, h * RET_DV:(h + 1) * RET_DV]
            inner = (_dot_nt(qm, km) * d_ref[h]).astype(BF16)
            state = s_ref[h]
            o = _dot(inner, vh) + _dot(qm, state.astype(BF16)) * xi_ref[h]
            g_chunk = jnp.exp(jnp.full((1, LANES), c, F32) * lg_ref[h])
            s_ref[h] = state * g_chunk + _dot_tn(jnp.where(sel, kz, 0.0).astype(BF16), vh)
            outs.append(o)
    if reverse:
        for h in range(RET_HEADS):
            tot = outs[h] + of_ref[:, h * RET_DV:(h + 1) * RET_DV]
            ms = jnp.mean(tot * tot, axis=-1, keepdims=True)
            gate = gate_ref[:, h * RET_DV:(h + 1) * RET_DV].astype(F32)
            o_ref[:, h * RET_DV:(h + 1) * RET_DV] = (tot * lax.rsqrt(ms + NORM_EPS) * gate).astype(o_ref.dtype)
    else:
        for h in range(RET_HEADS):
            o_ref[:, h * RET_DV:(h + 1) * RET_DV] = outs[h]


def _retention_pass(reverse, log_g, rq, rk, rv, cx_rk, cx_rv, batch, n, ctx_len, o_fwd=None, gate=None):
    c = min(RET_CHUNK, n)
    nc = n // c
    if reverse:
        seq = lambda b, i: (b * nc + nc - 1 - i, 0)
    else:
        seq = lambda b, i: (b * nc + i, 0)
    ctx = lambda b, i: (b, 0)
    in_specs = [pl.BlockSpec(memory_space=pltpu.SMEM),
                pl.BlockSpec((c, RET_QK_WIDTH), seq),
                pl.BlockSpec((c, RET_QK_WIDTH), seq),
                pl.BlockSpec((c, RET_V_WIDTH), seq),
                pl.BlockSpec((ctx_len, RET_QK_WIDTH), ctx),
                pl.BlockSpec((ctx_len, RET_V_WIDTH), ctx)]
    args = [log_g, rq, rk, rv, cx_rk, cx_rv]
    if reverse:
        in_specs += [pl.BlockSpec((c, RET_V_WIDTH), seq), pl.BlockSpec((c, RET_V_WIDTH), seq)]
        args += [o_fwd, gate]
    return pl.pallas_call(
        functools.partial(_ret_kernel, reverse),
        grid=(batch, nc),
        in_specs=in_specs,
        out_specs=pl.BlockSpec((c, RET_V_WIDTH), seq),
        out_shape=jax.ShapeDtypeStruct((batch * n, RET_V_WIDTH), BF16 if reverse else F32),
        scratch_shapes=[pltpu.VMEM((RET_HEADS, LANES, RET_DV), F32),
                        pltpu.VMEM((RET_HEADS, c, c), F32),
                        pltpu.VMEM((RET_HEADS, c, RET_DV), F32),
                        pltpu.VMEM((RET_HEADS // 2, c, LANES), F32)],
        compiler_params=_cparams("arbitrary", "arbitrary"),
        name="retention_bwd" if reverse else "retention_fwd",
    )(*args)


def _outproj_kernel(x_ref, ona_ref, oret_ref, w_ref, mod_ref, gain_ref, x1_ref, h_ref):
    mix = _dot(ona_ref[...], w_ref[0:NA_WIDTH, :]) + _dot(oret_ref[...], w_ref[NA_WIDTH:, :])
    x1 = x_ref[...] + mod_ref[0, 2:3, :] * mix
    x1_ref[...] = x1
    h_ref[...] = _rms_mod(x1, gain_ref[...], mod_ref[0, 3:4, :], mod_ref[0, 4:5, :]).astype(BF16)


def _outproj(x2d, ona, oret, w_out_bf, mods, gain, tm, tiles_per_batch):
    t, d = x2d.shape
    row = lambda i: (i, 0)
    const = lambda i: (0, 0)
    return pl.pallas_call(
        _outproj_kernel,
        grid=(t // tm,),
        in_specs=[pl.BlockSpec((tm, d), row),
                  pl.BlockSpec((tm, NA_WIDTH), row),
                  pl.BlockSpec((tm, RET_V_WIDTH), row),
                  pl.BlockSpec((NA_WIDTH + RET_V_WIDTH, d), const),
                  pl.BlockSpec((1, 6, d), lambda i: (i // tiles_per_batch, 0, 0)),
                  pl.BlockSpec((1, d), const)],
        out_specs=[pl.BlockSpec((tm, d), row), pl.BlockSpec((tm, d), row)],
        out_shape=[jax.ShapeDtypeStruct((t, d), F32), jax.ShapeDtypeStruct((t, d), BF16)],
        compiler_params=_cparams("arbitrary"),
        name="outproj",
    )(x2d, ona, oret, w_out_bf, mods, gain)


def _swiglu_kernel(h_ref, x1_ref, wg_ref, wu_ref, wd_ref, mod_ref, o_ref, acc_ref):
    j = pl.program_id(1)

    @pl.when(j == 0)
    def _():
        acc_ref[...] = jnp.zeros_like(acc_ref)

    h = h_ref[...]
    g = _dot(h, wg_ref[...])
    u = _dot(h, wu_ref[...])
    a = (g * _sigmoid(g) * u).astype(BF16)
    acc_ref[...] += _dot(a, wd_ref[...])

    @pl.when(j == pl.num_programs(1) - 1)
    def _():
        o_ref[...] = x1_ref[...] + mod_ref[0, 5:6, :] * acc_ref[...]


def _swiglu(h_bf, x1, w_gate_up_bf, w_down_bf, mods, tm, tiles_per_batch):
    t, d = x1.shape
    f = w_down_bf.shape[0]
    nf = 2 if (f // 2) % LANES == 0 else 1
    tf = f // nf
    return pl.pallas_call(
        _swiglu_kernel,
        grid=(t // tm, nf),
        in_specs=[pl.BlockSpec((tm, d), lambda i, j: (i, 0)),
                  pl.BlockSpec((tm, d), lambda i, j: (i, 0)),
                  pl.BlockSpec((d, tf), lambda i, j: (0, j)),
                  pl.BlockSpec((d, tf), lambda i, j: (0, nf + j)),
                  pl.BlockSpec((tf, d), lambda i, j: (j, 0)),
                  pl.BlockSpec((1, 6, d), lambda i, j: (i // tiles_per_batch, 0, 0))],
        out_specs=pl.BlockSpec((tm, d), lambda i, j: (i, 0)),
        out_shape=jax.ShapeDtypeStruct((t, d), F32),
        scratch_shapes=[pltpu.VMEM((tm, d), F32)],
        compiler_params=_cparams("arbitrary", "arbitrary"),
        name="swiglu",
    )(h_bf, x1, w_gate_up_bf, w_gate_up_bf, w_down_bf, mods)


def _dft_tables(n, d):
    n2 = FFT_N2
    n1 = n // n2
    s = SUBLANES
    eye = np.eye(s)
    ang1 = 2.0 * np.pi * (np.outer(np.arange(n1), np.arange(n1)) % n1) / n1
    f1 = np.stack([np.cos(ang1), -np.sin(ang1)]) / np.sqrt(n)
    f1k = np.einsum('ckn,jl->ckjnl', f1, eye).reshape(2 * n1 * s, n1 * s)
    angt = 2.0 * np.pi * (np.outer(np.arange(n1), np.arange(n2)) % n) / n
    tw = np.stack([np.cos(angt), -np.sin(angt)])
    tw = tw.reshape(2, n1, n2 // s, s).transpose(0, 2, 1, 3).reshape(2, n2 // s, n1 * s, 1)
    tw = np.broadcast_to(tw, (2, n2 // s, n1 * s, LANES))
    ang2 = 2.0 * np.pi * (np.outer(np.arange(n2), np.arange(n2)) % n2) / n2
    cr, ci = np.cos(ang2), -np.sin(ang2)
    m2 = np.array([[cr, -ci], [ci, cr]])
    m2k = np.einsum('abkn,jl->akjbln', m2, eye).reshape(2 * n2 * s, 2 * s * n2)
    gw = d // FOURIER_GROUPS
    angc = 2.0 * np.pi * (np.outer(np.arange(gw), np.arange(gw)) % gw) / gw
    cs = np.concatenate([np.cos(angc), np.sin(angc)], axis=0) / np.sqrt(gw)
    return (f1k.astype(np.float32), tw.astype(np.float32), m2k.astype(np.float32),
            cs.astype(np.float32))


def _fold_kernel(cs_ref, w_ref, o_ref):
    gw = w_ref.shape[0]
    o_ref[0] = jnp.dot(cs_ref[0:gw, :], w_ref[...], precision=HIGHEST, preferred_element_type=F32)
    o_ref[1] = jnp.dot(cs_ref[gw:, :], w_ref[...], precision=HIGHEST, preferred_element_type=F32)


def _fold_channel_dft(cs, w_fourier):
    d = w_fourier.shape[0]
    gw = d // FOURIER_GROUPS
    return pl.pallas_call(
        _fold_kernel,
        grid=(FOURIER_GROUPS,),
        in_specs=[pl.BlockSpec((2 * gw, gw), lambda g: (0, 0)),
                  pl.BlockSpec((gw, d), lambda g: (g, 0))],
        out_specs=pl.BlockSpec((2, gw, d), lambda g: (0, g, 0)),
        out_shape=jax.ShapeDtypeStruct((2, d, d), F32),
        compiler_params=_cparams("arbitrary"),
        name="fold_channel_dft",
    )(cs, w_fourier)


def _fft1_kernel(x_ref, mod_ref, gain_ref, f_ref, tw_ref, o_ref):
    _, n1, s, d = x_ref.shape
    x = x_ref[...].reshape(n1 * s, d)
    h = _rms_mod(x, gain_ref[...], mod_ref[0, 0:1, :], mod_ref[0, 1:2, :]).astype(BF16)
    a = _dot(f_ref[...], h)
    ar = a[:n1 * s]
    ai = a[n1 * s:]
    reps = d // LANES
    twr = jnp.tile(tw_ref[0, 0], (1, reps))
    twi = jnp.tile(tw_ref[1, 0], (1, reps))
    o_ref[0, 0] = (ar * twr - ai * twi).reshape(n1, s, d)
    o_ref[0, 1] = (ar * twi + ai * twr).reshape(n1, s, d)


def _fft_stage1(x2, mods, gain, f1k_bf, tw, batch, n):
    d = x2.shape[-1]
    n2 = FFT_N2
    n1 = n // n2
    s = SUBLANES
    xv = x2.reshape(batch, n1, n2, d)
    return pl.pallas_call(
        _fft1_kernel,
        grid=(n2 // s, batch),
        in_specs=[pl.BlockSpec((1, n1, s, d), lambda g, b: (b, 0, g, 0)),
                  pl.BlockSpec((1, 6, d), lambda g, b: (b, 0, 0)),
                  pl.BlockSpec((1, d), lambda g, b: (0, 0)),
                  pl.BlockSpec((2 * n1 * s, n1 * s), lambda g, b: (0, 0)),
                  pl.BlockSpec((2, 1, n1 * s, LANES), lambda g, b: (0, g, 0, 0))],
        out_specs=pl.BlockSpec((1, 2, n1, s, d), lambda g, b: (b, 0, 0, g, 0)),
        out_shape=jax.ShapeDtypeStruct((batch, 2, n1, n2, d), F32),
        compiler_params=_cparams("arbitrary", "arbitrary"),
        name="fft_stage1",
    )(xv, mods, gain, f1k_bf, tw)


def _fft2_kernel(n_experts, b_ref, x_ref, m_ref, wf_ref, mod_ref, gain_ref, wr_ref,
                 x3_ref, h2_ref, route_ref):
    _, _, s, n2, d = b_ref.shape
    rows = n2 * s
    bmat = b_ref[...].reshape(2 * s * n2, d).astype(BF16)
    z = _dot(m_ref[...], bmat)
    y = _dot(z[:rows].astype(BF16), wf_ref[0]) + _dot(z[rows:].astype(BF16), wf_ref[1])
    x3 = x_ref[...].reshape(rows, d) + mod_ref[0, 2:3, :] * y
    x3_ref[...] = x3.reshape(x3_ref.shape)
    h2 = _rms_mod(x3, gain_ref[...], mod_ref[0, 3:4, :], mod_ref[0, 4:5, :])
    h2_ref[...] = h2.reshape(rows, d // LANES, LANES).reshape(h2_ref.shape)
    logits = jnp.dot(h2, wr_ref[...], precision=HIGHEST, preferred_element_type=F32)
    lane = lax.broadcasted_iota(jnp.int32, logits.shape, 1).astype(F32)
    lg = jnp.where(lane < n_experts, logits, -jnp.inf)
    m1 = jnp.max(lg, axis=-1, keepdims=True)
    i1 = jnp.min(jnp.where(lg == m1, lane, float(LANES)), axis=-1, keepdims=True)
    lg2 = jnp.where(lane == i1, -jnp.inf, lg)
    m2 = jnp.max(lg2, axis=-1, keepdims=True)
    i2 = jnp.min(jnp.where(lg2 == m2, lane, float(LANES)), axis=-1, keepdims=True)
    e = jnp.exp(m2 - m1)
    g1 = 1.0 / (1.0 + e)
    g2 = e / (1.0 + e)
    route = jnp.where(lane == 0, g1, jnp.where(lane == 1, g2, jnp.where(lane == 2, i1,
                                                                       jnp.where(lane == 3, i2, 0.0))))
    route_ref[...] = route.reshape(route_ref.shape)


def _fft_stage2(bmid, x2, m2k_bf, wf_bf, mods, gain, wr_pad, n_experts, batch, n):
    d = x2.shape[-1]
    n2 = FFT_N2
    n1 = n // n2
    s = SUBLANES
    xv = x2.reshape(batch, n2, n1, d)
    pos = lambda g, b: (b, 0, g, 0)
    const2 = lambda g, b: (0, 0)
    x3, h2, route = pl.pallas_call(
        functools.partial(_fft2_kernel, n_experts),
        grid=(n1 // s, batch),
        in_specs=[pl.BlockSpec((1, 2, s, n2, d), lambda g, b: (b, 0, g, 0, 0)),
                  pl.BlockSpec((1, n2, s, d), pos),
                  pl.BlockSpec((2 * n2 * s, 2 * n2 * s), const2),
                  pl.BlockSpec((2, d, d), lambda g, b: (0, 0, 0)),
                  pl.BlockSpec((1, 6, d), lambda g, b: (b, 0, 0)),
                  pl.BlockSpec((1, d), const2),
                  pl.BlockSpec((d, LANES), const2)],
        out_specs=[pl.BlockSpec((1, n2, s, d), pos),
                   pl.BlockSpec((1, n2, s, d // LANES, LANES), lambda g, b: (b, 0, g, 0, 0)),
                   pl.BlockSpec((1, n2, s, LANES), pos)],
        out_shape=[jax.ShapeDtypeStruct((batch, n2, n1, d), F32),
                   jax.ShapeDtypeStruct((batch, n2, n1, d // LANES, LANES), F32),
                   jax.ShapeDtypeStruct((batch, n2, n1, LANES), F32)],
        compiler_params=_cparams("arbitrary", "arbitrary"),
        name="fft_stage2",
    )(bmid, xv, m2k_bf, wf_bf, mods, gain, wr_pad)
    t = batch * n
    return x3.reshape(t, d), h2.reshape(t, d // LANES, LANES), route.reshape(t, LANES)


def _rowcopy_kernel(chunk, sidx_ref, didx_ref, src_ref, *rest):
    dst_ref, sem = rest[-2], rest[-1]
    base = pl.program_id(0) * chunk

    def body(i, carry):
        s = sidx_ref[base + i]
        t = didx_ref[base + i]
        pltpu.make_async_copy(src_ref.at[pl.ds(s, 1)], dst_ref.at[pl.ds(t, 1)], sem).start()
        return carry

    lax.fori_loop(0, chunk, body, 0, unroll=8)

    def drain(i, carry):
        pltpu.make_async_copy(src_ref.at[pl.ds(0, 1)], dst_ref.at[pl.ds(0, 1)], sem).wait()
        return carry

    lax.fori_loop(0, chunk, drain, 0, unroll=8)


def _rowcopy(src, sidx, didx, n_dst, dst_init=None):
    n = sidx.shape[0]
    chunk = min(ROWCOPY_CHUNK, n)
    in_specs = [pl.BlockSpec(memory_space=pl.ANY)]
    args = [sidx, didx, src]
    aliases = {}
    if dst_init is not None:
        in_specs.append(pl.BlockSpec(memory_space=pl.ANY))
        args.append(dst_init)
        aliases = {3: 0}
    return pl.pallas_call(
        functools.partial(_rowcopy_kernel, chunk),
        grid_spec=pltpu.PrefetchScalarGridSpec(
            num_scalar_prefetch=2,
            grid=(n // chunk,),
            in_specs=in_specs,
            out_specs=pl.BlockSpec(memory_space=pl.ANY),
            scratch_shapes=[pltpu.SemaphoreType.DMA(())]),
        out_shape=jax.ShapeDtypeStruct((n_dst,) + src.shape[1:], src.dtype),
        input_output_aliases=aliases,
        compiler_params=pltpu.CompilerParams(dimension_semantics=("arbitrary",), has_side_effects=True),
        name="rowcopy",
    )(*args)


def _expert_kernel(bexp_ref, nused_ref, x_ref, wg_ref, wu_ref, wd_ref, o_ref, xb_ref, acc_ref):
    i = pl.program_id(0)
    j = pl.program_id(1)
    last = pl.num_programs(1) - 1
    used = i < nused_ref[0]

    @pl.when(used)
    def _():
        @pl.when(j == 0)
        def _():
            xb_ref[...] = x_ref[...].reshape(xb_ref.shape).astype(BF16)
            acc_ref[...] = jnp.zeros_like(acc_ref)

        xb = xb_ref[...]
        g = _dot(xb, wg_ref[0])
        u = _dot(xb, wu_ref[0])
        a = (g * _sigmoid(g) * u).astype(BF16)
        acc_ref[...] += _dot(a, wd_ref[0])

        @pl.when(j == last)
        def _():
            o_ref[...] = acc_ref[...].reshape(o_ref.shape)

    @pl.when(jnp.logical_not(used) & (j == last))
    def _():
        o_ref[...] = jnp.zeros_like(o_ref)


def _expert_ffn(xs, w_gu_bf, w_dn_bf, block_expert, n_used):
    n_slots, sub, _ = xs.shape
    e, f, d = w_dn_bf.shape
    nb = n_slots // MOE_BLOCK
    nf = 4 if (f // 4) % LANES == 0 else 1
    tf = f // nf

    def xmap(i, j, be, nu):
        return (jnp.minimum(i, nu[0] - 1), 0, 0)

    def jeff(i, j, nu):
        return jnp.where(i < nu[0], j, nf - 1)

    return pl.pallas_call(
        _expert_kernel,
        grid_spec=pltpu.PrefetchScalarGridSpec(
            num_scalar_prefetch=2,
            grid=(nb, nf),
            in_specs=[pl.BlockSpec((MOE_BLOCK, sub, LANES), xmap),
                      pl.BlockSpec((1, d, tf), lambda i, j, be, nu: (be[i], 0, jeff(i, j, nu))),
                      pl.BlockSpec((1, d, tf), lambda i, j, be, nu: (be[i], 0, nf + jeff(i, j, nu))),
                      pl.BlockSpec((1, tf, d), lambda i, j, be, nu: (be[i], jeff(i, j, nu), 0))],
            out_specs=pl.BlockSpec((MOE_BLOCK, sub, LANES), lambda i, j, be, nu: (i, 0, 0)),
            scratch_shapes=[pltpu.VMEM((MOE_BLOCK, d), BF16), pltpu.VMEM((MOE_BLOCK, d), F32)]),
        out_shape=jax.ShapeDtypeStruct((n_slots, sub, LANES), F32),
        compiler_params=_cparams("arbitrary", "arbitrary"),
        name="expert_ffn",
    )(block_expert, n_used, xs, w_gu_bf, w_gu_bf, w_dn_bf)


def _combine_kernel(x_ref, y0_ref, y1_ref, route_ref, mod_ref, o_ref):
    g0 = route_ref[:, 0:1]
    g1 = route_ref[:, 1:2]
    y0 = y0_ref[...].reshape(x_ref.shape)
    y1 = y1_ref[...].reshape(x_ref.shape)
    o_ref[...] = x_ref[...] + mod_ref[0, 5:6, :] * (g0 * y0 + g1 * y1)


def _combine(x3, ycat, route, mods, tm, tiles_per_batch):
    t, d = x3.shape
    nt = t // tm
    row = lambda i: (i, 0)
    return pl.pallas_call(
        _combine_kernel,
        grid=(nt,),
        in_specs=[pl.BlockSpec((tm, d), row),
                  pl.BlockSpec((tm, d // LANES, LANES), lambda i: (i, 0, 0)),
                  pl.BlockSpec((tm, d // LANES, LANES), lambda i: (nt + i, 0, 0)),
                  pl.BlockSpec((tm, LANES), row),
                  pl.BlockSpec((1, 6, d), lambda i: (i // tiles_per_batch, 0, 0))],
        out_specs=pl.BlockSpec((tm, d), row),
        out_shape=jax.ShapeDtypeStruct((t, d), F32),
        compiler_params=_cparams("arbitrary"),
        name="combine",
    )(x3, ycat, ycat, route, mods)


def _route_slots(route, n_experts):
    t = route.shape[0]
    expert = route[:, 2:2 + TOP_K].astype(jnp.int32).reshape(t * TOP_K)
    onehot = (expert[:, None] == jnp.arange(n_experts, dtype=jnp.int32)[None, :]).astype(jnp.int32)
    csum = jnp.cumsum(onehot, axis=0)
    counts = csum[-1]
    padded = (counts + MOE_BLOCK - 1) // MOE_BLOCK * MOE_BLOCK
    pad_end = jnp.cumsum(padded)
    pad_start = pad_end - padded
    slot = jnp.sum(onehot * (pad_start[None, :] + csum - 1), axis=1).astype(jnp.int32)
    n_slots = t * TOP_K + n_experts * MOE_BLOCK
    nb = n_slots // MOE_BLOCK
    block_expert = jnp.minimum(
        jnp.searchsorted(pad_end, jnp.arange(nb, dtype=jnp.int32) * MOE_BLOCK, side='right'),
        n_experts - 1).astype(jnp.int32)
    n_used = (pad_end[-1:] // MOE_BLOCK).astype(jnp.int32)
    return slot, block_expert, n_used, n_slots


def _rope_tables(n):
    pos = np.arange(n)
    row = (pos // GRID_W).astype(np.float32)
    col = (pos % GRID_W).astype(np.float32)
    n_freq = RET_DK // 4
    inv_freq = (np.float32(ROPE_BASE) ** (-np.arange(n_freq, dtype=np.float32) / n_freq)).astype(np.float32)
    ang = np.concatenate([row[:, None] * inv_freq, col[:, None] * inv_freq], axis=-1)
    cos = np.cos(ang.astype(np.float64))
    sin = np.sin(ang.astype(np.float64))
    cos_h = np.concatenate([cos, cos], axis=-1)
    sin_h = np.concatenate([-sin, sin], axis=-1)
    return (np.tile(cos_h, (1, RET_HEADS)).astype(np.float32),
            np.tile(sin_h, (1, RET_HEADS)).astype(np.float32))


def _permute_in_proj(w_in):
    perm = np.arange(_IN_WIDTH)
    half = np.concatenate([np.arange(0, RET_DK, 2), np.arange(1, RET_DK, 2)])
    for off in (_O_RQ, _O_RK):
        for h in range(RET_HEADS):
            perm[off + h * RET_DK: off + (h + 1) * RET_DK] = off + h * RET_DK + half
    return w_in[:, perm]


def kernel(x, c, ctx, c_ctx, l0_norm_mix, l0_norm_ffn, l0_w_ada, l0_b_ada, l0_w_in, l0_q_gain, l0_k_gain, l0_rpb, l0_ret_decay_fwd, l0_ret_decay_bwd, l0_w_out, l0_w_gate_up, l0_w_down, l1_norm_mix, l1_norm_ffn, l1_w_ada, l1_b_ada, l1_w_fourier, l1_w_router, l1_w_exp_gate_up, l1_w_exp_down):
    batch, n, d = x.shape
    ctx_len = ctx.shape[1]
    t = batch * n
    rows = n // GRID_W
    n_experts = l1_w_router.shape[1]
    assert n % (NA_QROWS * GRID_W) == 0 and rows >= NA_KROWS and n % FFT_N2 == 0
    assert l0_w_in.shape[1] == _IN_WIDTH and d % LANES == 0

    x2d = x.reshape(t, d)
    ctx2d = ctx.reshape(batch * ctx_len, d)

    ada0 = _ada(jnp.concatenate([c, c_ctx[None, :]], axis=0), l0_w_ada, l0_b_ada)
    mods0 = ada0[:batch].reshape(batch, 6, d)
    mods_ctx = ada0[batch:batch + 1].reshape(1, 6, d)
    mods1 = _ada(c, l1_w_ada, l1_b_ada)[:batch].reshape(batch, 6, d)

    w_in_bf = _permute_in_proj(l0_w_in).astype(BF16)
    qg = jnp.tile(l0_q_gain.astype(F32), NA_HEADS).reshape(1, NA_WIDTH)
    kg = jnp.tile(l0_k_gain.astype(F32), NA_HEADS).reshape(1, NA_WIDTH)
    gmat = jnp.asarray(np.kron(np.eye(NA_HEADS), np.full((NA_HEAD_DIM, NA_HEAD_DIM), 1.0 / NA_HEAD_DIM)),
                       BF16)
    cos_np, sin_np = _rope_tables(n)
    gain_mix0 = l0_norm_mix.reshape(1, d)
    tm = min(INPROJ_ROWS, n)
    naq, rq, rg, nak, nav, rk, rv = _inproj(
        x2d, mods0, gain_mix0, w_in_bf, qg, kg, gmat, jnp.asarray(cos_np), jnp.asarray(sin_np),
        tm, n // tm, n // tm)
    ctx_rows = batch * ctx_len
    _, _, _, cx_nak, cx_nav, cx_rk, cx_rv = _inproj(
        ctx2d, mods_ctx, gain_mix0, w_in_bf, qg, kg, gmat,
        jnp.ones((ctx_len, RET_QK_WIDTH), F32), jnp.zeros((ctx_len, RET_QK_WIDTH), F32),
        ctx_len, ctx_rows // ctx_len, 1)

    bias = _na_bias_table(l0_rpb, rows)
    o_na = _na_attention(naq, nak, nav, cx_nak, cx_nav, bias, batch, n, ctx_len)

    lg_f = jax.nn.log_sigmoid(l0_ret_decay_fwd.astype(F32))
    lg_b = jax.nn.log_sigmoid(l0_ret_decay_bwd.astype(F32))
    o_f = _retention_pass(False, lg_f, rq, rk, rv, cx_rk, cx_rv, batch, n, ctx_len)
    o_ret = _retention_pass(True, lg_b, rq, rk, rv, cx_rk, cx_rv, batch, n, ctx_len, o_fwd=o_f, gate=rg)

    x1, h_ffn = _outproj(x2d, o_na, o_ret, l0_w_out.astype(BF16), mods0, l0_norm_ffn.reshape(1, d),
                         tm, n // tm)
    tf_rows = min(FFN_ROWS, n)
    x2 = _swiglu(h_ffn, x1, l0_w_gate_up.astype(BF16), l0_w_down.astype(BF16), mods0,
                 tf_rows, n // tf_rows)

    f1k, tw, m2k, cs = _dft_tables(n, d)
    wf = _fold_channel_dft(jnp.asarray(cs), l1_w_fourier).astype(BF16)
    bmid = _fft_stage1(x2, mods1, l1_norm_mix.reshape(1, d), jnp.asarray(f1k, BF16), jnp.asarray(tw),
                       batch, n)
    wr_pad = jnp.zeros((d, LANES), F32).at[:, :n_experts].set(l1_w_router)
    x3, h2, route = _fft_stage2(bmid, x2, jnp.asarray(m2k, BF16), wf, mods1, l1_norm_ffn.reshape(1, d),
                                wr_pad, n_experts, batch, n)

    slot, block_expert, n_used, n_slots = _route_slots(route, n_experts)
    token = jnp.arange(t * TOP_K, dtype=jnp.int32) // TOP_K
    xs = _rowcopy(h2, token, slot, n_slots, dst_init=jnp.zeros((n_slots, d // LANES, LANES), F32))
    ys = _expert_ffn(xs, l1_w_exp_gate_up.astype(BF16), l1_w_exp_down.astype(BF16), block_expert, n_used)
    slot_km = slot.reshape(t, TOP_K).T.reshape(t * TOP_K)
    ycat = _rowcopy(ys, slot_km, jnp.arange(t * TOP_K, dtype=jnp.int32), t * TOP_K)
    out = _combine(x3, ycat, route, mods1, tm, n // tm)
    return out.reshape(batch, n, d)
```

```python
import functools

import numpy as np
import jax
import jax.numpy as jnp
from jax import lax
from jax.experimental import pallas as pl
from jax.experimental.pallas import tpu as pltpu

F32 = jnp.float32
BF16 = jnp.bfloat16
HIGHEST = lax.Precision.HIGHEST

GRID_W = 64
NA_HEADS = 8
NA_HEAD_DIM = 64
NA_WIN_ROWS = 8
NA_WIN_COLS = 16
RET_HEADS = 4
RET_DK = 64
RET_DV = 128
ROPE_BASE = 10000.0
FOURIER_GROUPS = 8
TOP_K = 2
NORM_EPS = 1e-6

NA_WIDTH = NA_HEADS * NA_HEAD_DIM
RET_QK_WIDTH = RET_HEADS * RET_DK
RET_V_WIDTH = RET_HEADS * RET_DV

LANES = 128
SUBLANES = 8
VMEM_LIMIT = 48 * 1024 * 1024

NEG_BIG = -1e30

INPROJ_ROWS = 512
NA_QROWS = 8
NA_KROWS = 16
RET_CHUNK = 512
FFN_ROWS = 1024
FFT_N2 = 64
MOE_BLOCK = 1024


def _cparams(*sem):
    return pltpu.CompilerParams(dimension_semantics=sem, vmem_limit_bytes=VMEM_LIMIT)


def _sigmoid(x):
    return 1.0 / (1.0 + jnp.exp(-x))


def _dot(a, b):
    return jnp.dot(a, b, preferred_element_type=F32)


def _dot_nt(a, b):
    return lax.dot_general(a, b, (((1,), (1,)), ((), ())), preferred_element_type=F32)


def _dot_tn(a, b):
    return lax.dot_general(a, b, (((0,), (0,)), ((), ())), preferred_element_type=F32)


def _rms_mod(x, gain, shift, scale):
    ms = jnp.mean(x * x, axis=-1, keepdims=True)
    return (x * lax.rsqrt(ms + NORM_EPS) * gain) * (1.0 + scale) + shift


def _ada_kernel(c_ref, w_ref, b_ref, o_ref):
    c = c_ref[...]
    s = c * _sigmoid(c)
    o_ref[...] = jnp.dot(s, w_ref[...], precision=HIGHEST, preferred_element_type=F32) + b_ref[...]


def _ada(cond, w_ada, b_ada):
    d, n = w_ada.shape
    cond8 = jnp.zeros((SUBLANES, d), F32).at[:cond.shape[0]].set(cond)
    tn = 1024
    return pl.pallas_call(
        _ada_kernel,
        grid=(n // tn,),
        in_specs=[pl.BlockSpec((SUBLANES, d), lambda j: (0, 0)),
                  pl.BlockSpec((d, tn), lambda j: (0, j)),
                  pl.BlockSpec((1, tn), lambda j: (0, j))],
        out_specs=pl.BlockSpec((SUBLANES, tn), lambda j: (0, j)),
        out_shape=jax.ShapeDtypeStruct((SUBLANES, n), F32),
        compiler_params=_cparams("arbitrary"),
        name="ada",
    )(cond8, w_ada, b_ada.reshape(1, n))


_O_NAQ = 0
_O_RQ = _O_NAQ + NA_WIDTH
_O_RG = _O_RQ + RET_QK_WIDTH
_O_NAK = _O_RG + RET_V_WIDTH
_O_NAV = _O_NAK + NA_WIDTH
_O_RK = _O_NAV + NA_WIDTH
_O_RV = _O_RK + RET_QK_WIDTH
_IN_WIDTH = _O_RV + RET_V_WIDTH


def _inproj_kernel(x_ref, mod_ref, gain_ref, w_ref, qg_ref, kg_ref, gmat_ref, cos_ref, sin_ref,
                   naq_ref, rq_ref, rg_ref, nak_ref, nav_ref, rk_ref, rv_ref):
    h = _rms_mod(x_ref[...], gain_ref[...], mod_ref[0, 0:1, :], mod_ref[0, 1:2, :]).astype(BF16)
    p = _dot(h, w_ref[...])
    gmat = gmat_ref[...]

    def headnorm(t, g):
        ms = _dot((t * t).astype(BF16), gmat)
        return t * lax.rsqrt(ms + NORM_EPS) * g

    cos = cos_ref[...]
    sin = sin_ref[...]
    lane = lax.broadcasted_iota(jnp.int32, (1, RET_QK_WIDTH), 1)
    first_half = (lane % RET_DK) < (RET_DK // 2)

    def rope(t):
        partner = jnp.where(first_half,
                            pltpu.roll(t, RET_QK_WIDTH - RET_DK // 2, axis=1),
                            pltpu.roll(t, RET_DK // 2, axis=1))
        return t * cos + partner * sin

    naq_ref[...] = (headnorm(p[:, _O_NAQ:_O_RQ], qg_ref[...]) * (NA_HEAD_DIM ** -0.5)).astype(BF16)
    rq_ref[...] = rope(p[:, _O_RQ:_O_RG]).astype(BF16)
    g = p[:, _O_RG:_O_NAK]
    rg_ref[...] = (g * _sigmoid(g)).astype(BF16)
    nak_ref[...] = headnorm(p[:, _O_NAK:_O_NAV], kg_ref[...]).astype(BF16)
    nav_ref[...] = p[:, _O_NAV:_O_RK].astype(BF16)
    rk_ref[...] = (rope(p[:, _O_RK:_O_RV]) * (RET_DK ** -0.5)).astype(BF16)
    rv_ref[...] = p[:, _O_RV:_IN_WIDTH].astype(BF16)


def _inproj(x2d, mods, gain, w_bf, qg, kg, gmat, cos, sin, tm, tiles_per_mod, pos_tiles):
    t, d = x2d.shape
    widths = (NA_WIDTH, RET_QK_WIDTH, RET_V_WIDTH, NA_WIDTH, NA_WIDTH, RET_QK_WIDTH, RET_V_WIDTH)
    row = lambda i: (i, 0)
    const = lambda i: (0, 0)
    return pl.pallas_call(
        _inproj_kernel,
        grid=(t // tm,),
        in_specs=[pl.BlockSpec((tm, d), row),
                  pl.BlockSpec((1, 6, d), lambda i: (i // tiles_per_mod, 0, 0)),
                  pl.BlockSpec((1, d), const),
                  pl.BlockSpec((d, _IN_WIDTH), const),
                  pl.BlockSpec((1, NA_WIDTH), const),
                  pl.BlockSpec((1, NA_WIDTH), const),
                  pl.BlockSpec((NA_WIDTH, NA_WIDTH), const),
                  pl.BlockSpec((tm, RET_QK_WIDTH), lambda i: (i % pos_tiles, 0)),
                  pl.BlockSpec((tm, RET_QK_WIDTH), lambda i: (i % pos_tiles, 0))],
        out_specs=[pl.BlockSpec((tm, w), row) for w in widths],
        out_shape=[jax.ShapeDtypeStruct((t, w), BF16) for w in widths],
        compiler_params=_cparams("arbitrary"),
        name="inproj",
    )(x2d, mods, gain, w_bf, qg, kg, gmat, cos, sin)


def _na_kernel(q_ref, k0, k1, k2, k3, v0, v1, v2, v3, kx_ref, vx_ref, bias_ref, o_ref):
    q2 = q_ref[...]
    lane = lax.broadcasted_iota(jnp.int32, (1, LANES), 1)
    ks = (k0, k1, k2, k3)
    vs = (v0, v1, v2, v3)
    kw = k0.shape[0]
    outs = []
    for hh in range(2):
        sel = (lane // NA_HEAD_DIM) == hh
        qm = jnp.where(sel, q2, jnp.zeros_like(q2))
        s = [_dot_nt(qm, ks[j][...]) + bias_ref[0, hh, :, j * kw:(j + 1) * kw].astype(F32)
             for j in range(4)]
        s.append(_dot_nt(qm, kx_ref[...]))
        m = s[0]
        for t in s[1:-1]:
            m = jnp.maximum(m, t)
        m = jnp.maximum(jnp.max(m, axis=-1, keepdims=True), jnp.max(s[-1], axis=-1, keepdims=True))
        den = jnp.zeros_like(m)
        acc = jnp.zeros((q2.shape[0], LANES), F32)
        vals = vs + (vx_ref,)
        for j in range(5):
            p = jnp.exp(s[j] - m)
            den = den + jnp.sum(p, axis=-1, keepdims=True)
            acc = acc + _dot(p.astype(BF16), vals[j][...])
        outs.append(acc / den)
    o_ref[...] = jnp.where((lane // NA_HEAD_DIM) == 0, outs[0], outs[1]).astype(o_ref.dtype)


def _na_bias_table(rpb, rows):
    h = rpb.shape[0]
    nqb = rows // NA_QROWS
    row_sel = np.zeros((3, NA_QROWS, NA_KROWS, 2 * NA_WIN_ROWS - 1), np.float32)
    row_ok = np.zeros((3, NA_QROWS, NA_KROWS), bool)
    for v, qb in enumerate((0, min(1, nqb - 1), nqb - 1)):
        start = int(np.clip(NA_QROWS * qb - NA_WIN_ROWS // 2, 0, rows - NA_KROWS))
        for rq in range(NA_QROWS):
            r = NA_QROWS * qb + rq
            r0 = int(np.clip(r - NA_WIN_ROWS // 2, 0, rows - NA_WIN_ROWS))
            for rk in range(NA_KROWS):
                kr = start + rk
                if r0 <= kr < r0 + NA_WIN_ROWS:
                    row_ok[v, rq, rk] = True
                    row_sel[v, rq, rk, kr - r + NA_WIN_ROWS - 1] = 1.0
    col_sel = np.zeros((GRID_W, GRID_W, 2 * NA_WIN_COLS - 1), np.float32)
    col_ok = np.zeros((GRID_W, GRID_W), bool)
    for c in range(GRID_W):
        c0 = int(np.clip(c - NA_WIN_COLS // 2, 0, GRID_W - NA_WIN_COLS))
        for kc in range(c0, c0 + NA_WIN_COLS):
            col_ok[c, kc] = True
            col_sel[c, kc, kc - c + NA_WIN_COLS - 1] = 1.0
    t = jnp.einsum('vqkr,hrc->vhqkc', jnp.asarray(row_sel), rpb.astype(F32), precision=HIGHEST)
    t = jnp.einsum('vhqkc,xyc->vhqxky', t, jnp.asarray(col_sel), precision=HIGHEST)
    ok = row_ok[:, None, :, None, :, None] & col_ok[None, None, None, :, None, :]
    t = jnp.where(jnp.asarray(ok), t, NEG_BIG)
    return t.reshape(3, h, NA_QROWS * GRID_W, NA_KROWS * GRID_W).astype(BF16)


def _na_attention(naq, nak, nav, cx_nak, cx_nav, bias, batch, n, ctx_len):
    rows = n // GRID_W
    qn = NA_QROWS * GRID_W
    kw = NA_KROWS * GRID_W // 4
    nqb = rows // NA_QROWS
    kblocks = n // kw
    npairs = NA_WIDTH // LANES

    def kmap(j):
        def f(hp, qb, b):
            start = jnp.clip(2 * qb - 1, 0, kblocks - 4)
            return (b * kblocks + start + j, hp)
        return f

    def bias_map(hp, qb, b):
        v = jnp.where(qb == 0, 0, jnp.where(qb == nqb - 1, 2, 1))
        return (v, hp, 0, 0)

    kv_specs = [pl.BlockSpec((kw, LANES), kmap(j)) for j in range(4)]
    return pl.pallas_call(
        _na_kernel,
        grid=(npairs, nqb, batch),
        in_specs=[pl.BlockSpec((qn, LANES), lambda hp, qb, b: (b * nqb + qb, hp))]
                 + kv_specs + kv_specs
                 + [pl.BlockSpec((ctx_len, LANES), lambda hp, qb, b: (b, hp)),
                    pl.BlockSpec((ctx_len, LANES), lambda hp, qb, b: (b, hp)),
                    pl.BlockSpec((1, 2, qn, 4 * kw), bias_map)],
        out_specs=pl.BlockSpec((qn, LANES), lambda hp, qb, b: (b * nqb + qb, hp)),
        out_shape=jax.ShapeDtypeStruct((batch * n, NA_WIDTH), BF16),
        compiler_params=_cparams("arbitrary", "arbitrary", "arbitrary"),
        name="na_attention",
    )(naq, nak, nak, nak, nak, nav, nav, nav, nav, cx_nak, cx_nav, bias)


def _ret_kernel(reverse, lg_ref, q_ref, k_ref, v_ref, kx_ref, vx_ref, *rest):
    if reverse:
        of_ref, gate_ref, o_ref, s_ref, d_ref, xi_ref, zeta_ref = rest
    else:
        o_ref, s_ref, d_ref, xi_ref, zeta_ref = rest
    c = q_ref.shape[0]
    first = (pl.program_id(0) == 0) & (pl.program_id(1) == 0)
    lane = lax.broadcasted_iota(jnp.int32, (1, LANES), 1)

    @pl.when(first)
    def _():
        i = lax.broadcasted_iota(jnp.int32, (c, c), 0)
        j = lax.broadcasted_iota(jnp.int32, (c, c), 1)
        diff = (j - i) if reverse else (i - j)
        difff = jnp.maximum(diff, 0).astype(F32)
        r = lax.broadcasted_iota(jnp.int32, (c, LANES), 0).astype(F32)
        for h in range(RET_HEADS):
            lg = lg_ref[h]
            d_ref[h] = jnp.where(diff >= 0, jnp.exp(difff * lg), 0.0)
            xi_ref[h] = jnp.exp(((c - r) if reverse else (r + 1.0)) * lg)
        for p in range(RET_HEADS // 2):
            lgp = jnp.where(lane < RET_DK, lg_ref[2 * p], lg_ref[2 * p + 1])
            zeta_ref[p] = jnp.exp((r if reverse else (c - 1.0 - r)) * lgp)

    @pl.when(pl.program_id(1) == 0)
    def _():
        l = kx_ref.shape[0]
        r = lax.broadcasted_iota(jnp.int32, (l, LANES), 0).astype(F32)
        for p in range(RET_HEADS // 2):
            lgp = jnp.where(lane < RET_DK, lg_ref[2 * p], lg_ref[2 * p + 1])
            zc = jnp.exp((r if reverse else (l - 1.0 - r)) * lgp)
            kz = kx_ref[:, p * LANES:(p + 1) * LANES].astype(F32) * zc
            for hh in range(2):
                h = 2 * p + hh
                km = jnp.where((lane // RET_DK) == hh, kz, 0.0).astype(BF16)
                s_ref[h] = _dot_tn(km, vx_ref[:, h * RET_DV:(h + 1) * RET_DV])

    outs = []
    for p in range(RET_HEADS // 2):
        q2 = q_ref[:, p * LANES:(p + 1) * LANES]
        k2 = k_ref[:, p * LANES:(p + 1) * LANES]
        kz = k2.astype(F32) * zeta_ref[p]
        for hh in range(2):
            h = 2 * p + hh
            sel = (lane // RET_DK) == hh
            qm = jnp.where(sel, q2, jnp.zeros_like(q2))
            km = jnp.where(sel, k2, jnp.zeros_like(k2))
            vh = v_ref[:, h * RET_DV:(h + 1) * RET_DV]
            inner = (_dot_nt(qm, km) * d_ref[h]).astype(BF16)
            state = s_ref[h]
            o = _dot(inner, vh) + _dot(qm, state.astype(BF16)) * xi_ref[h]
            g_chunk = jnp.exp(jnp.full((1, LANES), c, F32) * lg_ref[h])
            s_ref[h] = state * g_chunk + _dot_tn(jnp.where(sel, kz, 0.0).astype(BF16), vh)
            outs.append(o)
    if reverse:
        for h in range(RET_HEADS):
            tot = outs[h] + of_ref[:, h * RET_DV:(h + 1) * RET_DV]
            ms = jnp.mean(tot * tot, axis=-1, keepdims=True)
            gate = gate_ref[:, h * RET_DV:(h + 1) * RET_DV].astype(F32)
            o_ref[:, h * RET_DV:(h + 1) * RET_DV] = (tot * lax.rsqrt(ms + NORM_EPS) * gate).astype(o_ref.dtype)
    else:
        for h in range(RET_HEADS):
            o_ref[:, h * RET_DV:(h + 1) * RET_DV] = outs[h]


def _retention_pass(reverse, log_g, rq, rk, rv, cx_rk, cx_rv, batch, n, ctx_len, o_fwd=None, gate=None):
    c = min(RET_CHUNK, n)
    nc = n // c
    if reverse:
        seq = lambda b, i: (b * nc + nc - 1 - i, 0)
    else:
        seq = lambda b, i: (b * nc + i, 0)
    ctx = lambda b, i: (b, 0)
    in_specs = [pl.BlockSpec(memory_space=pltpu.SMEM),
                pl.BlockSpec((c, RET_QK_WIDTH), seq),
                pl.BlockSpec((c, RET_QK_WIDTH), seq),
                pl.BlockSpec((c, RET_V_WIDTH), seq),
                pl.BlockSpec((ctx_len, RET_QK_WIDTH), ctx),
                pl.BlockSpec((ctx_len, RET_V_WIDTH), ctx)]
    args = [log_g, rq, rk, rv, cx_rk, cx_rv]
    if reverse:
        in_specs += [pl.BlockSpec((c, RET_V_WIDTH), seq), pl.BlockSpec((c, RET_V_WIDTH), seq)]
        args += [o_fwd, gate]
    return pl.pallas_call(
        functools.partial(_ret_kernel, reverse),
        grid=(batch, nc),
        in_specs=in_specs,
        out_specs=pl.BlockSpec((c, RET_V_WIDTH), seq),
        out_shape=jax.ShapeDtypeStruct((batch * n, RET_V_WIDTH), BF16 if reverse else F32),
        scratch_shapes=[pltpu.VMEM((RET_HEADS, LANES, RET_DV), F32),
                        pltpu.VMEM((RET_HEADS, c, c), F32),
                        pltpu.VMEM((RET_HEADS, c, RET_DV), F32),
                        pltpu.VMEM((RET_HEADS // 2, c, LANES), F32)],
        compiler_params=_cparams("arbitrary", "arbitrary"),
        name="retention_bwd" if reverse else "retention_fwd",
    )(*args)


def _outproj_kernel(x_ref, ona_ref, oret_ref, w_ref, mod_ref, gain_ref, x1_ref, h_ref):
    mix = _dot(ona_ref[...], w_ref[0:NA_WIDTH, :]) + _dot(oret_ref[...], w_ref[NA_WIDTH:, :])
    x1 = x_ref[...] + mod_ref[0, 2:3, :] * mix
    x1_ref[...] = x1
    h_ref[...] = _rms_mod(x1, gain_ref[...], mod_ref[0, 3:4, :], mod_ref[0, 4:5, :]).astype(BF16)


def _outproj(x2d, ona, oret, w_out_bf, mods, gain, tm, tiles_per_batch):
    t, d = x2d.shape
    row = lambda i: (i, 0)
    const = lambda i: (0, 0)
    return pl.pallas_call(
        _outproj_kernel,
        grid=(t // tm,),
        in_specs=[pl.BlockSpec((tm, d), row),
                  pl.BlockSpec((tm, NA_WIDTH), row),
                  pl.BlockSpec((tm, RET_V_WIDTH), row),
                  pl.BlockSpec((NA_WIDTH + RET_V_WIDTH, d), const),
                  pl.BlockSpec((1, 6, d), lambda i: (i // tiles_per_batch, 0, 0)),
                  pl.BlockSpec((1, d), const)],
        out_specs=[pl.BlockSpec((tm, d), row), pl.BlockSpec((tm, d), row)],
        out_shape=[jax.ShapeDtypeStruct((t, d), F32), jax.ShapeDtypeStruct((t, d), BF16)],
        compiler_params=_cparams("arbitrary"),
        name="outproj",
    )(x2d, ona, oret, w_out_bf, mods, gain)


def _swiglu_kernel(h_ref, x1_ref, wg_ref, wu_ref, wd_ref, mod_ref, o_ref, acc_ref):
    j = pl.program_id(1)

    @pl.when(j == 0)
    def _():
        acc_ref[...] = jnp.zeros_like(acc_ref)

    h = h_ref[...]
    g = _dot(h, wg_ref[...])
    u = _dot(h, wu_ref[...])
    a = (g * _sigmoid(g) * u).astype(BF16)
    acc_ref[...] += _dot(a, wd_ref[...])

    @pl.when(j == pl.num_programs(1) - 1)
    def _():
        o_ref[...] = x1_ref[...] + mod_ref[0, 5:6, :] * acc_ref[...]


def _swiglu(h_bf, x1, w_gate_up_bf, w_down_bf, mods, tm, tiles_per_batch):
    t, d = x1.shape
    f = w_down_bf.shape[0]
    nf = 2 if (f // 2) % LANES == 0 else 1
    tf = f // nf
    return pl.pallas_call(
        _swiglu_kernel,
        grid=(t // tm, nf),
        in_specs=[pl.BlockSpec((tm, d), lambda i, j: (i, 0)),
                  pl.BlockSpec((tm, d), lambda i, j: (i, 0)),
                  pl.BlockSpec((d, tf), lambda i, j: (0, j)),
                  pl.BlockSpec((d, tf), lambda i, j: (0, nf + j)),
                  pl.BlockSpec((tf, d), lambda i, j: (j, 0)),
                  pl.BlockSpec((1, 6, d), lambda i, j: (i // tiles_per_batch, 0, 0))],
        out_specs=pl.BlockSpec((tm, d), lambda i, j: (i, 0)),
        out_shape=jax.ShapeDtypeStruct((t, d), F32),
        scratch_shapes=[pltpu.VMEM((tm, d), F32)],
        compiler_params=_cparams("arbitrary", "arbitrary"),
        name="swiglu",
    )(h_bf, x1, w_gate_up_bf, w_gate_up_bf, w_down_bf, mods)


def _dft_tables(n, d):
    n2 = FFT_N2
    n1 = n // n2
    s = SUBLANES
    eye = np.eye(s)
    ang1 = 2.0 * np.pi * (np.outer(np.arange(n1), np.arange(n1)) % n1) / n1
    f1 = np.stack([np.cos(ang1), -np.sin(ang1)]) / np.sqrt(n)
    f1k = np.einsum('ckn,jl->ckjnl', f1, eye).reshape(2 * n1 * s, n1 * s)
    angt = 2.0 * np.pi * (np.outer(np.arange(n1), np.arange(n2)) % n) / n
    tw = np.stack([np.cos(angt), -np.sin(angt)])
    tw = tw.reshape(2, n1, n2 // s, s).transpose(0, 2, 1, 3).reshape(2, n2 // s, n1 * s, 1)
    tw = np.broadcast_to(tw, (2, n2 // s, n1 * s, LANES))
    ang2 = 2.0 * np.pi * (np.outer(np.arange(n2), np.arange(n2)) % n2) / n2
    cr, ci = np.cos(ang2), -np.sin(ang2)
    m2 = np.array([[cr, -ci], [ci, cr]])
    m2k = np.einsum('abkn,jl->akjbln', m2, eye).reshape(2 * n2 * s, 2 * s * n2)
    gw = d // FOURIER_GROUPS
    angc = 2.0 * np.pi * (np.outer(np.arange(gw), np.arange(gw)) % gw) / gw
    cs = np.concatenate([np.cos(angc), np.sin(angc)], axis=0) / np.sqrt(gw)
    return (f1k.astype(np.float32), tw.astype(np.float32), m2k.astype(np.float32),
            cs.astype(np.float32))


def _fold_kernel(cs_ref, w_ref, o_ref):
    gw = w_ref.shape[0]
    o_ref[0] = jnp.dot(cs_ref[0:gw, :], w_ref[...], precision=HIGHEST, preferred_element_type=F32)
    o_ref[1] = jnp.dot(cs_ref[gw:, :], w_ref[...], precision=HIGHEST, preferred_element_type=F32)


def _fold_channel_dft(cs, w_fourier):
    d = w_fourier.shape[0]
    gw = d // FOURIER_GROUPS
    return pl.pallas_call(
        _fold_kernel,
        grid=(FOURIER_GROUPS,),
        in_specs=[pl.BlockSpec((2 * gw, gw), lambda g: (0, 0)),
                  pl.BlockSpec((gw, d), lambda g: (g, 0))],
        out_specs=pl.BlockSpec((2, gw, d), lambda g: (0, g, 0)),
        out_shape=jax.ShapeDtypeStruct((2, d, d), F32),
        compiler_params=_cparams("arbitrary"),
        name="fold_channel_dft",
    )(cs, w_fourier)


def _fft1_kernel(x_ref, mod_ref, gain_ref, f_ref, tw_ref, o_ref):
    _, n1, s, d = x_ref.shape
    x = x_ref[...].reshape(n1 * s, d)
    h = _rms_mod(x, gain_ref[...], mod_ref[0, 0:1, :], mod_ref[0, 1:2, :]).astype(BF16)
    a = _dot(f_ref[...], h)
    ar = a[:n1 * s]
    ai = a[n1 * s:]
    reps = d // LANES
    twr = jnp.tile(tw_ref[0, 0], (1, reps))
    twi = jnp.tile(tw_ref[1, 0], (1, reps))
    o_ref[0, 0] = (ar * twr - ai * twi).reshape(n1, s, d)
    o_ref[0, 1] = (ar * twi + ai * twr).reshape(n1, s, d)


def _fft_stage1(x2, mods, gain, f1k_bf, tw, batch, n):
    d = x2.shape[-1]
    n2 = FFT_N2
    n1 = n // n2
    s = SUBLANES
    xv = x2.reshape(batch, n1, n2, d)
    return pl.pallas_call(
        _fft1_kernel,
        grid=(n2 // s, batch),
        in_specs=[pl.BlockSpec((1, n1, s, d), lambda g, b: (b, 0, g, 0)),
                  pl.BlockSpec((1, 6, d), lambda g, b: (b, 0, 0)),
                  pl.BlockSpec((1, d), lambda g, b: (0, 0)),
                  pl.BlockSpec((2 * n1 * s, n1 * s), lambda g, b: (0, 0)),
                  pl.BlockSpec((2, 1, n1 * s, LANES), lambda g, b: (0, g, 0, 0))],
        out_specs=pl.BlockSpec((1, 2, n1, s, d), lambda g, b: (b, 0, 0, g, 0)),
        out_shape=jax.ShapeDtypeStruct((batch, 2, n1, n2, d), F32),
        compiler_params=_cparams("arbitrary", "arbitrary"),
        name="fft_stage1",
    )(xv, mods, gain, f1k_bf, tw)


def _fft2_kernel(n_experts, b_ref, x_ref, m_ref, wf_ref, mod_ref, gain_ref, wr_ref,
                 x3_ref, h2_ref, route_ref):
    _, _, s, n2, d = b_ref.shape
    rows = n2 * s
    bmat = b_ref[...].reshape(2 * s * n2, d).astype(BF16)
    z = _dot(m_ref[...], bmat)
    y = _dot(z[:rows].astype(BF16), wf_ref[0]) + _dot(z[rows:].astype(BF16), wf_ref[1])
    x3 = x_ref[...].reshape(rows, d) + mod_ref[0, 2:3, :] * y
    x3_ref[...] = x3.reshape(x3_ref.shape)
    h2 = _rms_mod(x3, gain_ref[...], mod_ref[0, 3:4, :], mod_ref[0, 4:5, :])
    h2_ref[...] = h2.reshape(rows, d // LANES, LANES).reshape(h2_ref.shape)
    logits = jnp.dot(h2, wr_ref[...], precision=HIGHEST, preferred_element_type=F32)
    lane = lax.broadcasted_iota(jnp.int32, logits.shape, 1).astype(F32)
    lg = jnp.where(lane < n_experts, logits, -jnp.inf)
    m1 = jnp.max(lg, axis=-1, keepdims=True)
    i1 = jnp.min(jnp.where(lg == m1, lane, float(LANES)), axis=-1, keepdims=True)
    lg2 = jnp.where(lane == i1, -jnp.inf, lg)
    m2 = jnp.max(lg2, axis=-1, keepdims=True)
    i2 = jnp.min(jnp.where(lg2 == m2, lane, float(LANES)), axis=-1, keepdims=True)
    e = jnp.exp(m2 - m1)
    g1 = 1.0 / (1.0 + e)
    g2 = e / (1.0 + e)
    route = jnp.where(lane == 0, g1, jnp.where(lane == 1, g2, jnp.where(lane == 2, i1,
                                                                       jnp.where(lane == 3, i2, 0.0))))
    route_ref[...] = route.reshape(route_ref.shape)


def _fft_stage2(bmid, x2, m2k_bf, wf_bf, mods, gain, wr_pad, n_experts, batch, n):
    d = x2.shape[-1]
    n2 = FFT_N2
    n1 = n // n2
    s = SUBLANES
    xv = x2.reshape(batch, n2, n1, d)
    pos = lambda g, b: (b, 0, g, 0)
    const2 = lambda g, b: (0, 0)
    x3, h2, route = pl.pallas_call(
        functools.partial(_fft2_kernel, n_experts),
        grid=(n1 // s, batch),
        in_specs=[pl.BlockSpec((1, 2, s, n2, d), lambda g, b: (b, 0, g, 0, 0)),
                  pl.BlockSpec((1, n2, s, d), pos),
                  pl.BlockSpec((2 * n2 * s, 2 * n2 * s), const2),
                  pl.BlockSpec((2, d, d), lambda g, b: (0, 0, 0)),
                  pl.BlockSpec((1, 6, d), lambda g, b: (b, 0, 0)),
                  pl.BlockSpec((1, d), const2),
                  pl.BlockSpec((d, LANES), const2)],
        out_specs=[pl.BlockSpec((1, n2, s, d), pos),
                   pl.BlockSpec((1, n2, s, d // LANES, LANES), lambda g, b: (b, 0, g, 0, 0)),
                   pl.BlockSpec((1, n2, s, LANES), pos)],
        out_shape=[jax.ShapeDtypeStruct((batch, n2, n1, d), F32),
                   jax.ShapeDtypeStruct((batch, n2, n1, d // LANES, LANES), F32),
                   jax.ShapeDtypeStruct((batch, n2, n1, LANES), F32)],
        compiler_params=_cparams("arbitrary", "arbitrary"),
        name="fft_stage2",
    )(bmid, xv, m2k_bf, wf_bf, mods, gain, wr_pad)
    t = batch * n
    return x3.reshape(t, d), h2.reshape(t, d // LANES, LANES), route.reshape(t, LANES)


def _dispatch_kernel(tm, slot_ref, h_ref, dst_in_ref, dst_ref, sem):
    del dst_in_ref
    base = pl.program_id(0) * tm * TOP_K

    def body(r, carry):
        for k in range(TOP_K):
            s = slot_ref[base + TOP_K * r + k]
            pltpu.make_async_copy(h_ref.at[pl.ds(r, 1)], dst_ref.at[pl.ds(s, 1)], sem).start()
        return carry

    lax.fori_loop(0, tm, body, 0, unroll=8)

    def drain(r, carry):
        pltpu.make_async_copy(h_ref.at[pl.ds(0, 1)], dst_ref.at[pl.ds(0, 1)], sem).wait()
        return carry

    lax.fori_loop(0, tm * TOP_K, drain, 0, unroll=8)


def _dispatch(h2, slot, n_slots, tm):
    t = h2.shape[0]
    tile = h2.shape[1:]
    return pl.pallas_call(
        functools.partial(_dispatch_kernel, tm),
        grid_spec=pltpu.PrefetchScalarGridSpec(
            num_scalar_prefetch=1,
            grid=(t // tm,),
            in_specs=[pl.BlockSpec((tm,) + tile, lambda i, sl: (i, 0, 0)),
                      pl.BlockSpec(memory_space=pl.ANY)],
            out_specs=pl.BlockSpec(memory_space=pl.ANY),
            scratch_shapes=[pltpu.SemaphoreType.DMA(())]),
        out_shape=jax.ShapeDtypeStruct((n_slots,) + tile, h2.dtype),
        input_output_aliases={2: 0},
        compiler_params=pltpu.CompilerParams(dimension_semantics=("arbitrary",), has_side_effects=True,
                                             vmem_limit_bytes=VMEM_LIMIT),
        name="dispatch",
    )(slot, h2, jnp.zeros((n_slots,) + tile, h2.dtype))


def _expert_kernel(bexp_ref, nused_ref, x_ref, wg_ref, wu_ref, wd_ref, o_ref, xb_ref, acc_ref):
    i = pl.program_id(0)
    j = pl.program_id(1)
    last = pl.num_programs(1) - 1
    used = i < nused_ref[0]

    @pl.when(used)
    def _():
        @pl.when(j == 0)
        def _():
            xb_ref[...] = x_ref[...].reshape(xb_ref.shape).astype(BF16)
            acc_ref[...] = jnp.zeros_like(acc_ref)

        xb = xb_ref[...]
        g = _dot(xb, wg_ref[0])
        u = _dot(xb, wu_ref[0])
        a = (g * _sigmoid(g) * u).astype(BF16)
        acc_ref[...] += _dot(a, wd_ref[0])

        @pl.when(j == last)
        def _():
            o_ref[...] = acc_ref[...].reshape(o_ref.shape)

    @pl.when(jnp.logical_not(used) & (j == last))
    def _():
        o_ref[...] = jnp.zeros_like(o_ref)


def _expert_ffn(xs, w_gu_bf, w_dn_bf, block_expert, n_used):
    n_slots, sub, _ = xs.shape
    e, f, d = w_dn_bf.shape
    nb = n_slots // MOE_BLOCK
    nf = 4 if (f // 4) % LANES == 0 else 1
    tf = f // nf

    def xmap(i, j, be, nu):
        return (jnp.minimum(i, nu[0] - 1), 0, 0)

    def jeff(i, j, nu):
        return jnp.where(i < nu[0], j, nf - 1)

    return pl.pallas_call(
        _expert_kernel,
        grid_spec=pltpu.PrefetchScalarGridSpec(
            num_scalar_prefetch=2,
            grid=(nb, nf),
            in_specs=[pl.BlockSpec((MOE_BLOCK, sub, LANES), xmap),
                      pl.BlockSpec((1, d, tf), lambda i, j, be, nu: (be[i], 0, jeff(i, j, nu))),
                      pl.BlockSpec((1, d, tf), lambda i, j, be, nu: (be[i], 0, nf + jeff(i, j, nu))),
                      pl.BlockSpec((1, tf, d), lambda i, j, be, nu: (be[i], jeff(i, j, nu), 0))],
            out_specs=pl.BlockSpec((MOE_BLOCK, sub, LANES), lambda i, j, be, nu: (i, 0, 0)),
            scratch_shapes=[pltpu.VMEM((MOE_BLOCK, d), BF16), pltpu.VMEM((MOE_BLOCK, d), F32)]),
        out_shape=jax.ShapeDtypeStruct((n_slots, sub, LANES), F32),
        compiler_params=_cparams("arbitrary", "arbitrary"),
        name="expert_ffn",
    )(block_expert, n_used, xs, w_gu_bf, w_gu_bf, w_dn_bf)


def _combine_kernel(tm, slot_ref, x_ref, route_ref, mod_ref, ys_ref, o_ref, ybuf, sem):
    i = pl.program_id(0)

    def issue(step, b):
        base = step * (tm * TOP_K)

        def body(r, carry):
            for k in range(TOP_K):
                s = slot_ref[base + TOP_K * r + k]
                pltpu.make_async_copy(ys_ref.at[pl.ds(s, 1)], ybuf.at[b, k, pl.ds(r, 1)], sem.at[b]).start()
            return carry

        lax.fori_loop(0, tm, body, 0, unroll=8)

    @pl.when(i == 0)
    def _():
        issue(0, 0)

    @pl.when(i + 1 < pl.num_programs(0))
    def _():
        issue(i + 1, (i + 1) % 2)

    b = i % 2

    def drain(r, carry):
        pltpu.make_async_copy(ys_ref.at[pl.ds(0, 1)], ybuf.at[b, 0, pl.ds(0, 1)], sem.at[b]).wait()
        return carry

    lax.fori_loop(0, tm * TOP_K, drain, 0, unroll=8)
    y0 = ybuf[b, 0].reshape(x_ref.shape)
    y1 = ybuf[b, 1].reshape(x_ref.shape)
    g0 = route_ref[:, 0:1]
    g1 = route_ref[:, 1:2]
    o_ref[...] = x_ref[...] + mod_ref[0, 5:6, :] * (g0 * y0 + g1 * y1)


def _combine(x3, ys, slot, route, mods, tm, tiles_per_batch):
    t, d = x3.shape
    tile = ys.shape[1:]
    row = lambda i, sl: (i, 0)
    return pl.pallas_call(
        functools.partial(_combine_kernel, tm),
        grid_spec=pltpu.PrefetchScalarGridSpec(
            num_scalar_prefetch=1,
            grid=(t // tm,),
            in_specs=[pl.BlockSpec((tm, d), row),
                      pl.BlockSpec((tm, LANES), row),
                      pl.BlockSpec((1, 6, d), lambda i, sl: (i // tiles_per_batch, 0, 0)),
                      pl.BlockSpec(memory_space=pl.ANY)],
            out_specs=pl.BlockSpec((tm, d), row),
            scratch_shapes=[pltpu.VMEM((2, TOP_K, tm) + tile, F32), pltpu.SemaphoreType.DMA((2,))]),
        out_shape=jax.ShapeDtypeStruct((t, d), F32),
        compiler_params=_cparams("arbitrary"),
        name="combine",
    )(slot, x3, route, mods, ys)


def _route_slots(route, n_experts):
    t = route.shape[0]
    expert = route[:, 2:2 + TOP_K].astype(jnp.int32).reshape(t * TOP_K)
    onehot = (expert[:, None] == jnp.arange(n_experts, dtype=jnp.int32)[None, :]).astype(jnp.int32)
    csum = jnp.cumsum(onehot, axis=0)
    counts = csum[-1]
    padded = (counts + MOE_BLOCK - 1) // MOE_BLOCK * MOE_BLOCK
    pad_end = jnp.cumsum(padded)
    pad_start = pad_end - padded
    slot = jnp.sum(onehot * (pad_start[None, :] + csum - 1), axis=1).astype(jnp.int32)
    n_slots = t * TOP_K + n_experts * MOE_BLOCK
    nb = n_slots // MOE_BLOCK
    block_expert = jnp.minimum(
        jnp.searchsorted(pad_end, jnp.arange(nb, dtype=jnp.int32) * MOE_BLOCK, side='right'),
        n_experts - 1).astype(jnp.int32)
    n_used = (pad_end[-1:] // MOE_BLOCK).astype(jnp.int32)
    return slot, block_expert, n_used, n_slots


def _rope_tables(n):
    pos = np.arange(n)
    row = (pos // GRID_W).astype(np.float32)
    col = (pos % GRID_W).astype(np.float32)
    n_freq = RET_DK // 4
    inv_freq = (np.float32(ROPE_BASE) ** (-np.arange(n_freq, dtype=np.float32) / n_freq)).astype(np.float32)
    ang = np.concatenate([row[:, None] * inv_freq, col[:, None] * inv_freq], axis=-1)
    cos = np.cos(ang.astype(np.float64))
    sin = np.sin(ang.astype(np.float64))
    cos_h = np.concatenate([cos, cos], axis=-1)
    sin_h = np.concatenate([-sin, sin], axis=-1)
    return (np.tile(cos_h, (1, RET_HEADS)).astype(np.float32),
            np.tile(sin_h, (1, RET_HEADS)).astype(np.float32))


def _permute_in_proj(w_in):
    perm = np.arange(_IN_WIDTH)
    half = np.concatenate([np.arange(0, RET_DK, 2), np.arange(1, RET_DK, 2)])
    for off in (_O_RQ, _O_RK):
        for h in range(RET_HEADS):
            perm[off + h * RET_DK: off + (h + 1) * RET_DK] = off + h * RET_DK + half
    return w_in[:, perm]


def kernel(x, c, ctx, c_ctx, l0_norm_mix, l0_norm_ffn, l0_w_ada, l0_b_ada, l0_w_in, l0_q_gain, l0_k_gain, l0_rpb, l0_ret_decay_fwd, l0_ret_decay_bwd, l0_w_out, l0_w_gate_up, l0_w_down, l1_norm_mix, l1_norm_ffn, l1_w_ada, l1_b_ada, l1_w_fourier, l1_w_router, l1_w_exp_gate_up, l1_w_exp_down):
    batch, n, d = x.shape
    ctx_len = ctx.shape[1]
    t = batch * n
    rows = n // GRID_W
    n_experts = l1_w_router.shape[1]
    assert n % (NA_QROWS * GRID_W) == 0 and rows >= NA_KROWS and n % FFT_N2 == 0
    assert l0_w_in.shape[1] == _IN_WIDTH and d % LANES == 0

    x2d = x.reshape(t, d)
    ctx2d = ctx.reshape(batch * ctx_len, d)

    ada0 = _ada(jnp.concatenate([c, c_ctx[None, :]], axis=0), l0_w_ada, l0_b_ada)
    mods0 = ada0[:batch].reshape(batch, 6, d)
    mods_ctx = ada0[batch:batch + 1].reshape(1, 6, d)
    mods1 = _ada(c, l1_w_ada, l1_b_ada)[:batch].reshape(batch, 6, d)

    w_in_bf = _permute_in_proj(l0_w_in).astype(BF16)
    qg = jnp.tile(l0_q_gain.astype(F32), NA_HEADS).reshape(1, NA_WIDTH)
    kg = jnp.tile(l0_k_gain.astype(F32), NA_HEADS).reshape(1, NA_WIDTH)
    gmat = jnp.asarray(np.kron(np.eye(NA_HEADS), np.full((NA_HEAD_DIM, NA_HEAD_DIM), 1.0 / NA_HEAD_DIM)),
                       BF16)
    cos_np, sin_np = _rope_tables(n)
    gain_mix0 = l0_norm_mix.reshape(1, d)
    tm = min(INPROJ_ROWS, n)
    naq, rq, rg, nak, nav, rk, rv = _inproj(
        x2d, mods0, gain_mix0, w_in_bf, qg, kg, gmat, jnp.asarray(cos_np), jnp.asarray(sin_np),
        tm, n // tm, n // tm)
    ctx_rows = batch * ctx_len
    _, _, _, cx_nak, cx_nav, cx_rk, cx_rv = _inproj(
        ctx2d, mods_ctx, gain_mix0, w_in_bf, qg, kg, gmat,
        jnp.ones((ctx_len, RET_QK_WIDTH), F32), jnp.zeros((ctx_len, RET_QK_WIDTH), F32),
        ctx_len, ctx_rows // ctx_len, 1)

    bias = _na_bias_table(l0_rpb, rows)
    o_na = _na_attention(naq, nak, nav, cx_nak, cx_nav, bias, batch, n, ctx_len)

    lg_f = jax.nn.log_sigmoid(l0_ret_decay_fwd.astype(F32))
    lg_b = jax.nn.log_sigmoid(l0_ret_decay_bwd.astype(F32))
    o_f = _retention_pass(False, lg_f, rq, rk, rv, cx_rk, cx_rv, batch, n, ctx_len)
    o_ret = _retention_pass(True, lg_b, rq, rk, rv, cx_rk, cx_rv, batch, n, ctx_len, o_fwd=o_f, gate=rg)

    x1, h_ffn = _outproj(x2d, o_na, o_ret, l0_w_out.astype(BF16), mods0, l0_norm_ffn.reshape(1, d),
                         tm, n // tm)
    tf_rows = min(FFN_ROWS, n)
    x2 = _swiglu(h_ffn, x1, l0_w_gate_up.astype(BF16), l0_w_down.astype(BF16), mods0,
                 tf_rows, n // tf_rows)

    f1k, tw, m2k, cs = _dft_tables(n, d)
    wf = _fold_channel_dft(jnp.asarray(cs), l1_w_fourier).astype(BF16)
    bmid = _fft_stage1(x2, mods1, l1_norm_mix.reshape(1, d), jnp.asarray(f1k, BF16), jnp.asarray(tw),
                       batch, n)
    wr_pad = jnp.zeros((d, LANES), F32).at[:, :n_experts].set(l1_w_router)
    x3, h2, route = _fft_stage2(bmid, x2, jnp.asarray(m2k, BF16), wf, mods1, l1_norm_ffn.reshape(1, d),
                                wr_pad, n_experts, batch, n)

    slot, block_expert, n_used, n_slots = _route_slots(route, n_experts)
    xs = _dispatch(h2, slot, n_slots, tm)
    ys = _expert_ffn(xs, l1_w_exp_gate_up.astype(BF16), l1_w_exp_down.astype(BF16), block_expert, n_used)
    out = _combine(x3, ys, slot, route, mods1, tm, n // tm)
    return out.reshape(batch, n, d)
```

```python
import functools

import numpy as np
import jax
import jax.numpy as jnp
from jax import lax
from jax.experimental import pallas as pl
from jax.experimental.pallas import tpu as pltpu

F32 = jnp.float32
BF16 = jnp.bfloat16
HIGHEST = lax.Precision.HIGHEST

GRID_W = 64
NA_HEADS = 8
NA_HEAD_DIM = 64
NA_WIN_ROWS = 8
NA_WIN_COLS = 16
RET_HEADS = 4
RET_DK = 64
RET_DV = 128
ROPE_BASE = 10000.0
FOURIER_GROUPS = 8
TOP_K = 2
NORM_EPS = 1e-6

NA_WIDTH = NA_HEADS * NA_HEAD_DIM
RET_QK_WIDTH = RET_HEADS * RET_DK
RET_V_WIDTH = RET_HEADS * RET_DV

LANES = 128
SUBLANES = 8
VMEM_LIMIT = 48 * 1024 * 1024

NEG_BIG = -1e30
LOG2E = 1.4426950408889634

INPROJ_ROWS = 512
NA_QROWS = 8
NA_KROWS = 16
RET_CHUNK = 512
FFN_ROWS = 1024
FFT_N2 = 64
MOE_BLOCK = 1024


def _cparams(*sem):
    return pltpu.CompilerParams(dimension_semantics=sem, vmem_limit_bytes=VMEM_LIMIT)


def _sigmoid(x):
    return 1.0 / (1.0 + jnp.exp(-x))


def _dot(a, b):
    return jnp.dot(a, b, preferred_element_type=F32)


def _dot_nt(a, b):
    return lax.dot_general(a, b, (((1,), (1,)), ((), ())), preferred_element_type=F32)


def _dot_tn(a, b):
    return lax.dot_general(a, b, (((0,), (0,)), ((), ())), preferred_element_type=F32)


def _rms_mod(x, gain, shift, scale):
    ms = jnp.mean(x * x, axis=-1, keepdims=True)
    return (x * lax.rsqrt(ms + NORM_EPS) * gain) * (1.0 + scale) + shift


def _ada_kernel(c_ref, w_ref, b_ref, o_ref):
    c = c_ref[...]
    s = c * _sigmoid(c)
    o_ref[...] = jnp.dot(s, w_ref[...], precision=HIGHEST, preferred_element_type=F32) + b_ref[...]


def _ada(cond, w_ada, b_ada):
    d, n = w_ada.shape
    cond8 = jnp.zeros((SUBLANES, d), F32).at[:cond.shape[0]].set(cond)
    tn = 1024
    return pl.pallas_call(
        _ada_kernel,
        grid=(n // tn,),
        in_specs=[pl.BlockSpec((SUBLANES, d), lambda j: (0, 0)),
                  pl.BlockSpec((d, tn), lambda j: (0, j)),
                  pl.BlockSpec((1, tn), lambda j: (0, j))],
        out_specs=pl.BlockSpec((SUBLANES, tn), lambda j: (0, j)),
        out_shape=jax.ShapeDtypeStruct((SUBLANES, n), F32),
        compiler_params=_cparams("arbitrary"),
        name="ada",
    )(cond8, w_ada, b_ada.reshape(1, n))


_O_NAQ = 0
_O_RQ = _O_NAQ + NA_WIDTH
_O_RG = _O_RQ + RET_QK_WIDTH
_O_NAK = _O_RG + RET_V_WIDTH
_O_NAV = _O_NAK + NA_WIDTH
_O_RK = _O_NAV + NA_WIDTH
_O_RV = _O_RK + RET_QK_WIDTH
_IN_WIDTH = _O_RV + RET_V_WIDTH


def _inproj_kernel(x_ref, mod_ref, gain_ref, w_ref, qg_ref, kg_ref, gmat_ref, cos_ref, sin_ref,
                   naq_ref, rq_ref, rg_ref, nak_ref, nav_ref, rk_ref, rv_ref):
    h = _rms_mod(x_ref[...], gain_ref[...], mod_ref[0, 0:1, :], mod_ref[0, 1:2, :]).astype(BF16)
    p = _dot(h, w_ref[...])
    gmat = gmat_ref[...]

    def headnorm(t, g):
        ms = _dot((t * t).astype(BF16), gmat)
        return t * lax.rsqrt(ms + NORM_EPS) * g

    cos = cos_ref[...]
    sin = sin_ref[...]
    lane = lax.broadcasted_iota(jnp.int32, (1, RET_QK_WIDTH), 1)
    even = (lane % 2) == 0

    def rope(t):
        partner = jnp.where(even, pltpu.roll(t, RET_QK_WIDTH - 1, axis=1), pltpu.roll(t, 1, axis=1))
        return t * cos + partner * sin

    naq_ref[...] = (headnorm(p[:, _O_NAQ:_O_RQ], qg_ref[...]) * (NA_HEAD_DIM ** -0.5 * LOG2E)).astype(BF16)
    rq_ref[...] = rope(p[:, _O_RQ:_O_RG]).astype(BF16)
    g = p[:, _O_RG:_O_NAK]
    rg_ref[...] = (g * _sigmoid(g)).astype(BF16)
    nak_ref[...] = headnorm(p[:, _O_NAK:_O_NAV], kg_ref[...]).astype(BF16)
    nav_ref[...] = p[:, _O_NAV:_O_RK].astype(BF16)
    rk_ref[...] = (rope(p[:, _O_RK:_O_RV]) * (RET_DK ** -0.5)).astype(BF16)
    rv_ref[...] = p[:, _O_RV:_IN_WIDTH].astype(BF16)


def _inproj(x2d, mods, gain, w_bf, qg, kg, gmat, cos, sin, tm, tiles_per_mod, pos_tiles):
    t, d = x2d.shape
    widths = (NA_WIDTH, RET_QK_WIDTH, RET_V_WIDTH, NA_WIDTH, NA_WIDTH, RET_QK_WIDTH, RET_V_WIDTH)
    row = lambda i: (i, 0)
    const = lambda i: (0, 0)
    return pl.pallas_call(
        _inproj_kernel,
        grid=(t // tm,),
        in_specs=[pl.BlockSpec((tm, d), row),
                  pl.BlockSpec((1, 6, d), lambda i: (i // tiles_per_mod, 0, 0)),
                  pl.BlockSpec((1, d), const),
                  pl.BlockSpec((d, _IN_WIDTH), const),
                  pl.BlockSpec((1, NA_WIDTH), const),
                  pl.BlockSpec((1, NA_WIDTH), const),
                  pl.BlockSpec((NA_WIDTH, NA_WIDTH), const),
                  pl.BlockSpec((tm, RET_QK_WIDTH), lambda i: (i % pos_tiles, 0)),
                  pl.BlockSpec((tm, RET_QK_WIDTH), lambda i: (i % pos_tiles, 0))],
        out_specs=[pl.BlockSpec((tm, w), row) for w in widths],
        out_shape=[jax.ShapeDtypeStruct((t, w), BF16) for w in widths],
        compiler_params=_cparams("arbitrary"),
        name="inproj",
    )(x2d, mods, gain, w_bf, qg, kg, gmat, cos, sin)


def _na_kernel(q_ref, k0, k1, k2, k3, v0, v1, v2, v3, kx_ref, vx_ref, bias_ref, o_ref):
    q2 = q_ref[...]
    lane = lax.broadcasted_iota(jnp.int32, (1, LANES), 1)
    ks = (k0, k1, k2, k3)
    vs = (v0, v1, v2, v3)
    kw = k0.shape[0]
    outs = []
    for hh in range(2):
        sel = (lane // NA_HEAD_DIM) == hh
        qm = jnp.where(sel, q2, jnp.zeros_like(q2))
        s = [_dot_nt(qm, ks[j][...]) + bias_ref[0, hh, :, j * kw:(j + 1) * kw].astype(F32)
             for j in range(4)]
        s.append(_dot_nt(qm, kx_ref[...]))
        m = s[0]
        for t in s[1:-1]:
            m = jnp.maximum(m, t)
        m = jnp.maximum(jnp.max(m, axis=-1, keepdims=True), jnp.max(s[-1], axis=-1, keepdims=True))
        den = jnp.zeros_like(m)
        acc = jnp.zeros((q2.shape[0], LANES), F32)
        vals = vs + (vx_ref,)
        for j in range(5):
            p = jnp.exp2(s[j] - m)
            den = den + jnp.sum(p, axis=-1, keepdims=True)
            acc = acc + _dot(p.astype(BF16), vals[j][...])
        outs.append(acc / den)
    o_ref[...] = jnp.where((lane // NA_HEAD_DIM) == 0, outs[0], outs[1]).astype(o_ref.dtype)


def _na_bias_table(rpb, rows):
    h = rpb.shape[0]
    nqb = rows // NA_QROWS
    row_sel = np.zeros((3, NA_QROWS, NA_KROWS, 2 * NA_WIN_ROWS - 1), np.float32)
    row_ok = np.zeros((3, NA_QROWS, NA_KROWS), bool)
    for v, qb in enumerate((0, min(1, nqb - 1), nqb - 1)):
        start = int(np.clip(NA_QROWS * qb - NA_WIN_ROWS // 2, 0, rows - NA_KROWS))
        for rq in range(NA_QROWS):
            r = NA_QROWS * qb + rq
            r0 = int(np.clip(r - NA_WIN_ROWS // 2, 0, rows - NA_WIN_ROWS))
            for rk in range(NA_KROWS):
                kr = start + rk
                if r0 <= kr < r0 + NA_WIN_ROWS:
                    row_ok[v, rq, rk] = True
                    row_sel[v, rq, rk, kr - r + NA_WIN_ROWS - 1] = 1.0
    col_sel = np.zeros((GRID_W, GRID_W, 2 * NA_WIN_COLS - 1), np.float32)
    col_ok = np.zeros((GRID_W, GRID_W), bool)
    for c in range(GRID_W):
        c0 = int(np.clip(c - NA_WIN_COLS // 2, 0, GRID_W - NA_WIN_COLS))
        for kc in range(c0, c0 + NA_WIN_COLS):
            col_ok[c, kc] = True
            col_sel[c, kc, kc - c + NA_WIN_COLS - 1] = 1.0
    t = jnp.einsum('vqkr,hrc->vhqkc', jnp.asarray(row_sel), rpb.astype(F32), precision=HIGHEST)
    t = jnp.einsum('vhqkc,xyc->vhqxky', t, jnp.asarray(col_sel), precision=HIGHEST)
    ok = row_ok[:, None, :, None, :, None] & col_ok[None, None, None, :, None, :]
    t = jnp.where(jnp.asarray(ok), t * LOG2E, NEG_BIG)
    return t.reshape(3, h, NA_QROWS * GRID_W, NA_KROWS * GRID_W).astype(BF16)


def _na_attention(naq, nak, nav, cx_nak, cx_nav, bias, batch, n, ctx_len):
    rows = n // GRID_W
    qn = NA_QROWS * GRID_W
    kw = NA_KROWS * GRID_W // 4
    nqb = rows // NA_QROWS
    kblocks = n // kw
    npairs = NA_WIDTH // LANES

    def kmap(j):
        def f(hp, qb, b):
            start = jnp.clip(2 * qb - 1, 0, kblocks - 4)
            return (b * kblocks + start + j, hp)
        return f

    def bias_map(hp, qb, b):
        v = jnp.where(qb == 0, 0, jnp.where(qb == nqb - 1, 2, 1))
        return (v, hp, 0, 0)

    kv_specs = [pl.BlockSpec((kw, LANES), kmap(j)) for j in range(4)]
    return pl.pallas_call(
        _na_kernel,
        grid=(npairs, nqb, batch),
        in_specs=[pl.BlockSpec((qn, LANES), lambda hp, qb, b: (b * nqb + qb, hp))]
                 + kv_specs + kv_specs
                 + [pl.BlockSpec((ctx_len, LANES), lambda hp, qb, b: (b, hp)),
                    pl.BlockSpec((ctx_len, LANES), lambda hp, qb, b: (b, hp)),
                    pl.BlockSpec((1, 2, qn, 4 * kw), bias_map)],
        out_specs=pl.BlockSpec((qn, LANES), lambda hp, qb, b: (b * nqb + qb, hp)),
        out_shape=jax.ShapeDtypeStruct((batch * n, NA_WIDTH), BF16),
        compiler_params=_cparams("arbitrary", "arbitrary", "arbitrary"),
        name="na_attention",
    )(naq, nak, nak, nak, nak, nav, nav, nav, nav, cx_nak, cx_nav, bias)


def _ret_kernel(reverse, lg_ref, q_ref, k_ref, v_ref, kx_ref, vx_ref, *rest):
    if reverse:
        of_ref, gate_ref, o_ref, s_ref, d_ref, xi_ref, zeta_ref = rest
    else:
        o_ref, s_ref, d_ref, xi_ref, zeta_ref = rest
    c = q_ref.shape[0]
    first = (pl.program_id(0) == 0) & (pl.program_id(1) == 0)
    lane = lax.broadcasted_iota(jnp.int32, (1, LANES), 1)

    @pl.when(first)
    def _():
        i = lax.broadcasted_iota(jnp.int32, (c, c), 0)
        j = lax.broadcasted_iota(jnp.int32, (c, c), 1)
        diff = (j - i) if reverse else (i - j)
        difff = jnp.maximum(diff, 0).astype(F32)
        r = lax.broadcasted_iota(jnp.int32, (c, LANES), 0).astype(F32)
        for h in range(RET_HEADS):
            lg = lg_ref[h]
            d_ref[h] = jnp.where(diff >= 0, jnp.exp(difff * lg), 0.0)
            xi_ref[h] = jnp.exp(((c - r) if reverse else (r + 1.0)) * lg)
        for p in range(RET_HEADS // 2):
            lgp = jnp.where(lane < RET_DK, lg_ref[2 * p], lg_ref[2 * p + 1])
            zeta_ref[p] = jnp.exp((r if reverse else (c - 1.0 - r)) * lgp)

    @pl.when(pl.program_id(1) == 0)
    def _():
        l = kx_ref.shape[0]
        r = lax.broadcasted_iota(jnp.int32, (l, LANES), 0).astype(F32)
        for p in range(RET_HEADS // 2):
            lgp = jnp.where(lane < RET_DK, lg_ref[2 * p], lg_ref[2 * p + 1])
            zc = jnp.exp((r if reverse else (l - 1.0 - r)) * lgp)
            kz = kx_ref[:, p * LANES:(p + 1) * LANES].astype(F32) * zc
            for hh in range(2):
                h = 2 * p + hh
                km = jnp.where((lane // RET_DK) == hh, kz, 0.0).astype(BF16)
                s_ref[h] = _dot_tn(km, vx_ref[:, h * RET_DV:(h + 1) * RET_DV])

    outs = []
    for p in range(RET_HEADS // 2):
        q2 = q_ref[:, p * LANES:(p + 1) * LANES]
        k2 = k_ref[:, p * LANES:(p + 1) * LANES]
        kz = k2.astype(F32) * zeta_ref[p]
        for hh in range(2):
            h = 2 * p + hh
            sel = (lane // RET_DK) == hh
            qm = jnp.where(sel, q2, jnp.zeros_like(q2))
            km = jnp.where(sel, k2, jnp.zeros_like(k2))
            vh = v_ref[:, h * RET_DV:(h + 1) * RET_DV]
            inner = (_dot_nt(qm, km) * d_ref[h]).astype(BF16)
            state = s_ref[h]
            o = _dot(inner, vh) + _dot(qm, state.astype(BF16)) * xi_ref[h]
            g_chunk = jnp.exp(jnp.full((1, LANES), c, F32) * lg_ref[h])
            s_ref[h] = state * g_chunk + _dot_tn(jnp.where(sel, kz, 0.0).astype(BF16), vh)
            outs.append(o)
    if reverse:
        for h in range(RET_HEADS):
            tot = outs[h] + of_ref[:, h * RET_DV:(h + 1) * RET_DV]
            ms = jnp.mean(tot * tot, axis=-1, keepdims=True)
            gate = gate_ref[:, h * RET_DV:(h + 1) * RET_DV].astype(F32)
            o_ref[:, h * RET_DV:(h + 1) * RET_DV] = (tot * lax.rsqrt(ms + NORM_EPS) * gate).astype(o_ref.dtype)
    else:
        for h in range(RET_HEADS):
            o_ref[:, h * RET_DV:(h + 1) * RET_DV] = outs[h]


def _retention_pass(reverse, log_g, rq, rk, rv, cx_rk, cx_rv, batch, n, ctx_len, o_fwd=None, gate=None):
    c = min(RET_CHUNK, n)
    nc = n // c
    if reverse:
        seq = lambda b, i: (b * nc + nc - 1 - i, 0)
    else:
        seq = lambda b, i: (b * nc + i, 0)
    ctx = lambda b, i: (b, 0)
    in_specs = [pl.BlockSpec(memory_space=pltpu.SMEM),
                pl.BlockSpec((c, RET_QK_WIDTH), seq),
                pl.BlockSpec((c, RET_QK_WIDTH), seq),
                pl.BlockSpec((c, RET_V_WIDTH), seq),
                pl.BlockSpec((ctx_len, RET_QK_WIDTH), ctx),
                pl.BlockSpec((ctx_len, RET_V_WIDTH), ctx)]
    args = [log_g, rq, rk, rv, cx_rk, cx_rv]
    if reverse:
        in_specs += [pl.BlockSpec((c, RET_V_WIDTH), seq), pl.BlockSpec((c, RET_V_WIDTH), seq)]
        args += [o_fwd, gate]
    return pl.pallas_call(
        functools.partial(_ret_kernel, reverse),
        grid=(batch, nc),
        in_specs=in_specs,
        out_specs=pl.BlockSpec((c, RET_V_WIDTH), seq),
        out_shape=jax.ShapeDtypeStruct((batch * n, RET_V_WIDTH), BF16 if reverse else F32),
        scratch_shapes=[pltpu.VMEM((RET_HEADS, LANES, RET_DV), F32),
                        pltpu.VMEM((RET_HEADS, c, c), F32),
                        pltpu.VMEM((RET_HEADS, c, RET_DV), F32),
                        pltpu.VMEM((RET_HEADS // 2, c, LANES), F32)],
        compiler_params=_cparams("arbitrary", "arbitrary"),
        name="retention_bwd" if reverse else "retention_fwd",
    )(*args)


def _outproj_kernel(x_ref, ona_ref, oret_ref, w_ref, mod_ref, gain_ref, x1_ref, h_ref):
    mix = _dot(ona_ref[...], w_ref[0:NA_WIDTH, :]) + _dot(oret_ref[...], w_ref[NA_WIDTH:, :])
    x1 = x_ref[...] + mod_ref[0, 2:3, :] * mix
    x1_ref[...] = x1
    h_ref[...] = _rms_mod(x1, gain_ref[...], mod_ref[0, 3:4, :], mod_ref[0, 4:5, :]).astype(BF16)


def _outproj(x2d, ona, oret, w_out_bf, mods, gain, tm, tiles_per_batch):
    t, d = x2d.shape
    row = lambda i: (i, 0)
    const = lambda i: (0, 0)
    return pl.pallas_call(
        _outproj_kernel,
        grid=(t // tm,),
        in_specs=[pl.BlockSpec((tm, d), row),
                  pl.BlockSpec((tm, NA_WIDTH), row),
                  pl.BlockSpec((tm, RET_V_WIDTH), row),
                  pl.BlockSpec((NA_WIDTH + RET_V_WIDTH, d), const),
                  pl.BlockSpec((1, 6, d), lambda i: (i // tiles_per_batch, 0, 0)),
                  pl.BlockSpec((1, d), const)],
        out_specs=[pl.BlockSpec((tm, d), row), pl.BlockSpec((tm, d), row)],
        out_shape=[jax.ShapeDtypeStruct((t, d), F32), jax.ShapeDtypeStruct((t, d), BF16)],
        compiler_params=_cparams("arbitrary"),
        name="outproj",
    )(x2d, ona, oret, w_out_bf, mods, gain)


def _swiglu_kernel(h_ref, x1_ref, wg_ref, wu_ref, wd_ref, mod_ref, o_ref, acc_ref):
    j = pl.program_id(1)

    @pl.when(j == 0)
    def _():
        acc_ref[...] = jnp.zeros_like(acc_ref)

    h = h_ref[...]
    g = _dot(h, wg_ref[...])
    u = _dot(h, wu_ref[...])
    a = (g * _sigmoid(g) * u).astype(BF16)
    acc_ref[...] += _dot(a, wd_ref[...])

    @pl.when(j == pl.num_programs(1) - 1)
    def _():
        o_ref[...] = x1_ref[...] + mod_ref[0, 5:6, :] * acc_ref[...]


def _swiglu(h_bf, x1, w_gate_up_bf, w_down_bf, mods, tm, tiles_per_batch):
    t, d = x1.shape
    f = w_down_bf.shape[0]
    nf = 2 if (f // 2) % LANES == 0 else 1
    tf = f // nf
    return pl.pallas_call(
        _swiglu_kernel,
        grid=(t // tm, nf),
        in_specs=[pl.BlockSpec((tm, d), lambda i, j: (i, 0)),
                  pl.BlockSpec((tm, d), lambda i, j: (i, 0)),
                  pl.BlockSpec((d, tf), lambda i, j: (0, j)),
                  pl.BlockSpec((d, tf), lambda i, j: (0, nf + j)),
                  pl.BlockSpec((tf, d), lambda i, j: (j, 0)),
                  pl.BlockSpec((1, 6, d), lambda i, j: (i // tiles_per_batch, 0, 0))],
        out_specs=pl.BlockSpec((tm, d), lambda i, j: (i, 0)),
        out_shape=jax.ShapeDtypeStruct((t, d), F32),
        scratch_shapes=[pltpu.VMEM((tm, d), F32)],
        compiler_params=_cparams("arbitrary", "arbitrary"),
        name="swiglu",
    )(h_bf, x1, w_gate_up_bf, w_gate_up_bf, w_down_bf, mods)


def _dft_tables(n, d):
    n2 = FFT_N2
    n1 = n // n2
    s = SUBLANES
    eye = np.eye(s)
    ang1 = 2.0 * np.pi * (np.outer(np.arange(n1), np.arange(n1)) % n1) / n1
    f1 = np.stack([np.cos(ang1), -np.sin(ang1)]) / np.sqrt(n)
    f1k = np.einsum('ckn,jl->ckjnl', f1, eye).reshape(2 * n1 * s, n1 * s)
    angt = 2.0 * np.pi * (np.outer(np.arange(n1), np.arange(n2)) % n) / n
    tw = np.stack([np.cos(angt), -np.sin(angt)])
    tw = tw.reshape(2, n1, n2 // s, s).transpose(0, 2, 1, 3).reshape(2, n2 // s, n1 * s, 1)
    tw = np.broadcast_to(tw, (2, n2 // s, n1 * s, LANES))
    ang2 = 2.0 * np.pi * (np.outer(np.arange(n2), np.arange(n2)) % n2) / n2
    cr, ci = np.cos(ang2), -np.sin(ang2)
    m2 = np.array([[cr, -ci], [ci, cr]])
    m2k = np.einsum('abkn,jl->akjbln', m2, eye).reshape(2 * n2 * s, 2 * s * n2)
    gw = d // FOURIER_GROUPS
    angc = 2.0 * np.pi * (np.outer(np.arange(gw), np.arange(gw)) % gw) / gw
    cs = np.concatenate([np.cos(angc), np.sin(angc)], axis=0) / np.sqrt(gw)
    return (f1k.astype(np.float32), tw.astype(np.float32), m2k.astype(np.float32),
            cs.astype(np.float32))


def _fold_kernel(cs_ref, w_ref, o_ref):
    gw = w_ref.shape[0]
    o_ref[0] = jnp.dot(cs_ref[0:gw, :], w_ref[...], precision=HIGHEST, preferred_element_type=F32)
    o_ref[1] = jnp.dot(cs_ref[gw:, :], w_ref[...], precision=HIGHEST, preferred_element_type=F32)


def _fold_channel_dft(cs, w_fourier):
    d = w_fourier.shape[0]
    gw = d // FOURIER_GROUPS
    return pl.pallas_call(
        _fold_kernel,
        grid=(FOURIER_GROUPS,),
        in_specs=[pl.BlockSpec((2 * gw, gw), lambda g: (0, 0)),
                  pl.BlockSpec((gw, d), lambda g: (g, 0))],
        out_specs=pl.BlockSpec((2, gw, d), lambda g: (0, g, 0)),
        out_shape=jax.ShapeDtypeStruct((2, d, d), F32),
        compiler_params=_cparams("arbitrary"),
        name="fold_channel_dft",
    )(cs, w_fourier)


def _fft1_kernel(x_ref, mod_ref, gain_ref, f_ref, tw_ref, o_ref):
    _, n1, s, d = x_ref.shape
    x = x_ref[...].reshape(n1 * s, d)
    h = _rms_mod(x, gain_ref[...], mod_ref[0, 0:1, :], mod_ref[0, 1:2, :]).astype(BF16)
    a = _dot(f_ref[...], h)
    ar = a[:n1 * s]
    ai = a[n1 * s:]
    reps = d // LANES
    twr = jnp.tile(tw_ref[0, 0], (1, reps))
    twi = jnp.tile(tw_ref[1, 0], (1, reps))
    o_ref[0, 0] = (ar * twr - ai * twi).reshape(n1, s, d)
    o_ref[0, 1] = (ar * twi + ai * twr).reshape(n1, s, d)


def _fft_stage1(x2, mods, gain, f1k_bf, tw, batch, n):
    d = x2.shape[-1]
    n2 = FFT_N2
    n1 = n // n2
    s = SUBLANES
    xv = x2.reshape(batch, n1, n2, d)
    return pl.pallas_call(
        _fft1_kernel,
        grid=(n2 // s, batch),
        in_specs=[pl.BlockSpec((1, n1, s, d), lambda g, b: (b, 0, g, 0)),
                  pl.BlockSpec((1, 6, d), lambda g, b: (b, 0, 0)),
                  pl.BlockSpec((1, d), lambda g, b: (0, 0)),
                  pl.BlockSpec((2 * n1 * s, n1 * s), lambda g, b: (0, 0)),
                  pl.BlockSpec((2, 1, n1 * s, LANES), lambda g, b: (0, g, 0, 0))],
        out_specs=pl.BlockSpec((1, 2, n1, s, d), lambda g, b: (b, 0, 0, g, 0)),
        out_shape=jax.ShapeDtypeStruct((batch, 2, n1, n2, d), F32),
        compiler_params=_cparams("arbitrary", "arbitrary"),
        name="fft_stage1",
    )(xv, mods, gain, f1k_bf, tw)


def _fft2_kernel(n_experts, b_ref, x_ref, m_ref, wf_ref, mod_ref, gain_ref, wr_ref,
                 x3_ref, h2_ref, route_ref):
    _, _, s, n2, d = b_ref.shape
    rows = n2 * s
    bmat = b_ref[...].reshape(2 * s * n2, d).astype(BF16)
    z = _dot(m_ref[...], bmat)
    y = _dot(z[:rows].astype(BF16), wf_ref[0]) + _dot(z[rows:].astype(BF16), wf_ref[1])
    x3 = x_ref[...].reshape(rows, d) + mod_ref[0, 2:3, :] * y
    x3_ref[...] = x3.reshape(x3_ref.shape)
    h2 = _rms_mod(x3, gain_ref[...], mod_ref[0, 3:4, :], mod_ref[0, 4:5, :])
    h2_ref[...] = h2.reshape(rows, d // LANES, LANES).reshape(h2_ref.shape)
    h_hi = h2.astype(BF16)
    h_lo = (h2 - h_hi.astype(F32)).astype(BF16)
    wr = wr_ref[...]
    w_hi = wr.astype(BF16)
    w_lo = (wr - w_hi.astype(F32)).astype(BF16)
    logits = _dot(h_hi, w_hi) + (_dot(h_lo, w_hi) + _dot(h_hi, w_lo))
    lane = lax.broadcasted_iota(jnp.int32, logits.shape, 1).astype(F32)
    lg = jnp.where(lane < n_experts, logits, -jnp.inf)
    m1 = jnp.max(lg, axis=-1, keepdims=True)
    i1 = jnp.min(jnp.where(lg == m1, lane, float(LANES)), axis=-1, keepdims=True)
    lg2 = jnp.where(lane == i1, -jnp.inf, lg)
    m2 = jnp.max(lg2, axis=-1, keepdims=True)
    i2 = jnp.min(jnp.where(lg2 == m2, lane, float(LANES)), axis=-1, keepdims=True)
    e = jnp.exp(m2 - m1)
    g1 = 1.0 / (1.0 + e)
    g2 = e / (1.0 + e)
    route = jnp.where(lane == 0, g1, jnp.where(lane == 1, g2, jnp.where(lane == 2, i1,
                                                                       jnp.where(lane == 3, i2, 0.0))))
    route_ref[...] = route.reshape(route_ref.shape)


def _fft_stage2(bmid, x2, m2k_bf, wf_bf, mods, gain, wr_pad, n_experts, batch, n):
    d = x2.shape[-1]
    n2 = FFT_N2
    n1 = n // n2
    s = SUBLANES
    xv = x2.reshape(batch, n2, n1, d)
    pos = lambda g, b: (b, 0, g, 0)
    const2 = lambda g, b: (0, 0)
    x3, h2, route = pl.pallas_call(
        functools.partial(_fft2_kernel, n_experts),
        grid=(n1 // s, batch),
        in_specs=[pl.BlockSpec((1, 2, s, n2, d), lambda g, b: (b, 0, g, 0, 0)),
                  pl.BlockSpec((1, n2, s, d), pos),
                  pl.BlockSpec((2 * n2 * s, 2 * n2 * s), const2),
                  pl.BlockSpec((2, d, d), lambda g, b: (0, 0, 0)),
                  pl.BlockSpec((1, 6, d), lambda g, b: (b, 0, 0)),
                  pl.BlockSpec((1, d), const2),
                  pl.BlockSpec((d, LANES), const2)],
        out_specs=[pl.BlockSpec((1, n2, s, d), pos),
                   pl.BlockSpec((1, n2, s, d // LANES, LANES), lambda g, b: (b, 0, g, 0, 0)),
                   pl.BlockSpec((1, n2, s, LANES), pos)],
        out_shape=[jax.ShapeDtypeStruct((batch, n2, n1, d), F32),
                   jax.ShapeDtypeStruct((batch, n2, n1, d // LANES, LANES), F32),
                   jax.ShapeDtypeStruct((batch, n2, n1, LANES), F32)],
        compiler_params=_cparams("arbitrary", "arbitrary"),
        name="fft_stage2",
    )(bmid, xv, m2k_bf, wf_bf, mods, gain, wr_pad)
    t = batch * n
    return x3.reshape(t, d), h2.reshape(t, d // LANES, LANES), route.reshape(t, LANES)


def _dispatch_kernel(tm, slot_ref, h_ref, dst_in_ref, dst_ref, sem):
    del dst_in_ref
    base = pl.program_id(0) * tm * TOP_K

    def body(r, carry):
        for k in range(TOP_K):
            s = slot_ref[base + TOP_K * r + k]
            pltpu.make_async_copy(h_ref.at[pl.ds(r, 1)], dst_ref.at[pl.ds(s, 1)], sem).start(priority=k % 2)
        return carry

    lax.fori_loop(0, tm, body, 0, unroll=8)

    def drain(r, carry):
        pltpu.make_async_copy(h_ref.at[pl.ds(0, 1)], dst_ref.at[pl.ds(0, 1)], sem).wait()
        return carry

    lax.fori_loop(0, tm * TOP_K, drain, 0, unroll=8)


def _dispatch(h2, slot, n_slots, tm):
    t = h2.shape[0]
    tile = h2.shape[1:]
    return pl.pallas_call(
        functools.partial(_dispatch_kernel, tm),
        grid_spec=pltpu.PrefetchScalarGridSpec(
            num_scalar_prefetch=1,
            grid=(t // tm,),
            in_specs=[pl.BlockSpec((tm,) + tile, lambda i, sl: (i, 0, 0)),
                      pl.BlockSpec(memory_space=pl.ANY)],
            out_specs=pl.BlockSpec(memory_space=pl.ANY),
            scratch_shapes=[pltpu.SemaphoreType.DMA(())]),
        out_shape=jax.ShapeDtypeStruct((n_slots,) + tile, h2.dtype),
        input_output_aliases={2: 0},
        compiler_params=pltpu.CompilerParams(dimension_semantics=("arbitrary",), has_side_effects=True,
                                             vmem_limit_bytes=VMEM_LIMIT),
        name="dispatch",
    )(slot, h2, jnp.zeros((n_slots,) + tile, h2.dtype))


def _expert_kernel(bexp_ref, nused_ref, x_ref, wg_ref, wu_ref, wd_ref, o_ref, xb_ref, acc_ref):
    i = pl.program_id(0)
    j = pl.program_id(1)
    last = pl.num_programs(1) - 1
    used = i < nused_ref[0]

    @pl.when(used)
    def _():
        @pl.when(j == 0)
        def _():
            xb_ref[...] = x_ref[...].reshape(xb_ref.shape).astype(BF16)
            acc_ref[...] = jnp.zeros_like(acc_ref)

        xb = xb_ref[...]
        g = _dot(xb, wg_ref[0])
        u = _dot(xb, wu_ref[0])
        a = (g * _sigmoid(g) * u).astype(BF16)
        acc_ref[...] += _dot(a, wd_ref[0])

        @pl.when(j == last)
        def _():
            o_ref[...] = acc_ref[...].reshape(o_ref.shape)

    @pl.when(jnp.logical_not(used) & (j == last))
    def _():
        o_ref[...] = jnp.zeros_like(o_ref)


def _expert_ffn(xs, w_gu_bf, w_dn_bf, block_expert, n_used):
    n_slots, sub, _ = xs.shape
    e, f, d = w_dn_bf.shape
    nb = n_slots // MOE_BLOCK
    nf = 4 if (f // 4) % LANES == 0 else 1
    tf = f // nf

    def xmap(i, j, be, nu):
        return (jnp.minimum(i, nu[0] - 1), 0, 0)

    def jeff(i, j, nu):
        return jnp.where(i < nu[0], j, nf - 1)

    return pl.pallas_call(
        _expert_kernel,
        grid_spec=pltpu.PrefetchScalarGridSpec(
            num_scalar_prefetch=2,
            grid=(nb, nf),
            in_specs=[pl.BlockSpec((MOE_BLOCK, sub, LANES), xmap),
                      pl.BlockSpec((1, d, tf), lambda i, j, be, nu: (be[i], 0, jeff(i, j, nu))),
                      pl.BlockSpec((1, d, tf), lambda i, j, be, nu: (be[i], 0, nf + jeff(i, j, nu))),
                      pl.BlockSpec((1, tf, d), lambda i, j, be, nu: (be[i], jeff(i, j, nu), 0))],
            out_specs=pl.BlockSpec((MOE_BLOCK, sub, LANES), lambda i, j, be, nu: (i, 0, 0)),
            scratch_shapes=[pltpu.VMEM((MOE_BLOCK, d), BF16), pltpu.VMEM((MOE_BLOCK, d), F32)]),
        out_shape=jax.ShapeDtypeStruct((n_slots, sub, LANES), F32),
        compiler_params=_cparams("arbitrary", "arbitrary"),
        name="expert_ffn",
    )(block_expert, n_used, xs, w_gu_bf, w_gu_bf, w_dn_bf)


def _combine_kernel(tm, slot_ref, x_ref, route_ref, mod_ref, ys_ref, o_ref, ybuf, sem):
    i = pl.program_id(0)

    def issue(step, b):
        base = step * (tm * TOP_K)

        def body(r, carry):
            for k in range(TOP_K):
                s = slot_ref[base + TOP_K * r + k]
                pltpu.make_async_copy(ys_ref.at[pl.ds(s, 1)], ybuf.at[b, k, pl.ds(r, 1)],
                                      sem.at[b]).start(priority=k % 2)
            return carry

        lax.fori_loop(0, tm, body, 0, unroll=8)

    @pl.when(i == 0)
    def _():
        issue(0, 0)

    @pl.when(i + 1 < pl.num_programs(0))
    def _():
        issue(i + 1, (i + 1) % 2)

    b = i % 2

    def drain(r, carry):
        pltpu.make_async_copy(ys_ref.at[pl.ds(0, 1)], ybuf.at[b, 0, pl.ds(0, 1)], sem.at[b]).wait()
        return carry

    lax.fori_loop(0, tm * TOP_K, drain, 0, unroll=8)
    y0 = ybuf[b, 0].reshape(x_ref.shape)
    y1 = ybuf[b, 1].reshape(x_ref.shape)
    g0 = route_ref[:, 0:1]
    g1 = route_ref[:, 1:2]
    o_ref[...] = x_ref[...] + mod_ref[0, 5:6, :] * (g0 * y0 + g1 * y1)


def _combine(x3, ys, slot, route, mods, tm, tiles_per_batch):
    t, d = x3.shape
    tile = ys.shape[1:]
    row = lambda i, sl: (i, 0)
    return pl.pallas_call(
        functools.partial(_combine_kernel, tm),
        grid_spec=pltpu.PrefetchScalarGridSpec(
            num_scalar_prefetch=1,
            grid=(t // tm,),
            in_specs=[pl.BlockSpec((tm, d), row),
                      pl.BlockSpec((tm, LANES), row),
                      pl.BlockSpec((1, 6, d), lambda i, sl: (i // tiles_per_batch, 0, 0)),
                      pl.BlockSpec(memory_space=pl.ANY)],
            out_specs=pl.BlockSpec((tm, d), row),
            scratch_shapes=[pltpu.VMEM((2, TOP_K, tm) + tile, F32), pltpu.SemaphoreType.DMA((2,))]),
        out_shape=jax.ShapeDtypeStruct((t, d), F32),
        compiler_params=_cparams("arbitrary"),
        name="combine",
    )(slot, x3, route, mods, ys)


def _route_slots(route, n_experts):
    t = route.shape[0]
    expert = route[:, 2:2 + TOP_K].astype(jnp.int32).reshape(t * TOP_K)
    onehot = (expert[:, None] == jnp.arange(n_experts, dtype=jnp.int32)[None, :]).astype(jnp.int32)
    csum = jnp.cumsum(onehot, axis=0)
    counts = csum[-1]
    padded = (counts + MOE_BLOCK - 1) // MOE_BLOCK * MOE_BLOCK
    pad_end = jnp.cumsum(padded)
    pad_start = pad_end - padded
    slot = jnp.sum(onehot * (pad_start[None, :] + csum - 1), axis=1).astype(jnp.int32)
    n_slots = t * TOP_K + n_experts * MOE_BLOCK
    nb = n_slots // MOE_BLOCK
    block_expert = jnp.minimum(
        jnp.searchsorted(pad_end, jnp.arange(nb, dtype=jnp.int32) * MOE_BLOCK, side='right'),
        n_experts - 1).astype(jnp.int32)
    n_used = (pad_end[-1:] // MOE_BLOCK).astype(jnp.int32)
    return slot, block_expert, n_used, n_slots


def _rope_tables(n):
    pos = np.arange(n)
    row = (pos // GRID_W).astype(np.float32)
    col = (pos % GRID_W).astype(np.float32)
    n_freq = RET_DK // 4
    inv_freq = (np.float32(ROPE_BASE) ** (-np.arange(n_freq, dtype=np.float32) / n_freq)).astype(np.float32)
    ang = np.concatenate([row[:, None] * inv_freq, col[:, None] * inv_freq], axis=-1)
    cos = np.cos(ang.astype(np.float64))
    sin = np.sin(ang.astype(np.float64))
    cos_h = np.repeat(cos, 2, axis=-1)
    sin_h = np.stack([-sin, sin], axis=-1).reshape(n, RET_DK)
    return (np.tile(cos_h, (1, RET_HEADS)).astype(np.float32),
            np.tile(sin_h, (1, RET_HEADS)).astype(np.float32))


def kernel(x, c, ctx, c_ctx, l0_norm_mix, l0_norm_ffn, l0_w_ada, l0_b_ada, l0_w_in, l0_q_gain, l0_k_gain, l0_rpb, l0_ret_decay_fwd, l0_ret_decay_bwd, l0_w_out, l0_w_gate_up, l0_w_down, l1_norm_mix, l1_norm_ffn, l1_w_ada, l1_b_ada, l1_w_fourier, l1_w_router, l1_w_exp_gate_up, l1_w_exp_down):
    batch, n, d = x.shape
    ctx_len = ctx.shape[1]
    t = batch * n
    rows = n // GRID_W
    n_experts = l1_w_router.shape[1]
    assert n % (NA_QROWS * GRID_W) == 0 and rows >= NA_KROWS and n % FFT_N2 == 0
    assert l0_w_in.shape[1] == _IN_WIDTH and d % LANES == 0

    x2d = x.reshape(t, d)
    ctx2d = ctx.reshape(batch * ctx_len, d)

    ada0 = _ada(jnp.concatenate([c, c_ctx[None, :]], axis=0), l0_w_ada, l0_b_ada)
    mods0 = ada0[:batch].reshape(batch, 6, d)
    mods_ctx = ada0[batch:batch + 1].reshape(1, 6, d)
    mods1 = _ada(c, l1_w_ada, l1_b_ada)[:batch].reshape(batch, 6, d)

    w_in_bf = l0_w_in.astype(BF16)
    qg = jnp.tile(l0_q_gain.astype(F32), NA_HEADS).reshape(1, NA_WIDTH)
    kg = jnp.tile(l0_k_gain.astype(F32), NA_HEADS).reshape(1, NA_WIDTH)
    gmat = jnp.asarray(np.kron(np.eye(NA_HEADS), np.full((NA_HEAD_DIM, NA_HEAD_DIM), 1.0 / NA_HEAD_DIM)),
                       BF16)
    cos_np, sin_np = _rope_tables(n)
    gain_mix0 = l0_norm_mix.reshape(1, d)
    tm = min(INPROJ_ROWS, n)
    naq, rq, rg, nak, nav, rk, rv = _inproj(
        x2d, mods0, gain_mix0, w_in_bf, qg, kg, gmat, jnp.asarray(cos_np), jnp.asarray(sin_np),
        tm, n // tm, n // tm)
    ctx_rows = batch * ctx_len
    _, _, _, cx_nak, cx_nav, cx_rk, cx_rv = _inproj(
        ctx2d, mods_ctx, gain_mix0, w_in_bf, qg, kg, gmat,
        jnp.ones((ctx_len, RET_QK_WIDTH), F32), jnp.zeros((ctx_len, RET_QK_WIDTH), F32),
        ctx_len, ctx_rows // ctx_len, 1)

    bias = _na_bias_table(l0_rpb, rows)
    o_na = _na_attention(naq, nak, nav, cx_nak, cx_nav, bias, batch, n, ctx_len)

    lg_f = jax.nn.log_sigmoid(l0_ret_decay_fwd.astype(F32))
    lg_b = jax.nn.log_sigmoid(l0_ret_decay_bwd.astype(F32))
    o_f = _retention_pass(False, lg_f, rq, rk, rv, cx_rk, cx_rv, batch, n, ctx_len)
    o_ret = _retention_pass(True, lg_b, rq, rk, rv, cx_rk, cx_rv, batch, n, ctx_len, o_fwd=o_f, gate=rg)

    x1, h_ffn = _outproj(x2d, o_na, o_ret, l0_w_out.astype(BF16), mods0, l0_norm_ffn.reshape(1, d),
                         tm, n // tm)
    tf_rows = min(FFN_ROWS, n)
    x2 = _swiglu(h_ffn, x1, l0_w_gate_up.astype(BF16), l0_w_down.astype(BF16), mods0,
                 tf_rows, n // tf_rows)

    f1k, tw, m2k, cs = _dft_tables(n, d)
    wf = _fold_channel_dft(jnp.asarray(cs), l1_w_fourier).astype(BF16)
    bmid = _fft_stage1(x2, mods1, l1_norm_mix.reshape(1, d), jnp.asarray(f1k, BF16), jnp.asarray(tw),
                       batch, n)
    wr_pad = jnp.zeros((d, LANES), F32).at[:, :n_experts].set(l1_w_router)
    x3, h2, route = _fft_stage2(bmid, x2, jnp.asarray(m2k, BF16), wf, mods1, l1_norm_ffn.reshape(1, d),
                                wr_pad, n_experts, batch, n)

    slot, block_expert, n_used, n_slots = _route_slots(route, n_experts)
    xs = _dispatch(h2, slot, n_slots, tm)
    ys = _expert_ffn(xs, l1_w_exp_gate_up.astype(BF16), l1_w_exp_down.astype(BF16), block_expert, n_used)
    out = _combine(x3, ys, slot, route, mods1, tm, n // tm)
    return out.reshape(batch, n, d)
```

```python
import functools

import numpy as np
import jax
import jax.numpy as jnp
from jax import lax
from jax.experimental import pallas as pl
from jax.experimental.pallas import tpu as pltpu

F32 = jnp.float32
BF16 = jnp.bfloat16
HIGHEST = lax.Precision.HIGHEST

GRID_W = 64
NA_HEADS = 8
NA_HEAD_DIM = 64
NA_WIN_ROWS = 8
NA_WIN_COLS = 16
RET_HEADS = 4
RET_DK = 64
RET_DV = 128
ROPE_BASE = 10000.0
FOURIER_GROUPS = 8
TOP_K = 2
NORM_EPS = 1e-6

NA_WIDTH = NA_HEADS * NA_HEAD_DIM
RET_QK_WIDTH = RET_HEADS * RET_DK
RET_V_WIDTH = RET_HEADS * RET_DV

LANES = 128
SUBLANES = 8
VMEM_LIMIT = 48 * 1024 * 1024

NEG_BIG = -1e30
LOG2E = 1.4426950408889634

INPROJ_ROWS = 512
NA_QROWS = 8
NA_KROWS = 16
NA_PAIRS_PER_STEP = 2
RET_CHUNK = 512
FFN_ROWS = 1024
FFT_N2 = 64
MOE_BLOCK = 1024


def _cparams(*sem):
    return pltpu.CompilerParams(dimension_semantics=sem, vmem_limit_bytes=VMEM_LIMIT)


def _sigmoid(x):
    return 1.0 / (1.0 + jnp.exp(-x))


def _dot(a, b):
    return jnp.dot(a, b, preferred_element_type=F32)


def _dot_nt(a, b):
    return lax.dot_general(a, b, (((1,), (1,)), ((), ())), preferred_element_type=F32)


def _dot_tn(a, b):
    return lax.dot_general(a, b, (((0,), (0,)), ((), ())), preferred_element_type=F32)


def _rms_mod(x, gain, shift, scale):
    ms = jnp.mean(x * x, axis=-1, keepdims=True)
    return (x * lax.rsqrt(ms + NORM_EPS) * gain) * (1.0 + scale) + shift


def _ada_kernel(c_ref, w_ref, b_ref, o_ref):
    c = c_ref[...]
    s = c * _sigmoid(c)
    o_ref[...] = jnp.dot(s, w_ref[...], precision=HIGHEST, preferred_element_type=F32) + b_ref[...]


def _ada(cond, w_ada, b_ada):
    d, n = w_ada.shape
    cond8 = jnp.zeros((SUBLANES, d), F32).at[:cond.shape[0]].set(cond)
    tn = 1024
    return pl.pallas_call(
        _ada_kernel,
        grid=(n // tn,),
        in_specs=[pl.BlockSpec((SUBLANES, d), lambda j: (0, 0)),
                  pl.BlockSpec((d, tn), lambda j: (0, j)),
                  pl.BlockSpec((1, tn), lambda j: (0, j))],
        out_specs=pl.BlockSpec((SUBLANES, tn), lambda j: (0, j)),
        out_shape=jax.ShapeDtypeStruct((SUBLANES, n), F32),
        compiler_params=_cparams("arbitrary"),
        name="ada",
    )(cond8, w_ada, b_ada.reshape(1, n))


_O_NAQ = 0
_O_RQ = _O_NAQ + NA_WIDTH
_O_RG = _O_RQ + RET_QK_WIDTH
_O_NAK = _O_RG + RET_V_WIDTH
_O_NAV = _O_NAK + NA_WIDTH
_O_RK = _O_NAV + NA_WIDTH
_O_RV = _O_RK + RET_QK_WIDTH
_IN_WIDTH = _O_RV + RET_V_WIDTH


def _inproj_kernel(x_ref, mod_ref, gain_ref, w_ref, qg_ref, kg_ref, gmat_ref, cos_ref, sin_ref,
                   naq_ref, rq_ref, rg_ref, nak_ref, nav_ref, rk_ref, rv_ref):
    h = _rms_mod(x_ref[...], gain_ref[...], mod_ref[0, 0:1, :], mod_ref[0, 1:2, :]).astype(BF16)
    p = _dot(h, w_ref[...])
    gmat = gmat_ref[...]

    def headnorm(t, g):
        ms = _dot((t * t).astype(BF16), gmat)
        return t * lax.rsqrt(ms + NORM_EPS) * g

    cos = cos_ref[...]
    sin = sin_ref[...]
    lane = lax.broadcasted_iota(jnp.int32, (1, RET_QK_WIDTH), 1)
    even = (lane % 2) == 0

    def rope(t):
        partner = jnp.where(even, pltpu.roll(t, RET_QK_WIDTH - 1, axis=1), pltpu.roll(t, 1, axis=1))
        return t * cos + partner * sin

    naq_ref[...] = (headnorm(p[:, _O_NAQ:_O_RQ], qg_ref[...]) * (NA_HEAD_DIM ** -0.5 * LOG2E)).astype(BF16)
    rq_ref[...] = rope(p[:, _O_RQ:_O_RG]).astype(BF16)
    g = p[:, _O_RG:_O_NAK]
    rg_ref[...] = (g * _sigmoid(g)).astype(BF16)
    nak_ref[...] = headnorm(p[:, _O_NAK:_O_NAV], kg_ref[...]).astype(BF16)
    nav_ref[...] = p[:, _O_NAV:_O_RK].astype(BF16)
    rk_ref[...] = (rope(p[:, _O_RK:_O_RV]) * (RET_DK ** -0.5)).astype(BF16)
    rv_ref[...] = p[:, _O_RV:_IN_WIDTH].astype(BF16)


def _inproj(x2d, mods, gain, w_bf, qg, kg, gmat, cos, sin, tm, tiles_per_mod, pos_tiles):
    t, d = x2d.shape
    widths = (NA_WIDTH, RET_QK_WIDTH, RET_V_WIDTH, NA_WIDTH, NA_WIDTH, RET_QK_WIDTH, RET_V_WIDTH)
    row = lambda i: (i, 0)
    const = lambda i: (0, 0)
    return pl.pallas_call(
        _inproj_kernel,
        grid=(t // tm,),
        in_specs=[pl.BlockSpec((tm, d), row),
                  pl.BlockSpec((1, 6, d), lambda i: (i // tiles_per_mod, 0, 0)),
                  pl.BlockSpec((1, d), const),
                  pl.BlockSpec((d, _IN_WIDTH), const),
                  pl.BlockSpec((1, NA_WIDTH), const),
                  pl.BlockSpec((1, NA_WIDTH), const),
                  pl.BlockSpec((NA_WIDTH, NA_WIDTH), const),
                  pl.BlockSpec((tm, RET_QK_WIDTH), lambda i: (i % pos_tiles, 0)),
                  pl.BlockSpec((tm, RET_QK_WIDTH), lambda i: (i % pos_tiles, 0))],
        out_specs=[pl.BlockSpec((tm, w), row) for w in widths],
        out_shape=[jax.ShapeDtypeStruct((t, w), BF16) for w in widths],
        compiler_params=_cparams("arbitrary"),
        name="inproj",
    )(x2d, mods, gain, w_bf, qg, kg, gmat, cos, sin)


def _na_kernel(q_ref, k0, k1, k2, k3, v0, v1, v2, v3, kx_ref, vx_ref, bias_ref, o_ref):
    lane = lax.broadcasted_iota(jnp.int32, (1, LANES), 1)
    ks = (k0, k1, k2, k3)
    vs = (v0, v1, v2, v3)
    kt = k0.shape[0] // LANES
    for pp in range(q_ref.shape[1] // LANES):
        cols = slice(pp * LANES, (pp + 1) * LANES)
        q2 = q_ref[:, cols]
        outs = []
        for hh in range(2):
            head = 2 * pp + hh
            sel = (lane // NA_HEAD_DIM) == hh
            qm = jnp.where(sel, q2, jnp.zeros_like(q2))
            s = []
            for j in range(4):
                bias = jnp.concatenate([bias_ref[0, head, kt * j + jj] for jj in range(kt)], axis=1)
                s.append(_dot_nt(qm, ks[j][:, cols]) + bias.astype(F32))
            s.append(_dot_nt(qm, kx_ref[:, cols]))
            m = s[0]
            for t in s[1:-1]:
                m = jnp.maximum(m, t)
            m = jnp.maximum(jnp.max(m, axis=-1, keepdims=True), jnp.max(s[-1], axis=-1, keepdims=True))
            den = jnp.zeros_like(m)
            acc = jnp.zeros((q2.shape[0], LANES), F32)
            vals = vs + (vx_ref,)
            for j in range(5):
                p = jnp.exp2(s[j] - m)
                den = den + jnp.sum(p, axis=-1, keepdims=True)
                acc = acc + _dot(p.astype(BF16), vals[j][:, cols])
            outs.append(acc / den)
        o_ref[:, cols] = jnp.where((lane // NA_HEAD_DIM) == 0, outs[0], outs[1]).astype(o_ref.dtype)


def _na_bias_table(rpb, rows):
    h = rpb.shape[0]
    nqb = rows // NA_QROWS
    row_sel = np.zeros((3, NA_QROWS, NA_KROWS, 2 * NA_WIN_ROWS - 1), np.float32)
    row_ok = np.zeros((3, NA_QROWS, NA_KROWS), bool)
    for v, qb in enumerate((0, min(1, nqb - 1), nqb - 1)):
        start = int(np.clip(NA_QROWS * qb - NA_WIN_ROWS // 2, 0, rows - NA_KROWS))
        for rq in range(NA_QROWS):
            r = NA_QROWS * qb + rq
            r0 = int(np.clip(r - NA_WIN_ROWS // 2, 0, rows - NA_WIN_ROWS))
            for rk in range(NA_KROWS):
                kr = start + rk
                if r0 <= kr < r0 + NA_WIN_ROWS:
                    row_ok[v, rq, rk] = True
                    row_sel[v, rq, rk, kr - r + NA_WIN_ROWS - 1] = 1.0
    col_sel = np.zeros((GRID_W, GRID_W, 2 * NA_WIN_COLS - 1), np.float32)
    col_ok = np.zeros((GRID_W, GRID_W), bool)
    for c in range(GRID_W):
        c0 = int(np.clip(c - NA_WIN_COLS // 2, 0, GRID_W - NA_WIN_COLS))
        for kc in range(c0, c0 + NA_WIN_COLS):
            col_ok[c, kc] = True
            col_sel[c, kc, kc - c + NA_WIN_COLS - 1] = 1.0
    g = LANES // GRID_W
    ncol = 2 * NA_WIN_COLS - 1
    col_sel2 = np.zeros((GRID_W, g * ncol, g * GRID_W), np.float32)
    for kk in range(g):
        col_sel2[:, kk * ncol:(kk + 1) * ncol, kk * GRID_W:(kk + 1) * GRID_W] = col_sel.transpose(0, 2, 1)
    t = jnp.einsum('vqkr,hrc->vhqkc', jnp.asarray(row_sel), rpb.astype(F32), precision=HIGHEST)
    t = t.reshape(3, h, NA_QROWS, NA_KROWS // g, g * ncol)
    t = jnp.einsum('vhqpe,xez->vhpqxz', t, jnp.asarray(col_sel2), precision=HIGHEST)
    ok = (row_ok.reshape(3, NA_QROWS, NA_KROWS // g, g).transpose(0, 2, 1, 3)[:, :, :, None, :, None]
          & col_ok[None, None, None, :, None, :])
    ok = ok.reshape(3, 1, NA_KROWS // g, NA_QROWS, GRID_W, LANES)
    t = jnp.where(jnp.asarray(ok), t * LOG2E, NEG_BIG)
    return t.reshape(3, h, NA_KROWS // g, NA_QROWS * GRID_W, LANES).astype(BF16)


def _na_attention(naq, nak, nav, cx_nak, cx_nav, bias, batch, n, ctx_len):
    rows = n // GRID_W
    qn = NA_QROWS * GRID_W
    kw = NA_KROWS * GRID_W // 4
    nqb = rows // NA_QROWS
    kblocks = n // kw
    width = NA_PAIRS_PER_STEP * LANES
    heads_per_step = width // NA_HEAD_DIM

    def kmap(j):
        def f(hp, qb, b):
            start = jnp.clip(2 * qb - 1, 0, kblocks - 4)
            return (b * kblocks + start + j, hp)
        return f

    def bias_map(hp, qb, b):
        v = jnp.where(qb == 0, 0, jnp.where(qb == nqb - 1, 2, 1))
        return (v, hp, 0, 0, 0)

    kv_specs = [pl.BlockSpec((kw, width), kmap(j)) for j in range(4)]
    return pl.pallas_call(
        _na_kernel,
        grid=(NA_WIDTH // width, nqb, batch),
        in_specs=[pl.BlockSpec((qn, width), lambda hp, qb, b: (b * nqb + qb, hp))]
                 + kv_specs + kv_specs
                 + [pl.BlockSpec((ctx_len, width), lambda hp, qb, b: (b, hp)),
                    pl.BlockSpec((ctx_len, width), lambda hp, qb, b: (b, hp)),
                    pl.BlockSpec((1, heads_per_step, 4 * kw // LANES, qn, LANES), bias_map)],
        out_specs=pl.BlockSpec((qn, width), lambda hp, qb, b: (b * nqb + qb, hp)),
        out_shape=jax.ShapeDtypeStruct((batch * n, NA_WIDTH), BF16),
        compiler_params=_cparams("arbitrary", "arbitrary", "arbitrary"),
        name="na_attention",
    )(naq, nak, nak, nak, nak, nav, nav, nav, nav, cx_nak, cx_nav, bias)


def _ret_kernel(reverse, lg_ref, q_ref, k_ref, v_ref, kx_ref, vx_ref, *rest):
    if reverse:
        of_ref, gate_ref, o_ref, s_ref, d_ref, xi_ref, zeta_ref = rest
    else:
        o_ref, s_ref, d_ref, xi_ref, zeta_ref = rest
    c = q_ref.shape[0]
    first = (pl.program_id(0) == 0) & (pl.program_id(1) == 0)
    lane = lax.broadcasted_iota(jnp.int32, (1, LANES), 1)

    @pl.when(first)
    def _():
        i = lax.broadcasted_iota(jnp.int32, (c, c), 0)
        j = lax.broadcasted_iota(jnp.int32, (c, c), 1)
        diff = (j - i) if reverse else (i - j)
        difff = jnp.maximum(diff, 0).astype(F32)
        r = lax.broadcasted_iota(jnp.int32, (c, LANES), 0).astype(F32)
        for h in range(RET_HEADS):
            lg = lg_ref[h]
            d_ref[h] = jnp.where(diff >= 0, jnp.exp(difff * lg), 0.0)
            xi_ref[h] = jnp.exp(((c - r) if reverse else (r + 1.0)) * lg)
        for p in range(RET_HEADS // 2):
            lgp = jnp.where(lane < RET_DK, lg_ref[2 * p], lg_ref[2 * p + 1])
            zeta_ref[p] = jnp.exp((r if reverse else (c - 1.0 - r)) * lgp)

    @pl.when(pl.program_id(1) == 0)
    def _():
        l = kx_ref.shape[0]
        r = lax.broadcasted_iota(jnp.int32, (l, LANES), 0).astype(F32)
        for p in range(RET_HEADS // 2):
            lgp = jnp.where(lane < RET_DK, lg_ref[2 * p], lg_ref[2 * p + 1])
            zc = jnp.exp((r if reverse else (l - 1.0 - r)) * lgp)
            kz = kx_ref[:, p * LANES:(p + 1) * LANES].astype(F32) * zc
            for hh in range(2):
                h = 2 * p + hh
                km = jnp.where((lane // RET_DK) == hh, kz, 0.0).astype(BF16)
                s_ref[h] = _dot_tn(km, vx_ref[:, h * RET_DV:(h + 1) * RET_DV])

    outs = []
    for p in range(RET_HEADS // 2):
        q2 = q_ref[:, p * LANES:(p + 1) * LANES]
        k2 = k_ref[:, p * LANES:(p + 1) * LANES]
        kz = k2.astype(F32) * zeta_ref[p]
        for hh in range(2):
            h = 2 * p + hh
            sel = (lane // RET_DK) == hh
            qm = jnp.where(sel, q2, jnp.zeros_like(q2))
            km = jnp.where(sel, k2, jnp.zeros_like(k2))
            vh = v_ref[:, h * RET_DV:(h + 1) * RET_DV]
            inner = (_dot_nt(qm, km) * d_ref[h]).astype(BF16)
            state = s_ref[h]
            o = _dot(inner, vh) + _dot(qm, state.astype(BF16)) * xi_ref[h]
            g_chunk = jnp.exp(jnp.full((1, LANES), c, F32) * lg_ref[h])
            s_ref[h] = state * g_chunk + _dot_tn(jnp.where(sel, kz, 0.0).astype(BF16), vh)
            outs.append(o)
    if reverse:
        for h in range(RET_HEADS):
            tot = outs[h] + of_ref[:, h * RET_DV:(h + 1) * RET_DV]
            ms = jnp.mean(tot * tot, axis=-1, keepdims=True)
            gate = gate_ref[:, h * RET_DV:(h + 1) * RET_DV].astype(F32)
            o_ref[:, h * RET_DV:(h + 1) * RET_DV] = (tot * lax.rsqrt(ms + NORM_EPS) * gate).astype(o_ref.dtype)
    else:
        for h in range(RET_HEADS):
            o_ref[:, h * RET_DV:(h + 1) * RET_DV] = outs[h]


def _retention_pass(reverse, log_g, rq, rk, rv, cx_rk, cx_rv, batch, n, ctx_len, o_fwd=None, gate=None):
    c = min(RET_CHUNK, n)
    nc = n // c
    if reverse:
        seq = lambda b, i: (b * nc + nc - 1 - i, 0)
    else:
        seq = lambda b, i: (b * nc + i, 0)
    ctx = lambda b, i: (b, 0)
    in_specs = [pl.BlockSpec(memory_space=pltpu.SMEM),
                pl.BlockSpec((c, RET_QK_WIDTH), seq),
                pl.BlockSpec((c, RET_QK_WIDTH), seq),
                pl.BlockSpec((c, RET_V_WIDTH), seq),
                pl.BlockSpec((ctx_len, RET_QK_WIDTH), ctx),
                pl.BlockSpec((ctx_len, RET_V_WIDTH), ctx)]
    args = [log_g, rq, rk, rv, cx_rk, cx_rv]
    if reverse:
        in_specs += [pl.BlockSpec((c, RET_V_WIDTH), seq), pl.BlockSpec((c, RET_V_WIDTH), seq)]
        args += [o_fwd, gate]
    return pl.pallas_call(
        functools.partial(_ret_kernel, reverse),
        grid=(batch, nc),
        in_specs=in_specs,
        out_specs=pl.BlockSpec((c, RET_V_WIDTH), seq),
        out_shape=jax.ShapeDtypeStruct((batch * n, RET_V_WIDTH), BF16 if reverse else F32),
        scratch_shapes=[pltpu.VMEM((RET_HEADS, LANES, RET_DV), F32),
                        pltpu.VMEM((RET_HEADS, c, c), F32),
                        pltpu.VMEM((RET_HEADS, c, RET_DV), F32),
                        pltpu.VMEM((RET_HEADS // 2, c, LANES), F32)],
        compiler_params=_cparams("arbitrary", "arbitrary"),
        name="retention_bwd" if reverse else "retention_fwd",
    )(*args)


def _outproj_kernel(x_ref, ona_ref, oret_ref, w_ref, mod_ref, gain_ref, x1_ref, h_ref):
    mix = _dot(ona_ref[...], w_ref[0:NA_WIDTH, :]) + _dot(oret_ref[...], w_ref[NA_WIDTH:, :])
    x1 = x_ref[...] + mod_ref[0, 2:3, :] * mix
    x1_ref[...] = x1
    h_ref[...] = _rms_mod(x1, gain_ref[...], mod_ref[0, 3:4, :], mod_ref[0, 4:5, :]).astype(BF16)


def _outproj(x2d, ona, oret, w_out_bf, mods, gain, tm, tiles_per_batch):
    t, d = x2d.shape
    row = lambda i: (i, 0)
    const = lambda i: (0, 0)
    return pl.pallas_call(
        _outproj_kernel,
        grid=(t // tm,),
        in_specs=[pl.BlockSpec((tm, d), row),
                  pl.BlockSpec((tm, NA_WIDTH), row),
                  pl.BlockSpec((tm, RET_V_WIDTH), row),
                  pl.BlockSpec((NA_WIDTH + RET_V_WIDTH, d), const),
                  pl.BlockSpec((1, 6, d), lambda i: (i // tiles_per_batch, 0, 0)),
                  pl.BlockSpec((1, d), const)],
        out_specs=[pl.BlockSpec((tm, d), row), pl.BlockSpec((tm, d), row)],
        out_shape=[jax.ShapeDtypeStruct((t, d), F32), jax.ShapeDtypeStruct((t, d), BF16)],
        compiler_params=_cparams("arbitrary"),
        name="outproj",
    )(x2d, ona, oret, w_out_bf, mods, gain)


def _swiglu_kernel(h_ref, x1_ref, wg_ref, wu_ref, wd_ref, mod_ref, o_ref, acc_ref):
    j = pl.program_id(1)

    @pl.when(j == 0)
    def _():
        acc_ref[...] = jnp.zeros_like(acc_ref)

    h = h_ref[...]
    g = _dot(h, wg_ref[...])
    u = _dot(h, wu_ref[...])
    a = (g * _sigmoid(g) * u).astype(BF16)
    acc_ref[...] += _dot(a, wd_ref[...])

    @pl.when(j == pl.num_programs(1) - 1)
    def _():
        o_ref[...] = x1_ref[...] + mod_ref[0, 5:6, :] * acc_ref[...]


def _swiglu(h_bf, x1, w_gate_up_bf, w_down_bf, mods, tm, tiles_per_batch):
    t, d = x1.shape
    f = w_down_bf.shape[0]
    nf = 2 if (f // 2) % LANES == 0 else 1
    tf = f // nf
    return pl.pallas_call(
        _swiglu_kernel,
        grid=(t // tm, nf),
        in_specs=[pl.BlockSpec((tm, d), lambda i, j: (i, 0)),
                  pl.BlockSpec((tm, d), lambda i, j: (i, 0)),
                  pl.BlockSpec((d, tf), lambda i, j: (0, j)),
                  pl.BlockSpec((d, tf), lambda i, j: (0, nf + j)),
                  pl.BlockSpec((tf, d), lambda i, j: (j, 0)),
                  pl.BlockSpec((1, 6, d), lambda i, j: (i // tiles_per_batch, 0, 0))],
        out_specs=pl.BlockSpec((tm, d), lambda i, j: (i, 0)),
        out_shape=jax.ShapeDtypeStruct((t, d), F32),
        scratch_shapes=[pltpu.VMEM((tm, d), F32)],
        compiler_params=_cparams("arbitrary", "arbitrary"),
        name="swiglu",
    )(h_bf, x1, w_gate_up_bf, w_gate_up_bf, w_down_bf, mods)


def _dft_tables(n, d):
    n2 = FFT_N2
    n1 = n // n2
    s = SUBLANES
    eye = np.eye(s)
    ang1 = 2.0 * np.pi * (np.outer(np.arange(n1), np.arange(n1)) % n1) / n1
    f1 = np.stack([np.cos(ang1), -np.sin(ang1)]) / np.sqrt(n)
    f1k = np.einsum('ckn,jl->ckjnl', f1, eye).reshape(2 * n1 * s, n1 * s)
    angt = 2.0 * np.pi * (np.outer(np.arange(n1), np.arange(n2)) % n) / n
    tw = np.stack([np.cos(angt), -np.sin(angt)])
    tw = tw.reshape(2, n1, n2 // s, s).transpose(0, 2, 1, 3).reshape(2, n2 // s, n1 * s, 1)
    tw = np.broadcast_to(tw, (2, n2 // s, n1 * s, LANES))
    ang2 = 2.0 * np.pi * (np.outer(np.arange(n2), np.arange(n2)) % n2) / n2
    cr, ci = np.cos(ang2), -np.sin(ang2)
    m2 = np.array([[cr, -ci], [ci, cr]])
    m2k = np.einsum('abkn,jl->akjbln', m2, eye).reshape(2 * n2 * s, 2 * s * n2)
    gw = d // FOURIER_GROUPS
    angc = 2.0 * np.pi * (np.outer(np.arange(gw), np.arange(gw)) % gw) / gw
    cs = np.concatenate([np.cos(angc), np.sin(angc)], axis=0) / np.sqrt(gw)
    return (f1k.astype(np.float32), tw.astype(np.float32), m2k.astype(np.float32),
            cs.astype(np.float32))


def _fold_kernel(cs_ref, w_ref, o_ref):
    gw = w_ref.shape[0]
    o_ref[0] = jnp.dot(cs_ref[0:gw, :], w_ref[...], precision=HIGHEST, preferred_element_type=F32)
    o_ref[1] = jnp.dot(cs_ref[gw:, :], w_ref[...], precision=HIGHEST, preferred_element_type=F32)


def _fold_channel_dft(cs, w_fourier):
    d = w_fourier.shape[0]
    gw = d // FOURIER_GROUPS
    return pl.pallas_call(
        _fold_kernel,
        grid=(FOURIER_GROUPS,),
        in_specs=[pl.BlockSpec((2 * gw, gw), lambda g: (0, 0)),
                  pl.BlockSpec((gw, d), lambda g: (g, 0))],
        out_specs=pl.BlockSpec((2, gw, d), lambda g: (0, g, 0)),
        out_shape=jax.ShapeDtypeStruct((2, d, d), F32),
        compiler_params=_cparams("arbitrary"),
        name="fold_channel_dft",
    )(cs, w_fourier)


def _fft1_kernel(x_ref, mod_ref, gain_ref, f_ref, tw_ref, o_ref):
    _, n1, s, d = x_ref.shape
    x = x_ref[...].reshape(n1 * s, d)
    h = _rms_mod(x, gain_ref[...], mod_ref[0, 0:1, :], mod_ref[0, 1:2, :]).astype(BF16)
    a = _dot(f_ref[...], h)
    ar = a[:n1 * s]
    ai = a[n1 * s:]
    reps = d // LANES
    twr = jnp.tile(tw_ref[0, 0], (1, reps))
    twi = jnp.tile(tw_ref[1, 0], (1, reps))
    o_ref[0, 0] = (ar * twr - ai * twi).reshape(n1, s, d)
    o_ref[0, 1] = (ar * twi + ai * twr).reshape(n1, s, d)


def _fft_stage1(x2, mods, gain, f1k_bf, tw, batch, n):
    d = x2.shape[-1]
    n2 = FFT_N2
    n1 = n // n2
    s = SUBLANES
    xv = x2.reshape(batch, n1, n2, d)
    return pl.pallas_call(
        _fft1_kernel,
        grid=(n2 // s, batch),
        in_specs=[pl.BlockSpec((1, n1, s, d), lambda g, b: (b, 0, g, 0)),
                  pl.BlockSpec((1, 6, d), lambda g, b: (b, 0, 0)),
                  pl.BlockSpec((1, d), lambda g, b: (0, 0)),
                  pl.BlockSpec((2 * n1 * s, n1 * s), lambda g, b: (0, 0)),
                  pl.BlockSpec((2, 1, n1 * s, LANES), lambda g, b: (0, g, 0, 0))],
        out_specs=pl.BlockSpec((1, 2, n1, s, d), lambda g, b: (b, 0, 0, g, 0)),
        out_shape=jax.ShapeDtypeStruct((batch, 2, n1, n2, d), F32),
        compiler_params=_cparams("arbitrary", "arbitrary"),
        name="fft_stage1",
    )(xv, mods, gain, f1k_bf, tw)


def _fft2_kernel(n_experts, b_ref, x_ref, m_ref, wf_ref, mod_ref, gain_ref, wr_ref,
                 x3_ref, h2_ref, route_ref):
    _, _, s, n2, d = b_ref.shape
    rows = n2 * s
    bmat = b_ref[...].reshape(2 * s * n2, d).astype(BF16)
    z = _dot(m_ref[...], bmat)
    y = _dot(z[:rows].astype(BF16), wf_ref[0]) + _dot(z[rows:].astype(BF16), wf_ref[1])
    x3 = x_ref[...].reshape(rows, d) + mod_ref[0, 2:3, :] * y
    x3_ref[...] = x3.reshape(x3_ref.shape)
    h2 = _rms_mod(x3, gain_ref[...], mod_ref[0, 3:4, :], mod_ref[0, 4:5, :])
    h2_ref[...] = h2.reshape(rows, d // LANES, LANES).reshape(h2_ref.shape)
    h_hi = h2.astype(BF16)
    h_lo = (h2 - h_hi.astype(F32)).astype(BF16)
    wr = wr_ref[...]
    w_hi = wr.astype(BF16)
    w_lo = (wr - w_hi.astype(F32)).astype(BF16)
    logits = _dot(h_hi, w_hi) + (_dot(h_lo, w_hi) + _dot(h_hi, w_lo))
    lane = lax.broadcasted_iota(jnp.int32, logits.shape, 1).astype(F32)
    lg = jnp.where(lane < n_experts, logits, -jnp.inf)
    m1 = jnp.max(lg, axis=-1, keepdims=True)
    i1 = jnp.min(jnp.where(lg == m1, lane, float(LANES)), axis=-1, keepdims=True)
    lg2 = jnp.where(lane == i1, -jnp.inf, lg)
    m2 = jnp.max(lg2, axis=-1, keepdims=True)
    i2 = jnp.min(jnp.where(lg2 == m2, lane, float(LANES)), axis=-1, keepdims=True)
    e = jnp.exp(m2 - m1)
    g1 = 1.0 / (1.0 + e)
    g2 = e / (1.0 + e)
    route = jnp.where(lane == 0, g1, jnp.where(lane == 1, g2, jnp.where(lane == 2, i1,
                                                                       jnp.where(lane == 3, i2, 0.0))))
    route_ref[...] = route.reshape(route_ref.shape)


def _fft_stage2(bmid, x2, m2k_bf, wf_bf, mods, gain, wr_pad, n_experts, batch, n):
    d = x2.shape[-1]
    n2 = FFT_N2
    n1 = n // n2
    s = SUBLANES
    xv = x2.reshape(batch, n2, n1, d)
    pos = lambda g, b: (b, 0, g, 0)
    const2 = lambda g, b: (0, 0)
    x3, h2, route = pl.pallas_call(
        functools.partial(_fft2_kernel, n_experts),
        grid=(n1 // s, batch),
        in_specs=[pl.BlockSpec((1, 2, s, n2, d), lambda g, b: (b, 0, g, 0, 0)),
                  pl.BlockSpec((1, n2, s, d), pos),
                  pl.BlockSpec((2 * n2 * s, 2 * n2 * s), const2),
                  pl.BlockSpec((2, d, d), lambda g, b: (0, 0, 0)),
                  pl.BlockSpec((1, 6, d), lambda g, b: (b, 0, 0)),
                  pl.BlockSpec((1, d), const2),
                  pl.BlockSpec((d, LANES), const2)],
        out_specs=[pl.BlockSpec((1, n2, s, d), pos),
                   pl.BlockSpec((1, n2, s, d // LANES, LANES), lambda g, b: (b, 0, g, 0, 0)),
                   pl.BlockSpec((1, n2, s, LANES), pos)],
        out_shape=[jax.ShapeDtypeStruct((batch, n2, n1, d), F32),
                   jax.ShapeDtypeStruct((batch, n2, n1, d // LANES, LANES), F32),
                   jax.ShapeDtypeStruct((batch, n2, n1, LANES), F32)],
        compiler_params=_cparams("arbitrary", "arbitrary"),
        name="fft_stage2",
    )(bmid, xv, m2k_bf, wf_bf, mods, gain, wr_pad)
    t = batch * n
    return x3.reshape(t, d), h2.reshape(t, d // LANES, LANES), route.reshape(t, LANES)


def _dispatch_kernel(tm, slot_ref, h_ref, dst_in_ref, dst_ref, sem):
    del dst_in_ref
    base = pl.program_id(0) * tm * TOP_K

    def body(r, carry):
        for k in range(TOP_K):
            s = slot_ref[base + TOP_K * r + k]
            pltpu.make_async_copy(h_ref.at[pl.ds(r, 1)], dst_ref.at[pl.ds(s, 1)], sem).start(priority=k % 2)
        return carry

    lax.fori_loop(0, tm, body, 0, unroll=8)

    def drain(r, carry):
        pltpu.make_async_copy(h_ref.at[pl.ds(0, 1)], dst_ref.at[pl.ds(0, 1)], sem).wait()
        return carry

    lax.fori_loop(0, tm * TOP_K, drain, 0, unroll=8)


def _dispatch(h2, slot, n_slots, tm):
    t = h2.shape[0]
    tile = h2.shape[1:]
    return pl.pallas_call(
        functools.partial(_dispatch_kernel, tm),
        grid_spec=pltpu.PrefetchScalarGridSpec(
            num_scalar_prefetch=1,
            grid=(t // tm,),
            in_specs=[pl.BlockSpec((tm,) + tile, lambda i, sl: (i, 0, 0)),
                      pl.BlockSpec(memory_space=pl.ANY)],
            out_specs=pl.BlockSpec(memory_space=pl.ANY),
            scratch_shapes=[pltpu.SemaphoreType.DMA(())]),
        out_shape=jax.ShapeDtypeStruct((n_slots,) + tile, h2.dtype),
        input_output_aliases={2: 0},
        compiler_params=pltpu.CompilerParams(dimension_semantics=("arbitrary",), has_side_effects=True,
                                             vmem_limit_bytes=VMEM_LIMIT),
        name="dispatch",
    )(slot, h2, jnp.zeros((n_slots,) + tile, h2.dtype))


def _expert_kernel(bexp_ref, nused_ref, x_ref, wg_ref, wu_ref, wd_ref, o_ref, xb_ref, acc_ref):
    i = pl.program_id(0)
    j = pl.program_id(1)
    last = pl.num_programs(1) - 1
    used = i < nused_ref[0]

    @pl.when(used)
    def _():
        @pl.when(j == 0)
        def _():
            xb_ref[...] = x_ref[...].reshape(xb_ref.shape).astype(BF16)
            acc_ref[...] = jnp.zeros_like(acc_ref)

        xb = xb_ref[...]
        g = _dot(xb, wg_ref[0])
        u = _dot(xb, wu_ref[0])
        a = (g * _sigmoid(g) * u).astype(BF16)
        acc_ref[...] += _dot(a, wd_ref[0])

        @pl.when(j == last)
        def _():
            o_ref[...] = acc_ref[...].reshape(o_ref.shape)

    @pl.when(jnp.logical_not(used) & (j == last))
    def _():
        o_ref[...] = jnp.zeros_like(o_ref)


def _expert_ffn(xs, w_gu_bf, w_dn_bf, block_expert, n_used):
    n_slots, sub, _ = xs.shape
    e, f, d = w_dn_bf.shape
    nb = n_slots // MOE_BLOCK
    nf = 4 if (f // 4) % LANES == 0 else 1
    tf = f // nf

    def xmap(i, j, be, nu):
        return (jnp.minimum(i, nu[0] - 1), 0, 0)

    def jeff(i, j, nu):
        return jnp.where(i < nu[0], j, nf - 1)

    return pl.pallas_call(
        _expert_kernel,
        grid_spec=pltpu.PrefetchScalarGridSpec(
            num_scalar_prefetch=2,
            grid=(nb, nf),
            in_specs=[pl.BlockSpec((MOE_BLOCK, sub, LANES), xmap),
                      pl.BlockSpec((1, d, tf), lambda i, j, be, nu: (be[i], 0, jeff(i, j, nu))),
                      pl.BlockSpec((1, d, tf), lambda i, j, be, nu: (be[i], 0, nf + jeff(i, j, nu))),
                      pl.BlockSpec((1, tf, d), lambda i, j, be, nu: (be[i], jeff(i, j, nu), 0))],
            out_specs=pl.BlockSpec((MOE_BLOCK, sub, LANES), lambda i, j, be, nu: (i, 0, 0)),
            scratch_shapes=[pltpu.VMEM((MOE_BLOCK, d), BF16), pltpu.VMEM((MOE_BLOCK, d), F32)]),
        out_shape=jax.ShapeDtypeStruct((n_slots, sub, LANES), F32),
        compiler_params=_cparams("arbitrary", "arbitrary"),
        name="expert_ffn",
    )(block_expert, n_used, xs, w_gu_bf, w_gu_bf, w_dn_bf)


def _combine_kernel(tm, slot_ref, x_ref, route_ref, mod_ref, ys_ref, o_ref, ybuf, sem):
    i = pl.program_id(0)

    def issue(step, b):
        base = step * (tm * TOP_K)

        def body(r, carry):
            for k in range(TOP_K):
                s = slot_ref[base + TOP_K * r + k]
                pltpu.make_async_copy(ys_ref.at[pl.ds(s, 1)], ybuf.at[b, k, pl.ds(r, 1)],
                                      sem.at[b]).start(priority=k % 2)
            return carry

        lax.fori_loop(0, tm, body, 0, unroll=8)

    @pl.when(i == 0)
    def _():
        issue(0, 0)

    @pl.when(i + 1 < pl.num_programs(0))
    def _():
        issue(i + 1, (i + 1) % 2)

    b = i % 2

    def drain(r, carry):
        pltpu.make_async_copy(ys_ref.at[pl.ds(0, 1)], ybuf.at[b, 0, pl.ds(0, 1)], sem.at[b]).wait()
        return carry

    lax.fori_loop(0, tm * TOP_K, drain, 0, unroll=8)
    y0 = ybuf[b, 0].reshape(x_ref.shape)
    y1 = ybuf[b, 1].reshape(x_ref.shape)
    g0 = route_ref[:, 0:1]
    g1 = route_ref[:, 1:2]
    o_ref[...] = x_ref[...] + mod_ref[0, 5:6, :] * (g0 * y0 + g1 * y1)


def _combine(x3, ys, slot, route, mods, tm, tiles_per_batch):
    t, d = x3.shape
    tile = ys.shape[1:]
    row = lambda i, sl: (i, 0)
    return pl.pallas_call(
        functools.partial(_combine_kernel, tm),
        grid_spec=pltpu.PrefetchScalarGridSpec(
            num_scalar_prefetch=1,
            grid=(t // tm,),
            in_specs=[pl.BlockSpec((tm, d), row),
                      pl.BlockSpec((tm, LANES), row),
                      pl.BlockSpec((1, 6, d), lambda i, sl: (i // tiles_per_batch, 0, 0)),
                      pl.BlockSpec(memory_space=pl.ANY)],
            out_specs=pl.BlockSpec((tm, d), row),
            scratch_shapes=[pltpu.VMEM((2, TOP_K, tm) + tile, F32), pltpu.SemaphoreType.DMA((2,))]),
        out_shape=jax.ShapeDtypeStruct((t, d), F32),
        compiler_params=_cparams("arbitrary"),
        name="combine",
    )(slot, x3, route, mods, ys)


def _route_slots(route, n_experts):
    t = route.shape[0]
    expert = route[:, 2:2 + TOP_K].astype(jnp.int32).reshape(t * TOP_K)
    onehot = (expert[:, None] == jnp.arange(n_experts, dtype=jnp.int32)[None, :]).astype(jnp.int32)
    csum = jnp.cumsum(onehot, axis=0)
    counts = csum[-1]
    padded = (counts + MOE_BLOCK - 1) // MOE_BLOCK * MOE_BLOCK
    pad_end = jnp.cumsum(padded)
    pad_start = pad_end - padded
    slot = jnp.sum(onehot * (pad_start[None, :] + csum - 1), axis=1).astype(jnp.int32)
    n_slots = t * TOP_K + n_experts * MOE_BLOCK
    nb = n_slots // MOE_BLOCK
    block_expert = jnp.minimum(
        jnp.searchsorted(pad_end, jnp.arange(nb, dtype=jnp.int32) * MOE_BLOCK, side='right'),
        n_experts - 1).astype(jnp.int32)
    n_used = (pad_end[-1:] // MOE_BLOCK).astype(jnp.int32)
    return slot, block_expert, n_used, n_slots


def _rope_tables(n):
    pos = np.arange(n)
    row = (pos // GRID_W).astype(np.float32)
    col = (pos % GRID_W).astype(np.float32)
    n_freq = RET_DK // 4
    inv_freq = (np.float32(ROPE_BASE) ** (-np.arange(n_freq, dtype=np.float32) / n_freq)).astype(np.float32)
    ang = np.concatenate([row[:, None] * inv_freq, col[:, None] * inv_freq], axis=-1)
    cos = np.cos(ang.astype(np.float64))
    sin = np.sin(ang.astype(np.float64))
    cos_h = np.repeat(cos, 2, axis=-1)
    sin_h = np.stack([-sin, sin], axis=-1).reshape(n, RET_DK)
    return (np.tile(cos_h, (1, RET_HEADS)).astype(np.float32),
            np.tile(sin_h, (1, RET_HEADS)).astype(np.float32))


def kernel(x, c, ctx, c_ctx, l0_norm_mix, l0_norm_ffn, l0_w_ada, l0_b_ada, l0_w_in, l0_q_gain, l0_k_gain, l0_rpb, l0_ret_decay_fwd, l0_ret_decay_bwd, l0_w_out, l0_w_gate_up, l0_w_down, l1_norm_mix, l1_norm_ffn, l1_w_ada, l1_b_ada, l1_w_fourier, l1_w_router, l1_w_exp_gate_up, l1_w_exp_down):
    batch, n, d = x.shape
    ctx_len = ctx.shape[1]
    t = batch * n
    rows = n // GRID_W
    n_experts = l1_w_router.shape[1]
    assert n % (NA_QROWS * GRID_W) == 0 and rows >= NA_KROWS and n % FFT_N2 == 0
    assert l0_w_in.shape[1] == _IN_WIDTH and d % LANES == 0

    x2d = x.reshape(t, d)
    ctx2d = ctx.reshape(batch * ctx_len, d)

    ada0 = _ada(jnp.concatenate([c, c_ctx[None, :]], axis=0), l0_w_ada, l0_b_ada)
    mods0 = ada0[:batch].reshape(batch, 6, d)
    mods_ctx = ada0[batch:batch + 1].reshape(1, 6, d)
    mods1 = _ada(c, l1_w_ada, l1_b_ada)[:batch].reshape(batch, 6, d)

    w_in_bf = l0_w_in.astype(BF16)
    qg = jnp.tile(l0_q_gain.astype(F32), NA_HEADS).reshape(1, NA_WIDTH)
    kg = jnp.tile(l0_k_gain.astype(F32), NA_HEADS).reshape(1, NA_WIDTH)
    gmat = jnp.asarray(np.kron(np.eye(NA_HEADS), np.full((NA_HEAD_DIM, NA_HEAD_DIM), 1.0 / NA_HEAD_DIM)),
                       BF16)
    cos_np, sin_np = _rope_tables(n)
    gain_mix0 = l0_norm_mix.reshape(1, d)
    tm = min(INPROJ_ROWS, n)
    naq, rq, rg, nak, nav, rk, rv = _inproj(
        x2d, mods0, gain_mix0, w_in_bf, qg, kg, gmat, jnp.asarray(cos_np), jnp.asarray(sin_np),
        tm, n // tm, n // tm)
    ctx_rows = batch * ctx_len
    _, _, _, cx_nak, cx_nav, cx_rk, cx_rv = _inproj(
        ctx2d, mods_ctx, gain_mix0, w_in_bf, qg, kg, gmat,
        jnp.ones((ctx_len, RET_QK_WIDTH), F32), jnp.zeros((ctx_len, RET_QK_WIDTH), F32),
        ctx_len, ctx_rows // ctx_len, 1)

    bias = _na_bias_table(l0_rpb, rows)
    o_na = _na_attention(naq, nak, nav, cx_nak, cx_nav, bias, batch, n, ctx_len)

    lg_f = jax.nn.log_sigmoid(l0_ret_decay_fwd.astype(F32))
    lg_b = jax.nn.log_sigmoid(l0_ret_decay_bwd.astype(F32))
    o_f = _retention_pass(False, lg_f, rq, rk, rv, cx_rk, cx_rv, batch, n, ctx_len)
    o_ret = _retention_pass(True, lg_b, rq, rk, rv, cx_rk, cx_rv, batch, n, ctx_len, o_fwd=o_f, gate=rg)

    x1, h_ffn = _outproj(x2d, o_na, o_ret, l0_w_out.astype(BF16), mods0, l0_norm_ffn.reshape(1, d),
                         tm, n // tm)
    tf_rows = min(FFN_ROWS, n)
    x2 = _swiglu(h_ffn, x1, l0_w_gate_up.astype(BF16), l0_w_down.astype(BF16), mods0,
                 tf_rows, n // tf_rows)

    f1k, tw, m2k, cs = _dft_tables(n, d)
    wf = _fold_channel_dft(jnp.asarray(cs), l1_w_fourier).astype(BF16)
    bmid = _fft_stage1(x2, mods1, l1_norm_mix.reshape(1, d), jnp.asarray(f1k, BF16), jnp.asarray(tw),
                       batch, n)
    wr_pad = jnp.zeros((d, LANES), F32).at[:, :n_experts].set(l1_w_router)
    x3, h2, route = _fft_stage2(bmid, x2, jnp.asarray(m2k, BF16), wf, mods1, l1_norm_ffn.reshape(1, d),
                                wr_pad, n_experts, batch, n)

    slot, block_expert, n_used, n_slots = _route_slots(route, n_experts)
    xs = _dispatch(h2, slot, n_slots, tm)
    ys = _expert_ffn(xs, l1_w_exp_gate_up.astype(BF16), l1_w_exp_down.astype(BF16), block_expert, n_used)
    out = _combine(x3, ys, slot, route, mods1, tm, n // tm)
    return out.reshape(batch, n, d)
```

```python
import functools

import numpy as np
import jax
import jax.numpy as jnp
from jax import lax
from jax.experimental import pallas as pl
from jax.experimental.pallas import tpu as pltpu

F32 = jnp.float32
BF16 = jnp.bfloat16
HIGHEST = lax.Precision.HIGHEST

GRID_W = 64
NA_HEADS = 8
NA_HEAD_DIM = 64
NA_WIN_ROWS = 8
NA_WIN_COLS = 16
RET_HEADS = 4
RET_DK = 64
RET_DV = 128
ROPE_BASE = 10000.0
FOURIER_GROUPS = 8
TOP_K = 2
NORM_EPS = 1e-6

NA_WIDTH = NA_HEADS * NA_HEAD_DIM
RET_QK_WIDTH = RET_HEADS * RET_DK
RET_V_WIDTH = RET_HEADS * RET_DV

LANES = 128
SUBLANES = 8
VMEM_LIMIT = 48 * 1024 * 1024

NEG_BIG = -1e30
LOG2E = 1.4426950408889634

INPROJ_ROWS = 512
NA_QROWS = 8
NA_KROWS = 16
NA_PAIRS_PER_STEP = 4
RET_CHUNK = 512
FFN_ROWS = 1024
FFT_N2 = 64
MOE_BLOCK = 1024


def _cparams(*sem):
    return pltpu.CompilerParams(dimension_semantics=sem, vmem_limit_bytes=VMEM_LIMIT)


def _sigmoid(x):
    return 0.5 * jnp.tanh(0.5 * x) + 0.5


def _dot(a, b):
    return jnp.dot(a, b, preferred_element_type=F32)


def _dot_nt(a, b):
    return lax.dot_general(a, b, (((1,), (1,)), ((), ())), preferred_element_type=F32)


def _dot_tn(a, b):
    return lax.dot_general(a, b, (((0,), (0,)), ((), ())), preferred_element_type=F32)


def _rms_mod(x, gain, shift, scale):
    ms = jnp.mean(x * x, axis=-1, keepdims=True)
    return (x * lax.rsqrt(ms + NORM_EPS) * gain) * (1.0 + scale) + shift


def _ada_kernel(c_ref, w_ref, b_ref, o_ref):
    c = c_ref[...]
    s = c * _sigmoid(c)
    o_ref[...] = jnp.dot(s, w_ref[...], precision=HIGHEST, preferred_element_type=F32) + b_ref[...]


def _ada(cond, w_ada, b_ada):
    d, n = w_ada.shape
    cond8 = jnp.zeros((SUBLANES, d), F32).at[:cond.shape[0]].set(cond)
    tn = 1024
    return pl.pallas_call(
        _ada_kernel,
        grid=(n // tn,),
        in_specs=[pl.BlockSpec((SUBLANES, d), lambda j: (0, 0)),
                  pl.BlockSpec((d, tn), lambda j: (0, j)),
                  pl.BlockSpec((1, tn), lambda j: (0, j))],
        out_specs=pl.BlockSpec((SUBLANES, tn), lambda j: (0, j)),
        out_shape=jax.ShapeDtypeStruct((SUBLANES, n), F32),
        compiler_params=_cparams("arbitrary"),
        name="ada",
    )(cond8, w_ada, b_ada.reshape(1, n))


_O_NAQ = 0
_O_RQ = _O_NAQ + NA_WIDTH
_O_RG = _O_RQ + RET_QK_WIDTH
_O_NAK = _O_RG + RET_V_WIDTH
_O_NAV = _O_NAK + NA_WIDTH
_O_RK = _O_NAV + NA_WIDTH
_O_RV = _O_RK + RET_QK_WIDTH
_IN_WIDTH = _O_RV + RET_V_WIDTH


def _inproj_kernel(x_ref, mod_ref, gain_ref, w_ref, qg_ref, kg_ref, gmat_ref, cos_ref, sin_ref,
                   naq_ref, rq_ref, rg_ref, nak_ref, nav_ref, rk_ref, rv_ref):
    h = _rms_mod(x_ref[...], gain_ref[...], mod_ref[0, 0:1, :], mod_ref[0, 1:2, :]).astype(BF16)
    p = _dot(h, w_ref[...])
    gmat = gmat_ref[...]

    def headnorm(t, g):
        ms = _dot((t * t).astype(BF16), gmat)
        return t * lax.rsqrt(ms + NORM_EPS) * g

    cos = cos_ref[...]
    sin = sin_ref[...]
    lane = lax.broadcasted_iota(jnp.int32, (1, RET_QK_WIDTH), 1)
    even = (lane % 2) == 0

    def rope(t):
        partner = jnp.where(even, pltpu.roll(t, RET_QK_WIDTH - 1, axis=1), pltpu.roll(t, 1, axis=1))
        return t * cos + partner * sin

    naq_ref[...] = (headnorm(p[:, _O_NAQ:_O_RQ], qg_ref[...]) * (NA_HEAD_DIM ** -0.5 * LOG2E)).astype(BF16)
    rq_ref[...] = rope(p[:, _O_RQ:_O_RG]).astype(BF16)
    g = p[:, _O_RG:_O_NAK]
    rg_ref[...] = (g * _sigmoid(g)).astype(BF16)
    nak_ref[...] = headnorm(p[:, _O_NAK:_O_NAV], kg_ref[...]).astype(BF16)
    nav_ref[...] = p[:, _O_NAV:_O_RK].astype(BF16)
    rk_ref[...] = (rope(p[:, _O_RK:_O_RV]) * (RET_DK ** -0.5)).astype(BF16)
    rv_ref[...] = p[:, _O_RV:_IN_WIDTH].astype(BF16)


def _inproj(x2d, mods, gain, w_bf, qg, kg, gmat, cos, sin, tm, tiles_per_mod, pos_tiles):
    t, d = x2d.shape
    widths = (NA_WIDTH, RET_QK_WIDTH, RET_V_WIDTH, NA_WIDTH, NA_WIDTH, RET_QK_WIDTH, RET_V_WIDTH)
    row = lambda i: (i, 0)
    const = lambda i: (0, 0)
    return pl.pallas_call(
        _inproj_kernel,
        grid=(t // tm,),
        in_specs=[pl.BlockSpec((tm, d), row),
                  pl.BlockSpec((1, 6, d), lambda i: (i // tiles_per_mod, 0, 0)),
                  pl.BlockSpec((1, d), const),
                  pl.BlockSpec((d, _IN_WIDTH), const),
                  pl.BlockSpec((1, NA_WIDTH), const),
                  pl.BlockSpec((1, NA_WIDTH), const),
                  pl.BlockSpec((NA_WIDTH, NA_WIDTH), const),
                  pl.BlockSpec((tm, RET_QK_WIDTH), lambda i: (i % pos_tiles, 0)),
                  pl.BlockSpec((tm, RET_QK_WIDTH), lambda i: (i % pos_tiles, 0))],
        out_specs=[pl.BlockSpec((tm, w), row) for w in widths],
        out_shape=[jax.ShapeDtypeStruct((t, w), BF16) for w in widths],
        compiler_params=_cparams("arbitrary"),
        name="inproj",
    )(x2d, mods, gain, w_bf, qg, kg, gmat, cos, sin)


def _na_kernel(q_ref, k0, k1, k2, k3, v0, v1, v2, v3, kx_ref, vx_ref, bias_ref, o_ref):
    lane = lax.broadcasted_iota(jnp.int32, (1, LANES), 1)
    ks = (k0, k1, k2, k3)
    vs = (v0, v1, v2, v3)
    kt = k0.shape[0] // LANES
    for pp in range(q_ref.shape[1] // LANES):
        cols = slice(pp * LANES, (pp + 1) * LANES)
        q2 = q_ref[:, cols]
        outs = []
        for hh in range(2):
            head = 2 * pp + hh
            sel = (lane // NA_HEAD_DIM) == hh
            qm = jnp.where(sel, q2, jnp.zeros_like(q2))
            s = []
            for j in range(4):
                bias = jnp.concatenate([bias_ref[0, head, kt * j + jj] for jj in range(kt)], axis=1)
                s.append(_dot_nt(qm, ks[j][:, cols]) + bias.astype(F32))
            s.append(_dot_nt(qm, kx_ref[:, cols]))
            m = s[0]
            for t in s[1:-1]:
                m = jnp.maximum(m, t)
            m = jnp.maximum(jnp.max(m, axis=-1, keepdims=True), jnp.max(s[-1], axis=-1, keepdims=True))
            den = jnp.zeros_like(m)
            acc = jnp.zeros((q2.shape[0], LANES), F32)
            vals = vs + (vx_ref,)
            for j in range(5):
                p = jnp.exp2(s[j] - m)
                den = den + jnp.sum(p, axis=-1, keepdims=True)
                acc = acc + _dot(p.astype(BF16), vals[j][:, cols])
            outs.append(acc / den)
        o_ref[:, cols] = jnp.where((lane // NA_HEAD_DIM) == 0, outs[0], outs[1]).astype(o_ref.dtype)


def _na_bias_table(rpb, rows):
    h = rpb.shape[0]
    nqb = rows // NA_QROWS
    row_sel = np.zeros((3, NA_QROWS, NA_KROWS, 2 * NA_WIN_ROWS - 1), np.float32)
    row_ok = np.zeros((3, NA_QROWS, NA_KROWS), bool)
    for v, qb in enumerate((0, min(1, nqb - 1), nqb - 1)):
        start = int(np.clip(NA_QROWS * qb - NA_WIN_ROWS // 2, 0, rows - NA_KROWS))
        for rq in range(NA_QROWS):
            r = NA_QROWS * qb + rq
            r0 = int(np.clip(r - NA_WIN_ROWS // 2, 0, rows - NA_WIN_ROWS))
            for rk in range(NA_KROWS):
                kr = start + rk
                if r0 <= kr < r0 + NA_WIN_ROWS:
                    row_ok[v, rq, rk] = True
                    row_sel[v, rq, rk, kr - r + NA_WIN_ROWS - 1] = 1.0
    col_sel = np.zeros((GRID_W, GRID_W, 2 * NA_WIN_COLS - 1), np.float32)
    col_ok = np.zeros((GRID_W, GRID_W), bool)
    for c in range(GRID_W):
        c0 = int(np.clip(c - NA_WIN_COLS // 2, 0, GRID_W - NA_WIN_COLS))
        for kc in range(c0, c0 + NA_WIN_COLS):
            col_ok[c, kc] = True
            col_sel[c, kc, kc - c + NA_WIN_COLS - 1] = 1.0
    g = LANES // GRID_W
    ncol = 2 * NA_WIN_COLS - 1
    col_sel2 = np.zeros((GRID_W, g * ncol, g * GRID_W), np.float32)
    for kk in range(g):
        col_sel2[:, kk * ncol:(kk + 1) * ncol, kk * GRID_W:(kk + 1) * GRID_W] = col_sel.transpose(0, 2, 1)
    t = jnp.einsum('vqkr,hrc->vhqkc', jnp.asarray(row_sel), rpb.astype(F32), precision=HIGHEST)
    t = t.reshape(3, h, NA_QROWS, NA_KROWS // g, g * ncol)
    t = jnp.einsum('vhqpe,xez->vhpqxz', t, jnp.asarray(col_sel2), precision=HIGHEST)
    ok = (row_ok.reshape(3, NA_QROWS, NA_KROWS // g, g).transpose(0, 2, 1, 3)[:, :, :, None, :, None]
          & col_ok[None, None, None, :, None, :])
    ok = ok.reshape(3, 1, NA_KROWS // g, NA_QROWS, GRID_W, LANES)
    t = jnp.where(jnp.asarray(ok), t * LOG2E, NEG_BIG)
    return t.reshape(3, h, NA_KROWS // g, NA_QROWS * GRID_W, LANES).astype(BF16)


def _na_attention(naq, nak, nav, cx_nak, cx_nav, bias, batch, n, ctx_len):
    rows = n // GRID_W
    qn = NA_QROWS * GRID_W
    kw = NA_KROWS * GRID_W // 4
    nqb = rows // NA_QROWS
    kblocks = n // kw
    width = NA_PAIRS_PER_STEP * LANES
    heads_per_step = width // NA_HEAD_DIM

    def kmap(j):
        def f(hp, qb, b):
            start = jnp.clip(2 * qb - 1, 0, kblocks - 4)
            return (b * kblocks + start + j, hp)
        return f

    def bias_map(hp, qb, b):
        v = jnp.where(qb == 0, 0, jnp.where(qb == nqb - 1, 2, 1))
        return (v, hp, 0, 0, 0)

    kv_specs = [pl.BlockSpec((kw, width), kmap(j)) for j in range(4)]
    return pl.pallas_call(
        _na_kernel,
        grid=(NA_WIDTH // width, nqb, batch),
        in_specs=[pl.BlockSpec((qn, width), lambda hp, qb, b: (b * nqb + qb, hp))]
                 + kv_specs + kv_specs
                 + [pl.BlockSpec((ctx_len, width), lambda hp, qb, b: (b, hp)),
                    pl.BlockSpec((ctx_len, width), lambda hp, qb, b: (b, hp)),
                    pl.BlockSpec((1, heads_per_step, 4 * kw // LANES, qn, LANES), bias_map)],
        out_specs=pl.BlockSpec((qn, width), lambda hp, qb, b: (b * nqb + qb, hp)),
        out_shape=jax.ShapeDtypeStruct((batch * n, NA_WIDTH), BF16),
        compiler_params=_cparams("arbitrary", "arbitrary", "arbitrary"),
        name="na_attention",
    )(naq, nak, nak, nak, nak, nav, nav, nav, nav, cx_nak, cx_nav, bias)


def _ret_kernel(reverse, lg_ref, q_ref, k_ref, v_ref, kx_ref, vx_ref, *rest):
    if reverse:
        of_ref, gate_ref, o_ref, s_ref, d_ref, xi_ref, zeta_ref = rest
    else:
        o_ref, s_ref, d_ref, xi_ref, zeta_ref = rest
    c = q_ref.shape[0]
    first = (pl.program_id(0) == 0) & (pl.program_id(1) == 0)
    lane = lax.broadcasted_iota(jnp.int32, (1, LANES), 1)

    @pl.when(first)
    def _():
        i = lax.broadcasted_iota(jnp.int32, (c, c), 0)
        j = lax.broadcasted_iota(jnp.int32, (c, c), 1)
        diff = (j - i) if reverse else (i - j)
        difff = jnp.maximum(diff, 0).astype(F32)
        r = lax.broadcasted_iota(jnp.int32, (c, LANES), 0).astype(F32)
        for h in range(RET_HEADS):
            lg = lg_ref[h]
            d_ref[h] = jnp.where(diff >= 0, jnp.exp(difff * lg), 0.0)
            xi_ref[h] = jnp.exp(((c - r) if reverse else (r + 1.0)) * lg)
        for p in range(RET_HEADS // 2):
            lgp = jnp.where(lane < RET_DK, lg_ref[2 * p], lg_ref[2 * p + 1])
            zeta_ref[p] = jnp.exp((r if reverse else (c - 1.0 - r)) * lgp)

    @pl.when(pl.program_id(1) == 0)
    def _():
        l = kx_ref.shape[0]
        r = lax.broadcasted_iota(jnp.int32, (l, LANES), 0).astype(F32)
        for p in range(RET_HEADS // 2):
            lgp = jnp.where(lane < RET_DK, lg_ref[2 * p], lg_ref[2 * p + 1])
            zc = jnp.exp((r if reverse else (l - 1.0 - r)) * lgp)
            kz = kx_ref[:, p * LANES:(p + 1) * LANES].astype(F32) * zc
            for hh in range(2):
                h = 2 * p + hh
                km = jnp.where((lane // RET_DK) == hh, kz, 0.0).astype(BF16)
                s_ref[h] = _dot_tn(km, vx_ref[:, h * RET_DV:(h + 1) * RET_DV])

    outs = []
    for p in range(RET_HEADS // 2):
        q2 = q_ref[:, p * LANES:(p + 1) * LANES]
        k2 = k_ref[:, p * LANES:(p + 1) * LANES]
        kz = k2.astype(F32) * zeta_ref[p]
        for hh in range(2):
            h = 2 * p + hh
            sel = (lane // RET_DK) == hh
            qm = jnp.where(sel, q2, jnp.zeros_like(q2))
            km = jnp.where(sel, k2, jnp.zeros_like(k2))
            vh = v_ref[:, h * RET_DV:(h + 1) * RET_DV]
            inner = (_dot_nt(qm, km) * d_ref[h]).astype(BF16)
            state = s_ref[h]
            o = _dot(inner, vh) + _dot(qm, state.astype(BF16)) * xi_ref[h]
            g_chunk = jnp.exp(jnp.full((1, LANES), c, F32) * lg_ref[h])
            s_ref[h] = state * g_chunk + _dot_tn(jnp.where(sel, kz, 0.0).astype(BF16), vh)
            outs.append(o)
    if reverse:
        for h in range(RET_HEADS):
            tot = outs[h] + of_ref[:, h * RET_DV:(h + 1) * RET_DV]
            ms = jnp.mean(tot * tot, axis=-1, keepdims=True)
            gate = gate_ref[:, h * RET_DV:(h + 1) * RET_DV].astype(F32)
            o_ref[:, h * RET_DV:(h + 1) * RET_DV] = (tot * lax.rsqrt(ms + NORM_EPS) * gate).astype(o_ref.dtype)
    else:
        for h in range(RET_HEADS):
            o_ref[:, h * RET_DV:(h + 1) * RET_DV] = outs[h]


def _retention_pass(reverse, log_g, rq, rk, rv, cx_rk, cx_rv, batch, n, ctx_len, o_fwd=None, gate=None):
    c = min(RET_CHUNK, n)
    nc = n // c
    if reverse:
        seq = lambda b, i: (b * nc + nc - 1 - i, 0)
    else:
        seq = lambda b, i: (b * nc + i, 0)
    ctx = lambda b, i: (b, 0)
    in_specs = [pl.BlockSpec(memory_space=pltpu.SMEM),
                pl.BlockSpec((c, RET_QK_WIDTH), seq),
                pl.BlockSpec((c, RET_QK_WIDTH), seq),
                pl.BlockSpec((c, RET_V_WIDTH), seq),
                pl.BlockSpec((ctx_len, RET_QK_WIDTH), ctx),
                pl.BlockSpec((ctx_len, RET_V_WIDTH), ctx)]
    args = [log_g, rq, rk, rv, cx_rk, cx_rv]
    if reverse:
        in_specs += [pl.BlockSpec((c, RET_V_WIDTH), seq), pl.BlockSpec((c, RET_V_WIDTH), seq)]
        args += [o_fwd, gate]
    return pl.pallas_call(
        functools.partial(_ret_kernel, reverse),
        grid=(batch, nc),
        in_specs=in_specs,
        out_specs=pl.BlockSpec((c, RET_V_WIDTH), seq),
        out_shape=jax.ShapeDtypeStruct((batch * n, RET_V_WIDTH), BF16 if reverse else F32),
        scratch_shapes=[pltpu.VMEM((RET_HEADS, LANES, RET_DV), F32),
                        pltpu.VMEM((RET_HEADS, c, c), F32),
                        pltpu.VMEM((RET_HEADS, c, RET_DV), F32),
                        pltpu.VMEM((RET_HEADS // 2, c, LANES), F32)],
        compiler_params=_cparams("arbitrary", "arbitrary"),
        name="retention_bwd" if reverse else "retention_fwd",
    )(*args)


def _outproj_kernel(x_ref, ona_ref, oret_ref, w_ref, mod_ref, gain_ref, x1_ref, h_ref):
    mix = _dot(ona_ref[...], w_ref[0:NA_WIDTH, :]) + _dot(oret_ref[...], w_ref[NA_WIDTH:, :])
    x1 = x_ref[...] + mod_ref[0, 2:3, :] * mix
    x1_ref[...] = x1
    h_ref[...] = _rms_mod(x1, gain_ref[...], mod_ref[0, 3:4, :], mod_ref[0, 4:5, :]).astype(BF16)


def _outproj(x2d, ona, oret, w_out_bf, mods, gain, tm, tiles_per_batch):
    t, d = x2d.shape
    row = lambda i: (i, 0)
    const = lambda i: (0, 0)
    return pl.pallas_call(
        _outproj_kernel,
        grid=(t // tm,),
        in_specs=[pl.BlockSpec((tm, d), row),
                  pl.BlockSpec((tm, NA_WIDTH), row),
                  pl.BlockSpec((tm, RET_V_WIDTH), row),
                  pl.BlockSpec((NA_WIDTH + RET_V_WIDTH, d), const),
                  pl.BlockSpec((1, 6, d), lambda i: (i // tiles_per_batch, 0, 0)),
                  pl.BlockSpec((1, d), const)],
        out_specs=[pl.BlockSpec((tm, d), row), pl.BlockSpec((tm, d), row)],
        out_shape=[jax.ShapeDtypeStruct((t, d), F32), jax.ShapeDtypeStruct((t, d), BF16)],
        compiler_params=_cparams("arbitrary"),
        name="outproj",
    )(x2d, ona, oret, w_out_bf, mods, gain)


def _swiglu_kernel(h_ref, x1_ref, wg_ref, wu_ref, wd_ref, mod_ref, o_ref, acc_ref):
    j = pl.program_id(1)

    @pl.when(j == 0)
    def _():
        acc_ref[...] = jnp.zeros_like(acc_ref)

    h = h_ref[...]
    g = _dot(h, wg_ref[...])
    u = _dot(h, wu_ref[...])
    a = (g * _sigmoid(g) * u).astype(BF16)
    acc_ref[...] += _dot(a, wd_ref[...])

    @pl.when(j == pl.num_programs(1) - 1)
    def _():
        o_ref[...] = x1_ref[...] + mod_ref[0, 5:6, :] * acc_ref[...]


def _swiglu(h_bf, x1, w_gate_up_bf, w_down_bf, mods, tm, tiles_per_batch):
    t, d = x1.shape
    f = w_down_bf.shape[0]
    nf = 2 if (f // 2) % LANES == 0 else 1
    tf = f // nf
    return pl.pallas_call(
        _swiglu_kernel,
        grid=(t // tm, nf),
        in_specs=[pl.BlockSpec((tm, d), lambda i, j: (i, 0)),
                  pl.BlockSpec((tm, d), lambda i, j: (i, 0)),
                  pl.BlockSpec((d, tf), lambda i, j: (0, j)),
                  pl.BlockSpec((d, tf), lambda i, j: (0, nf + j)),
                  pl.BlockSpec((tf, d), lambda i, j: (j, 0)),
                  pl.BlockSpec((1, 6, d), lambda i, j: (i // tiles_per_batch, 0, 0))],
        out_specs=pl.BlockSpec((tm, d), lambda i, j: (i, 0)),
        out_shape=jax.ShapeDtypeStruct((t, d), F32),
        scratch_shapes=[pltpu.VMEM((tm, d), F32)],
        compiler_params=_cparams("arbitrary", "arbitrary"),
        name="swiglu",
    )(h_bf, x1, w_gate_up_bf, w_gate_up_bf, w_down_bf, mods)


def _dft_tables(n, d):
    n2 = FFT_N2
    n1 = n // n2
    s = SUBLANES
    eye = np.eye(s)
    ang1 = 2.0 * np.pi * (np.outer(np.arange(n1), np.arange(n1)) % n1) / n1
    f1 = np.stack([np.cos(ang1), -np.sin(ang1)]) / np.sqrt(n)
    f1k = np.einsum('ckn,jl->ckjnl', f1, eye).reshape(2 * n1 * s, n1 * s)
    angt = 2.0 * np.pi * (np.outer(np.arange(n1), np.arange(n2)) % n) / n
    tw = np.stack([np.cos(angt), -np.sin(angt)])
    tw = tw.reshape(2, n1, n2 // s, s).transpose(0, 2, 1, 3).reshape(2, n2 // s, n1 * s, 1)
    tw = np.broadcast_to(tw, (2, n2 // s, n1 * s, LANES))
    ang2 = 2.0 * np.pi * (np.outer(np.arange(n2), np.arange(n2)) % n2) / n2
    cr, ci = np.cos(ang2), -np.sin(ang2)
    m2 = np.array([[cr, -ci], [ci, cr]])
    m2k = np.einsum('abkn,jl->akjbln', m2, eye).reshape(2 * n2 * s, 2 * s * n2)
    gw = d // FOURIER_GROUPS
    angc = 2.0 * np.pi * (np.outer(np.arange(gw), np.arange(gw)) % gw) / gw
    cs = np.concatenate([np.cos(angc), np.sin(angc)], axis=0) / np.sqrt(gw)
    return (f1k.astype(np.float32), tw.astype(np.float32), m2k.astype(np.float32),
            cs.astype(np.float32))


def _fold_kernel(cs_ref, w_ref, o_ref):
    gw = w_ref.shape[0]
    o_ref[0] = jnp.dot(cs_ref[0:gw, :], w_ref[...], precision=HIGHEST, preferred_element_type=F32)
    o_ref[1] = jnp.dot(cs_ref[gw:, :], w_ref[...], precision=HIGHEST, preferred_element_type=F32)


def _fold_channel_dft(cs, w_fourier):
    d = w_fourier.shape[0]
    gw = d // FOURIER_GROUPS
    return pl.pallas_call(
        _fold_kernel,
        grid=(FOURIER_GROUPS,),
        in_specs=[pl.BlockSpec((2 * gw, gw), lambda g: (0, 0)),
                  pl.BlockSpec((gw, d), lambda g: (g, 0))],
        out_specs=pl.BlockSpec((2, gw, d), lambda g: (0, g, 0)),
        out_shape=jax.ShapeDtypeStruct((2, d, d), F32),
        compiler_params=_cparams("arbitrary"),
        name="fold_channel_dft",
    )(cs, w_fourier)


def _fft1_kernel(x_ref, mod_ref, gain_ref, f_ref, tw_ref, o_ref):
    _, n1, s, d = x_ref.shape
    x = x_ref[...].reshape(n1 * s, d)
    h = _rms_mod(x, gain_ref[...], mod_ref[0, 0:1, :], mod_ref[0, 1:2, :]).astype(BF16)
    a = _dot(f_ref[...], h)
    ar = a[:n1 * s]
    ai = a[n1 * s:]
    reps = d // LANES
    twr = jnp.tile(tw_ref[0, 0], (1, reps))
    twi = jnp.tile(tw_ref[1, 0], (1, reps))
    o_ref[0, 0] = (ar * twr - ai * twi).reshape(n1, s, d)
    o_ref[0, 1] = (ar * twi + ai * twr).reshape(n1, s, d)


def _fft_stage1(x2, mods, gain, f1k_bf, tw, batch, n):
    d = x2.shape[-1]
    n2 = FFT_N2
    n1 = n // n2
    s = SUBLANES
    xv = x2.reshape(batch, n1, n2, d)
    return pl.pallas_call(
        _fft1_kernel,
        grid=(n2 // s, batch),
        in_specs=[pl.BlockSpec((1, n1, s, d), lambda g, b: (b, 0, g, 0)),
                  pl.BlockSpec((1, 6, d), lambda g, b: (b, 0, 0)),
                  pl.BlockSpec((1, d), lambda g, b: (0, 0)),
                  pl.BlockSpec((2 * n1 * s, n1 * s), lambda g, b: (0, 0)),
                  pl.BlockSpec((2, 1, n1 * s, LANES), lambda g, b: (0, g, 0, 0))],
        out_specs=pl.BlockSpec((1, 2, n1, s, d), lambda g, b: (b, 0, 0, g, 0)),
        out_shape=jax.ShapeDtypeStruct((batch, 2, n1, n2, d), F32),
        compiler_params=_cparams("arbitrary", "arbitrary"),
        name="fft_stage1",
    )(xv, mods, gain, f1k_bf, tw)


def _fft2_kernel(n_experts, b_ref, x_ref, m_ref, wf_ref, mod_ref, gain_ref, wr_ref,
                 x3_ref, h2_ref, route_ref):
    _, _, s, n2, d = b_ref.shape
    rows = n2 * s
    bmat = b_ref[...].reshape(2 * s * n2, d).astype(BF16)
    z = _dot(m_ref[...], bmat)
    y = _dot(z[:rows].astype(BF16), wf_ref[0]) + _dot(z[rows:].astype(BF16), wf_ref[1])
    x3 = x_ref[...].reshape(rows, d) + mod_ref[0, 2:3, :] * y
    x3_ref[...] = x3.reshape(x3_ref.shape)
    h2 = _rms_mod(x3, gain_ref[...], mod_ref[0, 3:4, :], mod_ref[0, 4:5, :])
    h2_ref[...] = h2.reshape(rows, d // LANES, LANES).reshape(h2_ref.shape)
    h_hi = h2.astype(BF16)
    h_lo = (h2 - h_hi.astype(F32)).astype(BF16)
    wr = wr_ref[...]
    w_hi = wr.astype(BF16)
    w_lo = (wr - w_hi.astype(F32)).astype(BF16)
    logits = _dot(h_hi, w_hi) + (_dot(h_lo, w_hi) + _dot(h_hi, w_lo))
    lane = lax.broadcasted_iota(jnp.int32, logits.shape, 1).astype(F32)
    lg = jnp.where(lane < n_experts, logits, -jnp.inf)
    m1 = jnp.max(lg, axis=-1, keepdims=True)
    i1 = jnp.min(jnp.where(lg == m1, lane, float(LANES)), axis=-1, keepdims=True)
    lg2 = jnp.where(lane == i1, -jnp.inf, lg)
    m2 = jnp.max(lg2, axis=-1, keepdims=True)
    i2 = jnp.min(jnp.where(lg2 == m2, lane, float(LANES)), axis=-1, keepdims=True)
    e = jnp.exp(m2 - m1)
    g1 = 1.0 / (1.0 + e)
    g2 = e / (1.0 + e)
    route = jnp.where(lane == 0, g1, jnp.where(lane == 1, g2, jnp.where(lane == 2, i1,
                                                                       jnp.where(lane == 3, i2, 0.0))))
    route_ref[...] = route.reshape(route_ref.shape)


def _fft_stage2(bmid, x2, m2k_bf, wf_bf, mods, gain, wr_pad, n_experts, batch, n):
    d = x2.shape[-1]
    n2 = FFT_N2
    n1 = n // n2
    s = SUBLANES
    xv = x2.reshape(batch, n2, n1, d)
    pos = lambda g, b: (b, 0, g, 0)
    const2 = lambda g, b: (0, 0)
    x3, h2, route = pl.pallas_call(
        functools.partial(_fft2_kernel, n_experts),
        grid=(n1 // s, batch),
        in_specs=[pl.BlockSpec((1, 2, s, n2, d), lambda g, b: (b, 0, g, 0, 0)),
                  pl.BlockSpec((1, n2, s, d), pos),
                  pl.BlockSpec((2 * n2 * s, 2 * n2 * s), const2),
                  pl.BlockSpec((2, d, d), lambda g, b: (0, 0, 0)),
                  pl.BlockSpec((1, 6, d), lambda g, b: (b, 0, 0)),
                  pl.BlockSpec((1, d), const2),
                  pl.BlockSpec((d, LANES), const2)],
        out_specs=[pl.BlockSpec((1, n2, s, d), pos),
                   pl.BlockSpec((1, n2, s, d // LANES, LANES), lambda g, b: (b, 0, g, 0, 0)),
                   pl.BlockSpec((1, n2, s, LANES), pos)],
        out_shape=[jax.ShapeDtypeStruct((batch, n2, n1, d), F32),
                   jax.ShapeDtypeStruct((batch, n2, n1, d // LANES, LANES), F32),
                   jax.ShapeDtypeStruct((batch, n2, n1, LANES), F32)],
        compiler_params=_cparams("arbitrary", "arbitrary"),
        name="fft_stage2",
    )(bmid, xv, m2k_bf, wf_bf, mods, gain, wr_pad)
    t = batch * n
    return x3.reshape(t, d), h2.reshape(t, d // LANES, LANES), route.reshape(t, LANES)


def _dispatch_kernel(tm, slot_ref, h_ref, dst_in_ref, dst_ref, sem):
    del dst_in_ref
    base = pl.program_id(0) * tm * TOP_K

    def body(r, carry):
        for k in range(TOP_K):
            s = slot_ref[base + TOP_K * r + k]
            pltpu.make_async_copy(h_ref.at[pl.ds(r, 1)], dst_ref.at[pl.ds(s, 1)], sem).start(priority=k % 2)
        return carry

    lax.fori_loop(0, tm, body, 0, unroll=8)

    def drain(r, carry):
        pltpu.make_async_copy(h_ref.at[pl.ds(0, 1)], dst_ref.at[pl.ds(0, 1)], sem).wait()
        return carry

    lax.fori_loop(0, tm * TOP_K, drain, 0, unroll=8)


def _dispatch(h2, slot, n_slots, tm):
    t = h2.shape[0]
    tile = h2.shape[1:]
    return pl.pallas_call(
        functools.partial(_dispatch_kernel, tm),
        grid_spec=pltpu.PrefetchScalarGridSpec(
            num_scalar_prefetch=1,
            grid=(t // tm,),
            in_specs=[pl.BlockSpec((tm,) + tile, lambda i, sl: (i, 0, 0)),
                      pl.BlockSpec(memory_space=pl.ANY)],
            out_specs=pl.BlockSpec(memory_space=pl.ANY),
            scratch_shapes=[pltpu.SemaphoreType.DMA(())]),
        out_shape=jax.ShapeDtypeStruct((n_slots,) + tile, h2.dtype),
        input_output_aliases={2: 0},
        compiler_params=pltpu.CompilerParams(dimension_semantics=("arbitrary",), has_side_effects=True,
                                             vmem_limit_bytes=VMEM_LIMIT),
        name="dispatch",
    )(slot, h2, jnp.zeros((n_slots,) + tile, h2.dtype))


def _expert_kernel(bexp_ref, nused_ref, x_ref, wg_ref, wu_ref, wd_ref, o_ref, xb_ref, acc_ref):
    i = pl.program_id(0)
    j = pl.program_id(1)
    last = pl.num_programs(1) - 1
    used = i < nused_ref[0]

    @pl.when(used)
    def _():
        @pl.when(j == 0)
        def _():
            xb_ref[...] = x_ref[...].reshape(xb_ref.shape).astype(BF16)
            acc_ref[...] = jnp.zeros_like(acc_ref)

        xb = xb_ref[...]
        g = _dot(xb, wg_ref[0])
        u = _dot(xb, wu_ref[0])
        a = (g * _sigmoid(g) * u).astype(BF16)
        acc_ref[...] += _dot(a, wd_ref[0])

        @pl.when(j == last)
        def _():
            o_ref[...] = acc_ref[...].reshape(o_ref.shape)

    @pl.when(jnp.logical_not(used) & (j == last))
    def _():
        o_ref[...] = jnp.zeros_like(o_ref)


def _expert_ffn(xs, w_gu_bf, w_dn_bf, block_expert, n_used):
    n_slots, sub, _ = xs.shape
    e, f, d = w_dn_bf.shape
    nb = n_slots // MOE_BLOCK
    nf = 4 if (f // 4) % LANES == 0 else 1
    tf = f // nf

    def xmap(i, j, be, nu):
        return (jnp.minimum(i, nu[0] - 1), 0, 0)

    def jeff(i, j, nu):
        return jnp.where(i < nu[0], j, nf - 1)

    return pl.pallas_call(
        _expert_kernel,
        grid_spec=pltpu.PrefetchScalarGridSpec(
            num_scalar_prefetch=2,
            grid=(nb, nf),
            in_specs=[pl.BlockSpec((MOE_BLOCK, sub, LANES), xmap),
                      pl.BlockSpec((1, d, tf), lambda i, j, be, nu: (be[i], 0, jeff(i, j, nu))),
                      pl.BlockSpec((1, d, tf), lambda i, j, be, nu: (be[i], 0, nf + jeff(i, j, nu))),
                      pl.BlockSpec((1, tf, d), lambda i, j, be, nu: (be[i], jeff(i, j, nu), 0))],
            out_specs=pl.BlockSpec((MOE_BLOCK, sub, LANES), lambda i, j, be, nu: (i, 0, 0)),
            scratch_shapes=[pltpu.VMEM((MOE_BLOCK, d), BF16), pltpu.VMEM((MOE_BLOCK, d), F32)]),
        out_shape=jax.ShapeDtypeStruct((n_slots, sub, LANES), F32),
        compiler_params=_cparams("arbitrary", "arbitrary"),
        name="expert_ffn",
    )(block_expert, n_used, xs, w_gu_bf, w_gu_bf, w_dn_bf)


def _combine_kernel(tm, slot_ref, x_ref, route_ref, mod_ref, ys_ref, o_ref, ybuf, sem):
    i = pl.program_id(0)

    def issue(step, b):
        base = step * (tm * TOP_K)

        def body(r, carry):
            for k in range(TOP_K):
                s = slot_ref[base + TOP_K * r + k]
                pltpu.make_async_copy(ys_ref.at[pl.ds(s, 1)], ybuf.at[b, k, pl.ds(r, 1)],
                                      sem.at[b]).start(priority=k % 2)
            return carry

        lax.fori_loop(0, tm, body, 0, unroll=8)

    @pl.when(i == 0)
    def _():
        issue(0, 0)

    @pl.when(i + 1 < pl.num_programs(0))
    def _():
        issue(i + 1, (i + 1) % 2)

    b = i % 2

    def drain(r, carry):
        pltpu.make_async_copy(ys_ref.at[pl.ds(0, 1)], ybuf.at[b, 0, pl.ds(0, 1)], sem.at[b]).wait()
        return carry

    lax.fori_loop(0, tm * TOP_K, drain, 0, unroll=8)
    y0 = ybuf[b, 0].reshape(x_ref.shape)
    y1 = ybuf[b, 1].reshape(x_ref.shape)
    g0 = route_ref[:, 0:1]
    g1 = route_ref[:, 1:2]
    o_ref[...] = x_ref[...] + mod_ref[0, 5:6, :] * (g0 * y0 + g1 * y1)


def _combine(x3, ys, slot, route, mods, tm, tiles_per_batch):
    t, d = x3.shape
    tile = ys.shape[1:]
    row = lambda i, sl: (i, 0)
    return pl.pallas_call(
        functools.partial(_combine_kernel, tm),
        grid_spec=pltpu.PrefetchScalarGridSpec(
            num_scalar_prefetch=1,
            grid=(t // tm,),
            in_specs=[pl.BlockSpec((tm, d), row),
                      pl.BlockSpec((tm, LANES), row),
                      pl.BlockSpec((1, 6, d), lambda i, sl: (i // tiles_per_batch, 0, 0)),
                      pl.BlockSpec(memory_space=pl.ANY)],
            out_specs=pl.BlockSpec((tm, d), row),
            scratch_shapes=[pltpu.VMEM((2, TOP_K, tm) + tile, F32), pltpu.SemaphoreType.DMA((2,))]),
        out_shape=jax.ShapeDtypeStruct((t, d), F32),
        compiler_params=_cparams("arbitrary"),
        name="combine",
    )(slot, x3, route, mods, ys)


def _route_slots(route, n_experts):
    t = route.shape[0]
    expert = route[:, 2:2 + TOP_K].astype(jnp.int32).reshape(t * TOP_K)
    onehot = (expert[:, None] == jnp.arange(n_experts, dtype=jnp.int32)[None, :]).astype(jnp.int32)
    csum = jnp.cumsum(onehot, axis=0)
    counts = csum[-1]
    padded = (counts + MOE_BLOCK - 1) // MOE_BLOCK * MOE_BLOCK
    pad_end = jnp.cumsum(padded)
    pad_start = pad_end - padded
    slot = jnp.sum(onehot * (pad_start[None, :] + csum - 1), axis=1).astype(jnp.int32)
    n_slots = t * TOP_K + n_experts * MOE_BLOCK
    nb = n_slots // MOE_BLOCK
    block_expert = jnp.minimum(
        jnp.searchsorted(pad_end, jnp.arange(nb, dtype=jnp.int32) * MOE_BLOCK, side='right'),
        n_experts - 1).astype(jnp.int32)
    n_used = (pad_end[-1:] // MOE_BLOCK).astype(jnp.int32)
    return slot, block_expert, n_used, n_slots


def _rope_tables(n):
    pos = np.arange(n)
    row = (pos // GRID_W).astype(np.float32)
    col = (pos % GRID_W).astype(np.float32)
    n_freq = RET_DK // 4
    inv_freq = (np.float32(ROPE_BASE) ** (-np.arange(n_freq, dtype=np.float32) / n_freq)).astype(np.float32)
    ang = np.concatenate([row[:, None] * inv_freq, col[:, None] * inv_freq], axis=-1)
    cos = np.cos(ang.astype(np.float64))
    sin = np.sin(ang.astype(np.float64))
    cos_h = np.repeat(cos, 2, axis=-1)
    sin_h = np.stack([-sin, sin], axis=-1).reshape(n, RET_DK)
    return (np.tile(cos_h, (1, RET_HEADS)).astype(np.float32),
            np.tile(sin_h, (1, RET_HEADS)).astype(np.float32))


def kernel(x, c, ctx, c_ctx, l0_norm_mix, l0_norm_ffn, l0_w_ada, l0_b_ada, l0_w_in, l0_q_gain, l0_k_gain, l0_rpb, l0_ret_decay_fwd, l0_ret_decay_bwd, l0_w_out, l0_w_gate_up, l0_w_down, l1_norm_mix, l1_norm_ffn, l1_w_ada, l1_b_ada, l1_w_fourier, l1_w_router, l1_w_exp_gate_up, l1_w_exp_down):
    batch, n, d = x.shape
    ctx_len = ctx.shape[1]
    t = batch * n
    rows = n // GRID_W
    n_experts = l1_w_router.shape[1]
    assert n % (NA_QROWS * GRID_W) == 0 and rows >= NA_KROWS and n % FFT_N2 == 0
    assert l0_w_in.shape[1] == _IN_WIDTH and d % LANES == 0

    x2d = x.reshape(t, d)
    ctx2d = ctx.reshape(batch * ctx_len, d)

    ada0 = _ada(jnp.concatenate([c, c_ctx[None, :]], axis=0), l0_w_ada, l0_b_ada)
    mods0 = ada0[:batch].reshape(batch, 6, d)
    mods_ctx = ada0[batch:batch + 1].reshape(1, 6, d)
    mods1 = _ada(c, l1_w_ada, l1_b_ada)[:batch].reshape(batch, 6, d)

    w_in_bf = l0_w_in.astype(BF16)
    qg = jnp.tile(l0_q_gain.astype(F32), NA_HEADS).reshape(1, NA_WIDTH)
    kg = jnp.tile(l0_k_gain.astype(F32), NA_HEADS).reshape(1, NA_WIDTH)
    gmat = jnp.asarray(np.kron(np.eye(NA_HEADS), np.full((NA_HEAD_DIM, NA_HEAD_DIM), 1.0 / NA_HEAD_DIM)),
                       BF16)
    cos_np, sin_np = _rope_tables(n)
    gain_mix0 = l0_norm_mix.reshape(1, d)
    tm = min(INPROJ_ROWS, n)
    naq, rq, rg, nak, nav, rk, rv = _inproj(
        x2d, mods0, gain_mix0, w_in_bf, qg, kg, gmat, jnp.asarray(cos_np), jnp.asarray(sin_np),
        tm, n // tm, n // tm)
    ctx_rows = batch * ctx_len
    _, _, _, cx_nak, cx_nav, cx_rk, cx_rv = _inproj(
        ctx2d, mods_ctx, gain_mix0, w_in_bf, qg, kg, gmat,
        jnp.ones((ctx_len, RET_QK_WIDTH), F32), jnp.zeros((ctx_len, RET_QK_WIDTH), F32),
        ctx_len, ctx_rows // ctx_len, 1)

    bias = _na_bias_table(l0_rpb, rows)
    o_na = _na_attention(naq, nak, nav, cx_nak, cx_nav, bias, batch, n, ctx_len)

    lg_f = jax.nn.log_sigmoid(l0_ret_decay_fwd.astype(F32))
    lg_b = jax.nn.log_sigmoid(l0_ret_decay_bwd.astype(F32))
    o_f = _retention_pass(False, lg_f, rq, rk, rv, cx_rk, cx_rv, batch, n, ctx_len)
    o_ret = _retention_pass(True, lg_b, rq, rk, rv, cx_rk, cx_rv, batch, n, ctx_len, o_fwd=o_f, gate=rg)

    x1, h_ffn = _outproj(x2d, o_na, o_ret, l0_w_out.astype(BF16), mods0, l0_norm_ffn.reshape(1, d),
                         tm, n // tm)
    tf_rows = min(FFN_ROWS, n)
    x2 = _swiglu(h_ffn, x1, l0_w_gate_up.astype(BF16), l0_w_down.astype(BF16), mods0,
                 tf_rows, n // tf_rows)

    f1k, tw, m2k, cs = _dft_tables(n, d)
    wf = _fold_channel_dft(jnp.asarray(cs), l1_w_fourier).astype(BF16)
    bmid = _fft_stage1(x2, mods1, l1_norm_mix.reshape(1, d), jnp.asarray(f1k, BF16), jnp.asarray(tw),
                       batch, n)
    wr_pad = jnp.zeros((d, LANES), F32).at[:, :n_experts].set(l1_w_router)
    x3, h2, route = _fft_stage2(bmid, x2, jnp.asarray(m2k, BF16), wf, mods1, l1_norm_ffn.reshape(1, d),
                                wr_pad, n_experts, batch, n)

    slot, block_expert, n_used, n_slots = _route_slots(route, n_experts)
    xs = _dispatch(h2, slot, n_slots, tm)
    ys = _expert_ffn(xs, l1_w_exp_gate_up.astype(BF16), l1_w_exp_down.astype(BF16), block_expert, n_used)
    out = _combine(x3, ys, slot, route, mods1, tm, n // tm)
    return out.reshape(batch, n, d)
```

```python
import functools

import numpy as np
import jax
import jax.numpy as jnp
from jax import lax
from jax.experimental import pallas as pl
from jax.experimental.pallas import tpu as pltpu

F32 = jnp.float32
BF16 = jnp.bfloat16
HIGHEST = lax.Precision.HIGHEST

GRID_W = 64
NA_HEADS = 8
NA_HEAD_DIM = 64
NA_WIN_ROWS = 8
NA_WIN_COLS = 16
RET_HEADS = 4
RET_DK = 64
RET_DV = 128
ROPE_BASE = 10000.0
FOURIER_GROUPS = 8
TOP_K = 2
NORM_EPS = 1e-6

NA_WIDTH = NA_HEADS * NA_HEAD_DIM
RET_QK_WIDTH = RET_HEADS * RET_DK
RET_V_WIDTH = RET_HEADS * RET_DV

LANES = 128
SUBLANES = 8
VMEM_LIMIT = 48 * 1024 * 1024

NEG_BIG = -1e30
LOG2E = 1.4426950408889634

INPROJ_ROWS = 512
NA_QROWS = 8
NA_KROWS = 16
NA_PAIRS_PER_STEP = 4
RET_CHUNK = 512
FFN_ROWS = 1024
FFT_N2 = 64
MOE_BLOCK = 512
EXPERT_F_TILES = 2
EXPERT_VMEM_LIMIT = 56 * 1024 * 1024


def _cparams(*sem):
    return pltpu.CompilerParams(dimension_semantics=sem, vmem_limit_bytes=VMEM_LIMIT)


def _sigmoid(x):
    return 0.5 * jnp.tanh(0.5 * x) + 0.5


def _dot(a, b):
    return jnp.dot(a, b, preferred_element_type=F32)


def _dot_nt(a, b):
    return lax.dot_general(a, b, (((1,), (1,)), ((), ())), preferred_element_type=F32)


def _dot_tn(a, b):
    return lax.dot_general(a, b, (((0,), (0,)), ((), ())), preferred_element_type=F32)


def _rms_mod(x, gain, shift, scale):
    ms = jnp.mean(x * x, axis=-1, keepdims=True)
    return (x * lax.rsqrt(ms + NORM_EPS) * gain) * (1.0 + scale) + shift


def _ada_kernel(c_ref, w_ref, b_ref, o_ref):
    c = c_ref[...]
    s = c * _sigmoid(c)
    o_ref[...] = jnp.dot(s, w_ref[...], precision=HIGHEST, preferred_element_type=F32) + b_ref[...]


def _ada(cond, w_ada, b_ada):
    d, n = w_ada.shape
    cond8 = jnp.zeros((SUBLANES, d), F32).at[:cond.shape[0]].set(cond)
    tn = 1024
    return pl.pallas_call(
        _ada_kernel,
        grid=(n // tn,),
        in_specs=[pl.BlockSpec((SUBLANES, d), lambda j: (0, 0)),
                  pl.BlockSpec((d, tn), lambda j: (0, j)),
                  pl.BlockSpec((1, tn), lambda j: (0, j))],
        out_specs=pl.BlockSpec((SUBLANES, tn), lambda j: (0, j)),
        out_shape=jax.ShapeDtypeStruct((SUBLANES, n), F32),
        compiler_params=_cparams("arbitrary"),
        name="ada",
    )(cond8, w_ada, b_ada.reshape(1, n))


_O_NAQ = 0
_O_RQ = _O_NAQ + NA_WIDTH
_O_RG = _O_RQ + RET_QK_WIDTH
_O_NAK = _O_RG + RET_V_WIDTH
_O_NAV = _O_NAK + NA_WIDTH
_O_RK = _O_NAV + NA_WIDTH
_O_RV = _O_RK + RET_QK_WIDTH
_IN_WIDTH = _O_RV + RET_V_WIDTH


def _inproj_kernel(x_ref, mod_ref, gain_ref, w_ref, qg_ref, kg_ref, gmat_ref, cos_ref, sin_ref,
                   naq_ref, rq_ref, rg_ref, nak_ref, nav_ref, rk_ref, rv_ref):
    h = _rms_mod(x_ref[...], gain_ref[...], mod_ref[0, 0:1, :], mod_ref[0, 1:2, :]).astype(BF16)
    p = _dot(h, w_ref[...])
    gmat = gmat_ref[...]

    def headnorm(t, g):
        ms = _dot((t * t).astype(BF16), gmat)
        return t * lax.rsqrt(ms + NORM_EPS) * g

    cos = cos_ref[...]
    sin = sin_ref[...]
    lane = lax.broadcasted_iota(jnp.int32, (1, RET_QK_WIDTH), 1)
    even = (lane % 2) == 0

    def rope(t):
        partner = jnp.where(even, pltpu.roll(t, RET_QK_WIDTH - 1, axis=1), pltpu.roll(t, 1, axis=1))
        return t * cos + partner * sin

    naq_ref[...] = (headnorm(p[:, _O_NAQ:_O_RQ], qg_ref[...]) * (NA_HEAD_DIM ** -0.5 * LOG2E)).astype(BF16)
    rq_ref[...] = rope(p[:, _O_RQ:_O_RG]).astype(BF16)
    g = p[:, _O_RG:_O_NAK]
    rg_ref[...] = (g * _sigmoid(g)).astype(BF16)
    nak_ref[...] = headnorm(p[:, _O_NAK:_O_NAV], kg_ref[...]).astype(BF16)
    nav_ref[...] = p[:, _O_NAV:_O_RK].astype(BF16)
    rk_ref[...] = (rope(p[:, _O_RK:_O_RV]) * (RET_DK ** -0.5)).astype(BF16)
    rv_ref[...] = p[:, _O_RV:_IN_WIDTH].astype(BF16)


def _inproj(x2d, mods, gain, w_bf, qg, kg, gmat, cos, sin, tm, tiles_per_mod, pos_tiles):
    t, d = x2d.shape
    widths = (NA_WIDTH, RET_QK_WIDTH, RET_V_WIDTH, NA_WIDTH, NA_WIDTH, RET_QK_WIDTH, RET_V_WIDTH)
    row = lambda i: (i, 0)
    const = lambda i: (0, 0)
    return pl.pallas_call(
        _inproj_kernel,
        grid=(t // tm,),
        in_specs=[pl.BlockSpec((tm, d), row),
                  pl.BlockSpec((1, 6, d), lambda i: (i // tiles_per_mod, 0, 0)),
                  pl.BlockSpec((1, d), const),
                  pl.BlockSpec((d, _IN_WIDTH), const),
                  pl.BlockSpec((1, NA_WIDTH), const),
                  pl.BlockSpec((1, NA_WIDTH), const),
                  pl.BlockSpec((NA_WIDTH, NA_WIDTH), const),
                  pl.BlockSpec((tm, RET_QK_WIDTH), lambda i: (i % pos_tiles, 0)),
                  pl.BlockSpec((tm, RET_QK_WIDTH), lambda i: (i % pos_tiles, 0))],
        out_specs=[pl.BlockSpec((tm, w), row) for w in widths],
        out_shape=[jax.ShapeDtypeStruct((t, w), BF16) for w in widths],
        compiler_params=_cparams("arbitrary"),
        name="inproj",
    )(x2d, mods, gain, w_bf, qg, kg, gmat, cos, sin)


def _na_kernel(q_ref, k0, k1, k2, k3, v0, v1, v2, v3, kx_ref, vx_ref, bias_ref, o_ref):
    lane = lax.broadcasted_iota(jnp.int32, (1, LANES), 1)
    ks = (k0, k1, k2, k3)
    vs = (v0, v1, v2, v3)
    kt = k0.shape[0] // LANES
    for pp in range(q_ref.shape[1] // LANES):
        cols = slice(pp * LANES, (pp + 1) * LANES)
        q2 = q_ref[:, cols]
        outs = []
        for hh in range(2):
            head = 2 * pp + hh
            sel = (lane // NA_HEAD_DIM) == hh
            qm = jnp.where(sel, q2, jnp.zeros_like(q2))
            s = []
            for j in range(4):
                bias = jnp.concatenate([bias_ref[0, head, kt * j + jj] for jj in range(kt)], axis=1)
                s.append(_dot_nt(qm, ks[j][:, cols]) + bias.astype(F32))
            s.append(_dot_nt(qm, kx_ref[:, cols]))
            m = s[0]
            for t in s[1:-1]:
                m = jnp.maximum(m, t)
            m = jnp.maximum(jnp.max(m, axis=-1, keepdims=True), jnp.max(s[-1], axis=-1, keepdims=True))
            den = jnp.zeros_like(m)
            acc = jnp.zeros((q2.shape[0], LANES), F32)
            vals = vs + (vx_ref,)
            for j in range(5):
                p = jnp.exp2(s[j] - m)
                den = den + jnp.sum(p, axis=-1, keepdims=True)
                acc = acc + _dot(p.astype(BF16), vals[j][:, cols])
            outs.append(acc / den)
        o_ref[:, cols] = jnp.where((lane // NA_HEAD_DIM) == 0, outs[0], outs[1]).astype(o_ref.dtype)


def _na_bias_table(rpb, rows):
    h = rpb.shape[0]
    nqb = rows // NA_QROWS
    row_sel = np.zeros((3, NA_QROWS, NA_KROWS, 2 * NA_WIN_ROWS - 1), np.float32)
    row_ok = np.zeros((3, NA_QROWS, NA_KROWS), bool)
    for v, qb in enumerate((0, min(1, nqb - 1), nqb - 1)):
        start = int(np.clip(NA_QROWS * qb - NA_WIN_ROWS // 2, 0, rows - NA_KROWS))
        for rq in range(NA_QROWS):
            r = NA_QROWS * qb + rq
            r0 = int(np.clip(r - NA_WIN_ROWS // 2, 0, rows - NA_WIN_ROWS))
            for rk in range(NA_KROWS):
                kr = start + rk
                if r0 <= kr < r0 + NA_WIN_ROWS:
                    row_ok[v, rq, rk] = True
                    row_sel[v, rq, rk, kr - r + NA_WIN_ROWS - 1] = 1.0
    col_sel = np.zeros((GRID_W, GRID_W, 2 * NA_WIN_COLS - 1), np.float32)
    col_ok = np.zeros((GRID_W, GRID_W), bool)
    for c in range(GRID_W):
        c0 = int(np.clip(c - NA_WIN_COLS // 2, 0, GRID_W - NA_WIN_COLS))
        for kc in range(c0, c0 + NA_WIN_COLS):
            col_ok[c, kc] = True
            col_sel[c, kc, kc - c + NA_WIN_COLS - 1] = 1.0
    g = LANES // GRID_W
    ncol = 2 * NA_WIN_COLS - 1
    col_sel2 = np.zeros((GRID_W, g * ncol, g * GRID_W), np.float32)
    for kk in range(g):
        col_sel2[:, kk * ncol:(kk + 1) * ncol, kk * GRID_W:(kk + 1) * GRID_W] = col_sel.transpose(0, 2, 1)
    t = jnp.einsum('vqkr,hrc->vhqkc', jnp.asarray(row_sel), rpb.astype(F32), precision=HIGHEST)
    t = t.reshape(3, h, NA_QROWS, NA_KROWS // g, g * ncol)
    t = jnp.einsum('vhqpe,xez->vhpqxz', t, jnp.asarray(col_sel2), precision=HIGHEST)
    ok = (row_ok.reshape(3, NA_QROWS, NA_KROWS // g, g).transpose(0, 2, 1, 3)[:, :, :, None, :, None]
          & col_ok[None, None, None, :, None, :])
    ok = ok.reshape(3, 1, NA_KROWS // g, NA_QROWS, GRID_W, LANES)
    t = jnp.where(jnp.asarray(ok), t * LOG2E, NEG_BIG)
    return t.reshape(3, h, NA_KROWS // g, NA_QROWS * GRID_W, LANES).astype(BF16)


def _na_attention(naq, nak, nav, cx_nak, cx_nav, bias, batch, n, ctx_len):
    rows = n // GRID_W
    qn = NA_QROWS * GRID_W
    kw = NA_KROWS * GRID_W // 4
    nqb = rows // NA_QROWS
    kblocks = n // kw
    width = NA_PAIRS_PER_STEP * LANES
    heads_per_step = width // NA_HEAD_DIM

    def kmap(j):
        def f(hp, qb, b):
            start = jnp.clip(2 * qb - 1, 0, kblocks - 4)
            return (b * kblocks + start + j, hp)
        return f

    def bias_map(hp, qb, b):
        v = jnp.where(qb == 0, 0, jnp.where(qb == nqb - 1, 2, 1))
        return (v, hp, 0, 0, 0)

    kv_specs = [pl.BlockSpec((kw, width), kmap(j)) for j in range(4)]
    return pl.pallas_call(
        _na_kernel,
        grid=(NA_WIDTH // width, nqb, batch),
        in_specs=[pl.BlockSpec((qn, width), lambda hp, qb, b: (b * nqb + qb, hp))]
                 + kv_specs + kv_specs
                 + [pl.BlockSpec((ctx_len, width), lambda hp, qb, b: (b, hp)),
                    pl.BlockSpec((ctx_len, width), lambda hp, qb, b: (b, hp)),
                    pl.BlockSpec((1, heads_per_step, 4 * kw // LANES, qn, LANES), bias_map)],
        out_specs=pl.BlockSpec((qn, width), lambda hp, qb, b: (b * nqb + qb, hp)),
        out_shape=jax.ShapeDtypeStruct((batch * n, NA_WIDTH), BF16),
        compiler_params=_cparams("arbitrary", "arbitrary", "arbitrary"),
        name="na_attention",
    )(naq, nak, nak, nak, nak, nav, nav, nav, nav, cx_nak, cx_nav, bias)


def _ret_kernel(reverse, lg_ref, q_ref, k_ref, v_ref, kx_ref, vx_ref, *rest):
    if reverse:
        of_ref, gate_ref, o_ref, s_ref, d_ref, xi_ref, zeta_ref = rest
    else:
        o_ref, s_ref, d_ref, xi_ref, zeta_ref = rest
    c = q_ref.shape[0]
    first = (pl.program_id(0) == 0) & (pl.program_id(1) == 0)
    lane = lax.broadcasted_iota(jnp.int32, (1, LANES), 1)

    @pl.when(first)
    def _():
        i = lax.broadcasted_iota(jnp.int32, (c, c), 0)
        j = lax.broadcasted_iota(jnp.int32, (c, c), 1)
        diff = (j - i) if reverse else (i - j)
        difff = jnp.maximum(diff, 0).astype(F32)
        r = lax.broadcasted_iota(jnp.int32, (c, LANES), 0).astype(F32)
        for h in range(RET_HEADS):
            lg = lg_ref[h]
            d_ref[h] = jnp.where(diff >= 0, jnp.exp(difff * lg), 0.0)
            xi_ref[h] = jnp.exp(((c - r) if reverse else (r + 1.0)) * lg)
        for p in range(RET_HEADS // 2):
            lgp = jnp.where(lane < RET_DK, lg_ref[2 * p], lg_ref[2 * p + 1])
            zeta_ref[p] = jnp.exp((r if reverse else (c - 1.0 - r)) * lgp)

    @pl.when(pl.program_id(1) == 0)
    def _():
        l = kx_ref.shape[0]
        r = lax.broadcasted_iota(jnp.int32, (l, LANES), 0).astype(F32)
        for p in range(RET_HEADS // 2):
            lgp = jnp.where(lane < RET_DK, lg_ref[2 * p], lg_ref[2 * p + 1])
            zc = jnp.exp((r if reverse else (l - 1.0 - r)) * lgp)
            kz = kx_ref[:, p * LANES:(p + 1) * LANES].astype(F32) * zc
            for hh in range(2):
                h = 2 * p + hh
                km = jnp.where((lane // RET_DK) == hh, kz, 0.0).astype(BF16)
                s_ref[h] = _dot_tn(km, vx_ref[:, h * RET_DV:(h + 1) * RET_DV])

    outs = []
    for p in range(RET_HEADS // 2):
        q2 = q_ref[:, p * LANES:(p + 1) * LANES]
        k2 = k_ref[:, p * LANES:(p + 1) * LANES]
        kz = k2.astype(F32) * zeta_ref[p]
        for hh in range(2):
            h = 2 * p + hh
            sel = (lane // RET_DK) == hh
            qm = jnp.where(sel, q2, jnp.zeros_like(q2))
            km = jnp.where(sel, k2, jnp.zeros_like(k2))
            vh = v_ref[:, h * RET_DV:(h + 1) * RET_DV]
            inner = (_dot_nt(qm, km) * d_ref[h]).astype(BF16)
            state = s_ref[h]
            o = _dot(inner, vh) + _dot(qm, state.astype(BF16)) * xi_ref[h]
            g_chunk = jnp.exp(jnp.full((1, LANES), c, F32) * lg_ref[h])
            s_ref[h] = state * g_chunk + _dot_tn(jnp.where(sel, kz, 0.0).astype(BF16), vh)
            outs.append(o)
    if reverse:
        for h in range(RET_HEADS):
            tot = outs[h] + of_ref[:, h * RET_DV:(h + 1) * RET_DV]
            ms = jnp.mean(tot * tot, axis=-1, keepdims=True)
            gate = gate_ref[:, h * RET_DV:(h + 1) * RET_DV].astype(F32)
            o_ref[:, h * RET_DV:(h + 1) * RET_DV] = (tot * lax.rsqrt(ms + NORM_EPS) * gate).astype(o_ref.dtype)
    else:
        for h in range(RET_HEADS):
            o_ref[:, h * RET_DV:(h + 1) * RET_DV] = outs[h]


def _retention_pass(reverse, log_g, rq, rk, rv, cx_rk, cx_rv, batch, n, ctx_len, o_fwd=None, gate=None):
    c = min(RET_CHUNK, n)
    nc = n // c
    if reverse:
        seq = lambda b, i: (b * nc + nc - 1 - i, 0)
    else:
        seq = lambda b, i: (b * nc + i, 0)
    ctx = lambda b, i: (b, 0)
    in_specs = [pl.BlockSpec(memory_space=pltpu.SMEM),
                pl.BlockSpec((c, RET_QK_WIDTH), seq),
                pl.BlockSpec((c, RET_QK_WIDTH), seq),
                pl.BlockSpec((c, RET_V_WIDTH), seq),
                pl.BlockSpec((ctx_len, RET_QK_WIDTH), ctx),
                pl.BlockSpec((ctx_len, RET_V_WIDTH), ctx)]
    args = [log_g, rq, rk, rv, cx_rk, cx_rv]
    if reverse:
        in_specs += [pl.BlockSpec((c, RET_V_WIDTH), seq), pl.BlockSpec((c, RET_V_WIDTH), seq)]
        args += [o_fwd, gate]
    return pl.pallas_call(
        functools.partial(_ret_kernel, reverse),
        grid=(batch, nc),
        in_specs=in_specs,
        out_specs=pl.BlockSpec((c, RET_V_WIDTH), seq),
        out_shape=jax.ShapeDtypeStruct((batch * n, RET_V_WIDTH), BF16 if reverse else F32),
        scratch_shapes=[pltpu.VMEM((RET_HEADS, LANES, RET_DV), F32),
                        pltpu.VMEM((RET_HEADS, c, c), F32),
                        pltpu.VMEM((RET_HEADS, c, RET_DV), F32),
                        pltpu.VMEM((RET_HEADS // 2, c, LANES), F32)],
        compiler_params=_cparams("arbitrary", "arbitrary"),
        name="retention_bwd" if reverse else "retention_fwd",
    )(*args)


def _outproj_kernel(x_ref, ona_ref, oret_ref, w_ref, mod_ref, gain_ref, x1_ref, h_ref):
    mix = _dot(ona_ref[...], w_ref[0:NA_WIDTH, :]) + _dot(oret_ref[...], w_ref[NA_WIDTH:, :])
    x1 = x_ref[...] + mod_ref[0, 2:3, :] * mix
    x1_ref[...] = x1
    h_ref[...] = _rms_mod(x1, gain_ref[...], mod_ref[0, 3:4, :], mod_ref[0, 4:5, :]).astype(BF16)


def _outproj(x2d, ona, oret, w_out_bf, mods, gain, tm, tiles_per_batch):
    t, d = x2d.shape
    row = lambda i: (i, 0)
    const = lambda i: (0, 0)
    return pl.pallas_call(
        _outproj_kernel,
        grid=(t // tm,),
        in_specs=[pl.BlockSpec((tm, d), row),
                  pl.BlockSpec((tm, NA_WIDTH), row),
                  pl.BlockSpec((tm, RET_V_WIDTH), row),
                  pl.BlockSpec((NA_WIDTH + RET_V_WIDTH, d), const),
                  pl.BlockSpec((1, 6, d), lambda i: (i // tiles_per_batch, 0, 0)),
                  pl.BlockSpec((1, d), const)],
        out_specs=[pl.BlockSpec((tm, d), row), pl.BlockSpec((tm, d), row)],
        out_shape=[jax.ShapeDtypeStruct((t, d), F32), jax.ShapeDtypeStruct((t, d), BF16)],
        compiler_params=_cparams("arbitrary"),
        name="outproj",
    )(x2d, ona, oret, w_out_bf, mods, gain)


def _swiglu_kernel(h_ref, x1_ref, wg_ref, wu_ref, wd_ref, mod_ref, o_ref, acc_ref):
    j = pl.program_id(1)

    @pl.when(j == 0)
    def _():
        acc_ref[...] = jnp.zeros_like(acc_ref)

    h = h_ref[...]
    g = _dot(h, wg_ref[...])
    u = _dot(h, wu_ref[...])
    a = (g * _sigmoid(g) * u).astype(BF16)
    acc_ref[...] += _dot(a, wd_ref[...])

    @pl.when(j == pl.num_programs(1) - 1)
    def _():
        o_ref[...] = x1_ref[...] + mod_ref[0, 5:6, :] * acc_ref[...]


def _swiglu(h_bf, x1, w_gate_up_bf, w_down_bf, mods, tm, tiles_per_batch):
    t, d = x1.shape
    f = w_down_bf.shape[0]
    nf = 2 if (f // 2) % LANES == 0 else 1
    tf = f // nf
    return pl.pallas_call(
        _swiglu_kernel,
        grid=(t // tm, nf),
        in_specs=[pl.BlockSpec((tm, d), lambda i, j: (i, 0)),
                  pl.BlockSpec((tm, d), lambda i, j: (i, 0)),
                  pl.BlockSpec((d, tf), lambda i, j: (0, j)),
                  pl.BlockSpec((d, tf), lambda i, j: (0, nf + j)),
                  pl.BlockSpec((tf, d), lambda i, j: (j, 0)),
                  pl.BlockSpec((1, 6, d), lambda i, j: (i // tiles_per_batch, 0, 0))],
        out_specs=pl.BlockSpec((tm, d), lambda i, j: (i, 0)),
        out_shape=jax.ShapeDtypeStruct((t, d), F32),
        scratch_shapes=[pltpu.VMEM((tm, d), F32)],
        compiler_params=_cparams("arbitrary", "arbitrary"),
        name="swiglu",
    )(h_bf, x1, w_gate_up_bf, w_gate_up_bf, w_down_bf, mods)


def _dft_tables(n, d):
    n2 = FFT_N2
    n1 = n // n2
    s = SUBLANES
    eye = np.eye(s)
    ang1 = 2.0 * np.pi * (np.outer(np.arange(n1), np.arange(n1)) % n1) / n1
    f1 = np.stack([np.cos(ang1), -np.sin(ang1)]) / np.sqrt(n)
    f1k = np.einsum('ckn,jl->ckjnl', f1, eye).reshape(2 * n1 * s, n1 * s)
    angt = 2.0 * np.pi * (np.outer(np.arange(n1), np.arange(n2)) % n) / n
    tw = np.stack([np.cos(angt), -np.sin(angt)])
    tw = tw.reshape(2, n1, n2 // s, s).transpose(0, 2, 1, 3).reshape(2, n2 // s, n1 * s, 1)
    tw = np.broadcast_to(tw, (2, n2 // s, n1 * s, LANES))
    ang2 = 2.0 * np.pi * (np.outer(np.arange(n2), np.arange(n2)) % n2) / n2
    cr, ci = np.cos(ang2), -np.sin(ang2)
    m2 = np.array([[cr, -ci], [ci, cr]])
    m2k = np.einsum('abkn,jl->akjbln', m2, eye).reshape(2 * n2 * s, 2 * s * n2)
    gw = d // FOURIER_GROUPS
    angc = 2.0 * np.pi * (np.outer(np.arange(gw), np.arange(gw)) % gw) / gw
    cs = np.concatenate([np.cos(angc), np.sin(angc)], axis=0) / np.sqrt(gw)
    return (f1k.astype(np.float32), tw.astype(np.float32), m2k.astype(np.float32),
            cs.astype(np.float32))


def _fold_kernel(cs_ref, w_ref, o_ref):
    gw = w_ref.shape[0]
    o_ref[0] = jnp.dot(cs_ref[0:gw, :], w_ref[...], precision=HIGHEST, preferred_element_type=F32)
    o_ref[1] = jnp.dot(cs_ref[gw:, :], w_ref[...], precision=HIGHEST, preferred_element_type=F32)


def _fold_channel_dft(cs, w_fourier):
    d = w_fourier.shape[0]
    gw = d // FOURIER_GROUPS
    return pl.pallas_call(
        _fold_kernel,
        grid=(FOURIER_GROUPS,),
        in_specs=[pl.BlockSpec((2 * gw, gw), lambda g: (0, 0)),
                  pl.BlockSpec((gw, d), lambda g: (g, 0))],
        out_specs=pl.BlockSpec((2, gw, d), lambda g: (0, g, 0)),
        out_shape=jax.ShapeDtypeStruct((2, d, d), F32),
        compiler_params=_cparams("arbitrary"),
        name="fold_channel_dft",
    )(cs, w_fourier)


def _fft1_kernel(x_ref, mod_ref, gain_ref, f_ref, tw_ref, o_ref):
    _, n1, s, d = x_ref.shape
    x = x_ref[...].reshape(n1 * s, d)
    h = _rms_mod(x, gain_ref[...], mod_ref[0, 0:1, :], mod_ref[0, 1:2, :]).astype(BF16)
    a = _dot(f_ref[...], h)
    ar = a[:n1 * s]
    ai = a[n1 * s:]
    reps = d // LANES
    twr = jnp.tile(tw_ref[0, 0], (1, reps))
    twi = jnp.tile(tw_ref[1, 0], (1, reps))
    o_ref[0, 0] = (ar * twr - ai * twi).reshape(n1, s, d)
    o_ref[0, 1] = (ar * twi + ai * twr).reshape(n1, s, d)


def _fft_stage1(x2, mods, gain, f1k_bf, tw, batch, n):
    d = x2.shape[-1]
    n2 = FFT_N2
    n1 = n // n2
    s = SUBLANES
    xv = x2.reshape(batch, n1, n2, d)
    return pl.pallas_call(
        _fft1_kernel,
        grid=(n2 // s, batch),
        in_specs=[pl.BlockSpec((1, n1, s, d), lambda g, b: (b, 0, g, 0)),
                  pl.BlockSpec((1, 6, d), lambda g, b: (b, 0, 0)),
                  pl.BlockSpec((1, d), lambda g, b: (0, 0)),
                  pl.BlockSpec((2 * n1 * s, n1 * s), lambda g, b: (0, 0)),
                  pl.BlockSpec((2, 1, n1 * s, LANES), lambda g, b: (0, g, 0, 0))],
        out_specs=pl.BlockSpec((1, 2, n1, s, d), lambda g, b: (b, 0, 0, g, 0)),
        out_shape=jax.ShapeDtypeStruct((batch, 2, n1, n2, d), F32),
        compiler_params=_cparams("arbitrary", "arbitrary"),
        name="fft_stage1",
    )(xv, mods, gain, f1k_bf, tw)


def _fft2_kernel(n_experts, b_ref, x_ref, m_ref, wf_ref, mod_ref, gain_ref, wr_ref,
                 x3_ref, h2_ref, route_ref):
    _, _, s, n2, d = b_ref.shape
    rows = n2 * s
    bmat = b_ref[...].reshape(2 * s * n2, d).astype(BF16)
    z = _dot(m_ref[...], bmat)
    y = _dot(z[:rows].astype(BF16), wf_ref[0]) + _dot(z[rows:].astype(BF16), wf_ref[1])
    x3 = x_ref[...].reshape(rows, d) + mod_ref[0, 2:3, :] * y
    x3_ref[...] = x3.reshape(x3_ref.shape)
    h2 = _rms_mod(x3, gain_ref[...], mod_ref[0, 3:4, :], mod_ref[0, 4:5, :])
    h2_ref[...] = h2.reshape(rows, d // LANES, LANES).reshape(h2_ref.shape)
    h_hi = h2.astype(BF16)
    h_lo = (h2 - h_hi.astype(F32)).astype(BF16)
    wr = wr_ref[...]
    w_hi = wr.astype(BF16)
    w_lo = (wr - w_hi.astype(F32)).astype(BF16)
    logits = _dot(h_hi, w_hi) + (_dot(h_lo, w_hi) + _dot(h_hi, w_lo))
    lane = lax.broadcasted_iota(jnp.int32, logits.shape, 1).astype(F32)
    lg = jnp.where(lane < n_experts, logits, -jnp.inf)
    m1 = jnp.max(lg, axis=-1, keepdims=True)
    i1 = jnp.min(jnp.where(lg == m1, lane, float(LANES)), axis=-1, keepdims=True)
    lg2 = jnp.where(lane == i1, -jnp.inf, lg)
    m2 = jnp.max(lg2, axis=-1, keepdims=True)
    i2 = jnp.min(jnp.where(lg2 == m2, lane, float(LANES)), axis=-1, keepdims=True)
    e = jnp.exp(m2 - m1)
    g1 = 1.0 / (1.0 + e)
    g2 = e / (1.0 + e)
    route = jnp.where(lane == 0, g1, jnp.where(lane == 1, g2, jnp.where(lane == 2, i1,
                                                                       jnp.where(lane == 3, i2, 0.0))))
    route_ref[...] = route.reshape(route_ref.shape)


def _fft_stage2(bmid, x2, m2k_bf, wf_bf, mods, gain, wr_pad, n_experts, batch, n):
    d = x2.shape[-1]
    n2 = FFT_N2
    n1 = n // n2
    s = SUBLANES
    xv = x2.reshape(batch, n2, n1, d)
    pos = lambda g, b: (b, 0, g, 0)
    const2 = lambda g, b: (0, 0)
    x3, h2, route = pl.pallas_call(
        functools.partial(_fft2_kernel, n_experts),
        grid=(n1 // s, batch),
        in_specs=[pl.BlockSpec((1, 2, s, n2, d), lambda g, b: (b, 0, g, 0, 0)),
                  pl.BlockSpec((1, n2, s, d), pos),
                  pl.BlockSpec((2 * n2 * s, 2 * n2 * s), const2),
                  pl.BlockSpec((2, d, d), lambda g, b: (0, 0, 0)),
                  pl.BlockSpec((1, 6, d), lambda g, b: (b, 0, 0)),
                  pl.BlockSpec((1, d), const2),
                  pl.BlockSpec((d, LANES), const2)],
        out_specs=[pl.BlockSpec((1, n2, s, d), pos),
                   pl.BlockSpec((1, n2, s, d // LANES, LANES), lambda g, b: (b, 0, g, 0, 0)),
                   pl.BlockSpec((1, n2, s, LANES), pos)],
        out_shape=[jax.ShapeDtypeStruct((batch, n2, n1, d), F32),
                   jax.ShapeDtypeStruct((batch, n2, n1, d // LANES, LANES), F32),
                   jax.ShapeDtypeStruct((batch, n2, n1, LANES), F32)],
        compiler_params=_cparams("arbitrary", "arbitrary"),
        name="fft_stage2",
    )(bmid, xv, m2k_bf, wf_bf, mods, gain, wr_pad)
    t = batch * n
    return x3.reshape(t, d), h2.reshape(t, d // LANES, LANES), route.reshape(t, LANES)


def _dispatch_kernel(tm, slot_ref, pad_end_ref, h_ref, dst_ref, zero_ref, sem, zsem):
    base = pl.program_id(0) * tm * TOP_K

    @pl.when(pl.program_id(0) == 0)
    def _():
        zero_ref[...] = jnp.zeros_like(zero_ref)
        n_experts = pad_end_ref.shape[0]
        total = pad_end_ref[n_experts - 1]
        starts = []
        for e in range(n_experts):
            prev = pad_end_ref[e - 1] if e > 0 else 0
            starts.append((pad_end_ref[e] > prev, pad_end_ref[e] - MOE_BLOCK))
        for q in range(n_experts):
            starts.append((total + q * MOE_BLOCK < dst_ref.shape[0], total + q * MOE_BLOCK))
        for needed, start in starts:
            @pl.when(needed)
            def _():
                copies = [pltpu.make_async_copy(zero_ref, dst_ref.at[pl.ds(start + q * tm, tm)], zsem)
                          for q in range(MOE_BLOCK // tm)]
                for cp in copies:
                    cp.start()
                for cp in copies:
                    cp.wait()

    def body(r, carry):
        for k in range(TOP_K):
            s = slot_ref[base + TOP_K * r + k]
            pltpu.make_async_copy(h_ref.at[pl.ds(r, 1)], dst_ref.at[pl.ds(s, 1)], sem).start(priority=k % 2)
        return carry

    lax.fori_loop(0, tm, body, 0, unroll=8)

    def drain(r, carry):
        pltpu.make_async_copy(h_ref.at[pl.ds(0, 1)], dst_ref.at[pl.ds(0, 1)], sem).wait()
        return carry

    lax.fori_loop(0, tm * TOP_K, drain, 0, unroll=8)


def _dispatch(h2, slot, pad_end, n_slots, tm):
    t = h2.shape[0]
    tile = h2.shape[1:]
    assert MOE_BLOCK % tm == 0
    return pl.pallas_call(
        functools.partial(_dispatch_kernel, tm),
        grid_spec=pltpu.PrefetchScalarGridSpec(
            num_scalar_prefetch=2,
            grid=(t // tm,),
            in_specs=[pl.BlockSpec((tm,) + tile, lambda i, sl, pe: (i, 0, 0))],
            out_specs=pl.BlockSpec(memory_space=pl.ANY),
            scratch_shapes=[pltpu.VMEM((tm,) + tile, h2.dtype),
                            pltpu.SemaphoreType.DMA(()), pltpu.SemaphoreType.DMA(())]),
        out_shape=jax.ShapeDtypeStruct((n_slots,) + tile, h2.dtype),
        compiler_params=pltpu.CompilerParams(dimension_semantics=("arbitrary",), has_side_effects=True,
                                             vmem_limit_bytes=VMEM_LIMIT),
        name="dispatch",
    )(slot, pad_end, h2)


def _expert_kernel(bexp_ref, nused_ref, x_ref, wg_ref, wu_ref, wd_ref, o_ref, xb_ref, acc_ref):
    i = pl.program_id(0)
    j = pl.program_id(1)
    last = pl.num_programs(1) - 1
    used = i < nused_ref[0]

    @pl.when(used)
    def _():
        @pl.when(j == 0)
        def _():
            xb_ref[...] = x_ref[...].reshape(xb_ref.shape).astype(BF16)
            acc_ref[...] = jnp.zeros_like(acc_ref)

        xb = xb_ref[...]
        g = _dot(xb, wg_ref[0])
        u = _dot(xb, wu_ref[0])
        a = (g * _sigmoid(g) * u).astype(BF16)
        acc_ref[...] += _dot(a, wd_ref[0])

        @pl.when(j == last)
        def _():
            o_ref[...] = acc_ref[...].reshape(o_ref.shape)

    @pl.when(jnp.logical_not(used) & (j == last))
    def _():
        o_ref[...] = jnp.zeros_like(o_ref)


def _expert_ffn(xs, w_gu_bf, w_dn_bf, block_expert, n_used):
    n_slots, sub, _ = xs.shape
    e, f, d = w_dn_bf.shape
    nb = n_slots // MOE_BLOCK
    nf = EXPERT_F_TILES if (f // EXPERT_F_TILES) % LANES == 0 else 1
    tf = f // nf

    def xmap(i, j, be, nu):
        return (jnp.minimum(i, nu[0] - 1), 0, 0)

    def jeff(i, j, nu):
        return jnp.where(i < nu[0], j, nf - 1)

    return pl.pallas_call(
        _expert_kernel,
        grid_spec=pltpu.PrefetchScalarGridSpec(
            num_scalar_prefetch=2,
            grid=(nb, nf),
            in_specs=[pl.BlockSpec((MOE_BLOCK, sub, LANES), xmap),
                      pl.BlockSpec((1, d, tf), lambda i, j, be, nu: (be[i], 0, jeff(i, j, nu))),
                      pl.BlockSpec((1, d, tf), lambda i, j, be, nu: (be[i], 0, nf + jeff(i, j, nu))),
                      pl.BlockSpec((1, tf, d), lambda i, j, be, nu: (be[i], jeff(i, j, nu), 0))],
            out_specs=pl.BlockSpec((MOE_BLOCK, sub, LANES), lambda i, j, be, nu: (i, 0, 0)),
            scratch_shapes=[pltpu.VMEM((MOE_BLOCK, d), BF16), pltpu.VMEM((MOE_BLOCK, d), F32)]),
        out_shape=jax.ShapeDtypeStruct((n_slots, sub, LANES), F32),
        compiler_params=pltpu.CompilerParams(dimension_semantics=("arbitrary", "arbitrary"),
                                             vmem_limit_bytes=EXPERT_VMEM_LIMIT),
        name="expert_ffn",
    )(block_expert, n_used, xs, w_gu_bf, w_gu_bf, w_dn_bf)


def _combine_kernel(tm, slot_ref, x_ref, route_ref, mod_ref, ys_ref, o_ref, ybuf, sem):
    i = pl.program_id(0)

    def issue(step, b):
        base = step * (tm * TOP_K)

        def body(r, carry):
            for k in range(TOP_K):
                s = slot_ref[base + TOP_K * r + k]
                pltpu.make_async_copy(ys_ref.at[pl.ds(s, 1)], ybuf.at[b, k, pl.ds(r, 1)],
                                      sem.at[b]).start(priority=k % 2)
            return carry

        lax.fori_loop(0, tm, body, 0, unroll=8)

    @pl.when(i == 0)
    def _():
        issue(0, 0)

    @pl.when(i + 1 < pl.num_programs(0))
    def _():
        issue(i + 1, (i + 1) % 2)

    b = i % 2

    def drain(r, carry):
        pltpu.make_async_copy(ys_ref.at[pl.ds(0, 1)], ybuf.at[b, 0, pl.ds(0, 1)], sem.at[b]).wait()
        return carry

    lax.fori_loop(0, tm * TOP_K, drain, 0, unroll=8)
    y0 = ybuf[b, 0].reshape(x_ref.shape)
    y1 = ybuf[b, 1].reshape(x_ref.shape)
    g0 = route_ref[:, 0:1]
    g1 = route_ref[:, 1:2]
    o_ref[...] = x_ref[...] + mod_ref[0, 5:6, :] * (g0 * y0 + g1 * y1)


def _combine(x3, ys, slot, route, mods, tm, tiles_per_batch):
    t, d = x3.shape
    tile = ys.shape[1:]
    row = lambda i, sl: (i, 0)
    return pl.pallas_call(
        functools.partial(_combine_kernel, tm),
        grid_spec=pltpu.PrefetchScalarGridSpec(
            num_scalar_prefetch=1,
            grid=(t // tm,),
            in_specs=[pl.BlockSpec((tm, d), row),
                      pl.BlockSpec((tm, LANES), row),
                      pl.BlockSpec((1, 6, d), lambda i, sl: (i // tiles_per_batch, 0, 0)),
                      pl.BlockSpec(memory_space=pl.ANY)],
            out_specs=pl.BlockSpec((tm, d), row),
            scratch_shapes=[pltpu.VMEM((2, TOP_K, tm) + tile, F32), pltpu.SemaphoreType.DMA((2,))]),
        out_shape=jax.ShapeDtypeStruct((t, d), F32),
        compiler_params=_cparams("arbitrary"),
        name="combine",
    )(slot, x3, route, mods, ys)


def _route_slots(route, n_experts):
    t = route.shape[0]
    expert = route[:, 2:2 + TOP_K].astype(jnp.int32).reshape(t * TOP_K)
    onehot = (expert[:, None] == jnp.arange(n_experts, dtype=jnp.int32)[None, :]).astype(jnp.int32)
    csum = jnp.cumsum(onehot, axis=0)
    counts = csum[-1]
    padded = (counts + MOE_BLOCK - 1) // MOE_BLOCK * MOE_BLOCK
    pad_end = jnp.cumsum(padded)
    pad_start = pad_end - padded
    slot = jnp.sum(onehot * (pad_start[None, :] + csum - 1), axis=1).astype(jnp.int32)
    n_slots = t * TOP_K + n_experts * MOE_BLOCK
    nb = n_slots // MOE_BLOCK
    block_expert = jnp.minimum(
        jnp.searchsorted(pad_end, jnp.arange(nb, dtype=jnp.int32) * MOE_BLOCK, side='right'),
        n_experts - 1).astype(jnp.int32)
    n_used = (pad_end[-1:] // MOE_BLOCK).astype(jnp.int32)
    return slot, block_expert, n_used, pad_end.astype(jnp.int32), n_slots


def _rope_tables(n):
    pos = np.arange(n)
    row = (pos // GRID_W).astype(np.float32)
    col = (pos % GRID_W).astype(np.float32)
    n_freq = RET_DK // 4
    inv_freq = (np.float32(ROPE_BASE) ** (-np.arange(n_freq, dtype=np.float32) / n_freq)).astype(np.float32)
    ang = np.concatenate([row[:, None] * inv_freq, col[:, None] * inv_freq], axis=-1)
    cos = np.cos(ang.astype(np.float64))
    sin = np.sin(ang.astype(np.float64))
    cos_h = np.repeat(cos, 2, axis=-1)
    sin_h = np.stack([-sin, sin], axis=-1).reshape(n, RET_DK)
    return (np.tile(cos_h, (1, RET_HEADS)).astype(np.float32),
            np.tile(sin_h, (1, RET_HEADS)).astype(np.float32))


def kernel(x, c, ctx, c_ctx, l0_norm_mix, l0_norm_ffn, l0_w_ada, l0_b_ada, l0_w_in, l0_q_gain, l0_k_gain, l0_rpb, l0_ret_decay_fwd, l0_ret_decay_bwd, l0_w_out, l0_w_gate_up, l0_w_down, l1_norm_mix, l1_norm_ffn, l1_w_ada, l1_b_ada, l1_w_fourier, l1_w_router, l1_w_exp_gate_up, l1_w_exp_down):
    batch, n, d = x.shape
    ctx_len = ctx.shape[1]
    t = batch * n
    rows = n // GRID_W
    n_experts = l1_w_router.shape[1]
    assert n % (NA_QROWS * GRID_W) == 0 and rows >= NA_KROWS and n % FFT_N2 == 0
    assert l0_w_in.shape[1] == _IN_WIDTH and d % LANES == 0

    x2d = x.reshape(t, d)
    ctx2d = ctx.reshape(batch * ctx_len, d)

    ada0 = _ada(jnp.concatenate([c, c_ctx[None, :]], axis=0), l0_w_ada, l0_b_ada)
    mods0 = ada0[:batch].reshape(batch, 6, d)
    mods_ctx = ada0[batch:batch + 1].reshape(1, 6, d)
    mods1 = _ada(c, l1_w_ada, l1_b_ada)[:batch].reshape(batch, 6, d)

    w_in_bf = l0_w_in.astype(BF16)
    qg = jnp.tile(l0_q_gain.astype(F32), NA_HEADS).reshape(1, NA_WIDTH)
    kg = jnp.tile(l0_k_gain.astype(F32), NA_HEADS).reshape(1, NA_WIDTH)
    gmat = jnp.asarray(np.kron(np.eye(NA_HEADS), np.full((NA_HEAD_DIM, NA_HEAD_DIM), 1.0 / NA_HEAD_DIM)),
                       BF16)
    cos_np, sin_np = _rope_tables(n)
    gain_mix0 = l0_norm_mix.reshape(1, d)
    tm = min(INPROJ_ROWS, n)
    naq, rq, rg, nak, nav, rk, rv = _inproj(
        x2d, mods0, gain_mix0, w_in_bf, qg, kg, gmat, jnp.asarray(cos_np), jnp.asarray(sin_np),
        tm, n // tm, n // tm)
    ctx_rows = batch * ctx_len
    _, _, _, cx_nak, cx_nav, cx_rk, cx_rv = _inproj(
        ctx2d, mods_ctx, gain_mix0, w_in_bf, qg, kg, gmat,
        jnp.ones((ctx_len, RET_QK_WIDTH), F32), jnp.zeros((ctx_len, RET_QK_WIDTH), F32),
        ctx_len, ctx_rows // ctx_len, 1)

    bias = _na_bias_table(l0_rpb, rows)
    o_na = _na_attention(naq, nak, nav, cx_nak, cx_nav, bias, batch, n, ctx_len)

    lg_f = jax.nn.log_sigmoid(l0_ret_decay_fwd.astype(F32))
    lg_b = jax.nn.log_sigmoid(l0_ret_decay_bwd.astype(F32))
    o_f = _retention_pass(False, lg_f, rq, rk, rv, cx_rk, cx_rv, batch, n, ctx_len)
    o_ret = _retention_pass(True, lg_b, rq, rk, rv, cx_rk, cx_rv, batch, n, ctx_len, o_fwd=o_f, gate=rg)

    x1, h_ffn = _outproj(x2d, o_na, o_ret, l0_w_out.astype(BF16), mods0, l0_norm_ffn.reshape(1, d),
                         tm, n // tm)
    tf_rows = min(FFN_ROWS, n)
    x2 = _swiglu(h_ffn, x1, l0_w_gate_up.astype(BF16), l0_w_down.astype(BF16), mods0,
                 tf_rows, n // tf_rows)

    f1k, tw, m2k, cs = _dft_tables(n, d)
    wf = _fold_channel_dft(jnp.asarray(cs), l1_w_fourier).astype(BF16)
    bmid = _fft_stage1(x2, mods1, l1_norm_mix.reshape(1, d), jnp.asarray(f1k, BF16), jnp.asarray(tw),
                       batch, n)
    wr_pad = jnp.zeros((d, LANES), F32).at[:, :n_experts].set(l1_w_router)
    x3, h2, route = _fft_stage2(bmid, x2, jnp.asarray(m2k, BF16), wf, mods1, l1_norm_ffn.reshape(1, d),
                                wr_pad, n_experts, batch, n)

    slot, block_expert, n_used, pad_end, n_slots = _route_slots(route, n_experts)
    xs = _dispatch(h2, slot, pad_end, n_slots, tm)
    ys = _expert_ffn(xs, l1_w_exp_gate_up.astype(BF16), l1_w_exp_down.astype(BF16), block_expert, n_used)
    out = _combine(x3, ys, slot, route, mods1, tm, n // tm)
    return out.reshape(batch, n, d)
```

```python
import functools

import numpy as np
import jax
import jax.numpy as jnp
from jax import lax
from jax.experimental import pallas as pl
from jax.experimental.pallas import tpu as pltpu

F32 = jnp.float32
BF16 = jnp.bfloat16
HIGHEST = lax.Precision.HIGHEST

GRID_W = 64
NA_HEADS = 8
NA_HEAD_DIM = 64
NA_WIN_ROWS = 8
NA_WIN_COLS = 16
RET_HEADS = 4
RET_DK = 64
RET_DV = 128
ROPE_BASE = 10000.0
FOURIER_GROUPS = 8
TOP_K = 2
NORM_EPS = 1e-6

NA_WIDTH = NA_HEADS * NA_HEAD_DIM
RET_QK_WIDTH = RET_HEADS * RET_DK
RET_V_WIDTH = RET_HEADS * RET_DV

LANES = 128
SUBLANES = 8
VMEM_LIMIT = 48 * 1024 * 1024

NEG_BIG = -1e30
LOG2E = 1.4426950408889634

INPROJ_ROWS = 512
NA_QROWS = 8
NA_KROWS = 16
NA_PAIRS_PER_STEP = 4
RET_CHUNK = 512
FFN_ROWS = 512
FFN_F_TILES = 1
FFT_N2 = 64
MOE_BLOCK = 512
EXPERT_F_TILES = 1
EXPERT_VMEM_LIMIT = 56 * 1024 * 1024


def _cparams(*sem):
    return pltpu.CompilerParams(dimension_semantics=sem, vmem_limit_bytes=VMEM_LIMIT)


def _sigmoid(x):
    return 0.5 * jnp.tanh(0.5 * x) + 0.5


def _dot(a, b):
    return jnp.dot(a, b, preferred_element_type=F32)


def _dot_nt(a, b):
    return lax.dot_general(a, b, (((1,), (1,)), ((), ())), preferred_element_type=F32)


def _dot_tn(a, b):
    return lax.dot_general(a, b, (((0,), (0,)), ((), ())), preferred_element_type=F32)


def _rms_mod(x, gain, shift, scale):
    ms = jnp.mean(x * x, axis=-1, keepdims=True)
    return (x * lax.rsqrt(ms + NORM_EPS) * gain) * (1.0 + scale) + shift


def _ada_kernel(c_ref, w_ref, b_ref, o_ref):
    c = c_ref[...]
    s = c * _sigmoid(c)
    o_ref[...] = jnp.dot(s, w_ref[...], precision=HIGHEST, preferred_element_type=F32) + b_ref[...]


def _ada(cond, w_ada, b_ada):
    d, n = w_ada.shape
    cond8 = jnp.zeros((SUBLANES, d), F32).at[:cond.shape[0]].set(cond)
    tn = 1024
    return pl.pallas_call(
        _ada_kernel,
        grid=(n // tn,),
        in_specs=[pl.BlockSpec((SUBLANES, d), lambda j: (0, 0)),
                  pl.BlockSpec((d, tn), lambda j: (0, j)),
                  pl.BlockSpec((1, tn), lambda j: (0, j))],
        out_specs=pl.BlockSpec((SUBLANES, tn), lambda j: (0, j)),
        out_shape=jax.ShapeDtypeStruct((SUBLANES, n), F32),
        compiler_params=_cparams("arbitrary"),
        name="ada",
    )(cond8, w_ada, b_ada.reshape(1, n))


_O_NAQ = 0
_O_RQ = _O_NAQ + NA_WIDTH
_O_RG = _O_RQ + RET_QK_WIDTH
_O_NAK = _O_RG + RET_V_WIDTH
_O_NAV = _O_NAK + NA_WIDTH
_O_RK = _O_NAV + NA_WIDTH
_O_RV = _O_RK + RET_QK_WIDTH
_IN_WIDTH = _O_RV + RET_V_WIDTH


def _inproj_kernel(x_ref, mod_ref, gain_ref, w_ref, qg_ref, kg_ref, gmat_ref, cos_ref, sin_ref,
                   naq_ref, rq_ref, rg_ref, nak_ref, nav_ref, rk_ref, rv_ref):
    h = _rms_mod(x_ref[...], gain_ref[...], mod_ref[0, 0:1, :], mod_ref[0, 1:2, :]).astype(BF16)
    p = _dot(h, w_ref[...])
    gmat = gmat_ref[...]

    def headnorm(t, g):
        ms = _dot((t * t).astype(BF16), gmat)
        return t * lax.rsqrt(ms + NORM_EPS) * g

    cos = cos_ref[...]
    sin = sin_ref[...]
    lane = lax.broadcasted_iota(jnp.int32, (1, RET_QK_WIDTH), 1)
    even = (lane % 2) == 0

    def rope(t):
        partner = jnp.where(even, pltpu.roll(t, RET_QK_WIDTH - 1, axis=1), pltpu.roll(t, 1, axis=1))
        return t * cos + partner * sin

    naq_ref[...] = (headnorm(p[:, _O_NAQ:_O_RQ], qg_ref[...]) * (NA_HEAD_DIM ** -0.5 * LOG2E)).astype(BF16)
    rq_ref[...] = rope(p[:, _O_RQ:_O_RG]).astype(BF16)
    g = p[:, _O_RG:_O_NAK]
    rg_ref[...] = (g * _sigmoid(g)).astype(BF16)
    nak_ref[...] = headnorm(p[:, _O_NAK:_O_NAV], kg_ref[...]).astype(BF16)
    nav_ref[...] = p[:, _O_NAV:_O_RK].astype(BF16)
    rk_ref[...] = (rope(p[:, _O_RK:_O_RV]) * (RET_DK ** -0.5)).astype(BF16)
    rv_ref[...] = p[:, _O_RV:_IN_WIDTH].astype(BF16)


def _inproj(x2d, mods, gain, w_bf, qg, kg, gmat, cos, sin, tm, tiles_per_mod, pos_tiles):
    t, d = x2d.shape
    widths = (NA_WIDTH, RET_QK_WIDTH, RET_V_WIDTH, NA_WIDTH, NA_WIDTH, RET_QK_WIDTH, RET_V_WIDTH)
    row = lambda i: (i, 0)
    const = lambda i: (0, 0)
    return pl.pallas_call(
        _inproj_kernel,
        grid=(t // tm,),
        in_specs=[pl.BlockSpec((tm, d), row),
                  pl.BlockSpec((1, 6, d), lambda i: (i // tiles_per_mod, 0, 0)),
                  pl.BlockSpec((1, d), const),
                  pl.BlockSpec((d, _IN_WIDTH), const),
                  pl.BlockSpec((1, NA_WIDTH), const),
                  pl.BlockSpec((1, NA_WIDTH), const),
                  pl.BlockSpec((NA_WIDTH, NA_WIDTH), const),
                  pl.BlockSpec((tm, RET_QK_WIDTH), lambda i: (i % pos_tiles, 0)),
                  pl.BlockSpec((tm, RET_QK_WIDTH), lambda i: (i % pos_tiles, 0))],
        out_specs=[pl.BlockSpec((tm, w), row) for w in widths],
        out_shape=[jax.ShapeDtypeStruct((t, w), BF16) for w in widths],
        compiler_params=_cparams("arbitrary"),
        name="inproj",
    )(x2d, mods, gain, w_bf, qg, kg, gmat, cos, sin)


def _na_kernel(q_ref, k0, k1, k2, k3, v0, v1, v2, v3, kx_ref, vx_ref, bias_ref, o_ref):
    lane = lax.broadcasted_iota(jnp.int32, (1, LANES), 1)
    ks = (k0, k1, k2, k3)
    vs = (v0, v1, v2, v3)
    kt = k0.shape[0] // LANES
    for pp in range(q_ref.shape[1] // LANES):
        cols = slice(pp * LANES, (pp + 1) * LANES)
        q2 = q_ref[:, cols]
        outs = []
        for hh in range(2):
            head = 2 * pp + hh
            sel = (lane // NA_HEAD_DIM) == hh
            qm = jnp.where(sel, q2, jnp.zeros_like(q2))
            s = []
            for j in range(4):
                bias = jnp.concatenate([bias_ref[0, head, kt * j + jj] for jj in range(kt)], axis=1)
                s.append(_dot_nt(qm, ks[j][:, cols]) + bias.astype(F32))
            s.append(_dot_nt(qm, kx_ref[:, cols]))
            m = s[0]
            for t in s[1:-1]:
                m = jnp.maximum(m, t)
            m = jnp.maximum(jnp.max(m, axis=-1, keepdims=True), jnp.max(s[-1], axis=-1, keepdims=True))
            den = jnp.zeros_like(m)
            acc = jnp.zeros((q2.shape[0], LANES), F32)
            vals = vs + (vx_ref,)
            for j in range(5):
                p = jnp.exp2(s[j] - m)
                den = den + jnp.sum(p, axis=-1, keepdims=True)
                acc = acc + _dot(p.astype(BF16), vals[j][:, cols])
            outs.append(acc / den)
        o_ref[:, cols] = jnp.where((lane // NA_HEAD_DIM) == 0, outs[0], outs[1]).astype(o_ref.dtype)


def _na_bias_table(rpb, rows):
    h = rpb.shape[0]
    nqb = rows // NA_QROWS
    row_sel = np.zeros((3, NA_QROWS, NA_KROWS, 2 * NA_WIN_ROWS - 1), np.float32)
    row_ok = np.zeros((3, NA_QROWS, NA_KROWS), bool)
    for v, qb in enumerate((0, min(1, nqb - 1), nqb - 1)):
        start = int(np.clip(NA_QROWS * qb - NA_WIN_ROWS // 2, 0, rows - NA_KROWS))
        for rq in range(NA_QROWS):
            r = NA_QROWS * qb + rq
            r0 = int(np.clip(r - NA_WIN_ROWS // 2, 0, rows - NA_WIN_ROWS))
            for rk in range(NA_KROWS):
                kr = start + rk
                if r0 <= kr < r0 + NA_WIN_ROWS:
                    row_ok[v, rq, rk] = True
                    row_sel[v, rq, rk, kr - r + NA_WIN_ROWS - 1] = 1.0
    col_sel = np.zeros((GRID_W, GRID_W, 2 * NA_WIN_COLS - 1), np.float32)
    col_ok = np.zeros((GRID_W, GRID_W), bool)
    for c in range(GRID_W):
        c0 = int(np.clip(c - NA_WIN_COLS // 2, 0, GRID_W - NA_WIN_COLS))
        for kc in range(c0, c0 + NA_WIN_COLS):
            col_ok[c, kc] = True
            col_sel[c, kc, kc - c + NA_WIN_COLS - 1] = 1.0
    g = LANES // GRID_W
    ncol = 2 * NA_WIN_COLS - 1
    col_sel2 = np.zeros((GRID_W, g * ncol, g * GRID_W), np.float32)
    for kk in range(g):
        col_sel2[:, kk * ncol:(kk + 1) * ncol, kk * GRID_W:(kk + 1) * GRID_W] = col_sel.transpose(0, 2, 1)
    t = jnp.einsum('vqkr,hrc->vhqkc', jnp.asarray(row_sel), rpb.astype(F32), precision=HIGHEST)
    t = t.reshape(3, h, NA_QROWS, NA_KROWS // g, g * ncol)
    t = jnp.einsum('vhqpe,xez->vhpqxz', t, jnp.asarray(col_sel2), precision=HIGHEST)
    ok = (row_ok.reshape(3, NA_QROWS, NA_KROWS // g, g).transpose(0, 2, 1, 3)[:, :, :, None, :, None]
          & col_ok[None, None, None, :, None, :])
    ok = ok.reshape(3, 1, NA_KROWS // g, NA_QROWS, GRID_W, LANES)
    t = jnp.where(jnp.asarray(ok), t * LOG2E, NEG_BIG)
    return t.reshape(3, h, NA_KROWS // g, NA_QROWS * GRID_W, LANES).astype(BF16)


def _na_attention(naq, nak, nav, cx_nak, cx_nav, bias, batch, n, ctx_len):
    rows = n // GRID_W
    qn = NA_QROWS * GRID_W
    kw = NA_KROWS * GRID_W // 4
    nqb = rows // NA_QROWS
    kblocks = n // kw
    width = NA_PAIRS_PER_STEP * LANES
    heads_per_step = width // NA_HEAD_DIM

    def kmap(j):
        def f(hp, qb, b):
            start = jnp.clip(2 * qb - 1, 0, kblocks - 4)
            return (b * kblocks + start + j, hp)
        return f

    def bias_map(hp, qb, b):
        v = jnp.where(qb == 0, 0, jnp.where(qb == nqb - 1, 2, 1))
        return (v, hp, 0, 0, 0)

    kv_specs = [pl.BlockSpec((kw, width), kmap(j)) for j in range(4)]
    return pl.pallas_call(
        _na_kernel,
        grid=(NA_WIDTH // width, nqb, batch),
        in_specs=[pl.BlockSpec((qn, width), lambda hp, qb, b: (b * nqb + qb, hp))]
                 + kv_specs + kv_specs
                 + [pl.BlockSpec((ctx_len, width), lambda hp, qb, b: (b, hp)),
                    pl.BlockSpec((ctx_len, width), lambda hp, qb, b: (b, hp)),
                    pl.BlockSpec((1, heads_per_step, 4 * kw // LANES, qn, LANES), bias_map)],
        out_specs=pl.BlockSpec((qn, width), lambda hp, qb, b: (b * nqb + qb, hp)),
        out_shape=jax.ShapeDtypeStruct((batch * n, NA_WIDTH), BF16),
        compiler_params=_cparams("arbitrary", "arbitrary", "arbitrary"),
        name="na_attention",
    )(naq, nak, nak, nak, nak, nav, nav, nav, nav, cx_nak, cx_nav, bias)


def _ret_kernel(reverse, lg_ref, q_ref, k_ref, v_ref, kx_ref, vx_ref, *rest):
    if reverse:
        of_ref, gate_ref, o_ref, s_ref, d_ref, xi_ref, zeta_ref = rest
    else:
        o_ref, s_ref, d_ref, xi_ref, zeta_ref = rest
    c = q_ref.shape[0]
    first = (pl.program_id(0) == 0) & (pl.program_id(1) == 0)
    lane = lax.broadcasted_iota(jnp.int32, (1, LANES), 1)

    @pl.when(first)
    def _():
        i = lax.broadcasted_iota(jnp.int32, (c, c), 0)
        j = lax.broadcasted_iota(jnp.int32, (c, c), 1)
        diff = (j - i) if reverse else (i - j)
        difff = jnp.maximum(diff, 0).astype(F32)
        r = lax.broadcasted_iota(jnp.int32, (c, LANES), 0).astype(F32)
        for h in range(RET_HEADS):
            lg = lg_ref[h]
            d_ref[h] = jnp.where(diff >= 0, jnp.exp(difff * lg), 0.0)
            xi_ref[h] = jnp.exp(((c - r) if reverse else (r + 1.0)) * lg)
        for p in range(RET_HEADS // 2):
            lgp = jnp.where(lane < RET_DK, lg_ref[2 * p], lg_ref[2 * p + 1])
            zeta_ref[p] = jnp.exp((r if reverse else (c - 1.0 - r)) * lgp)

    @pl.when(pl.program_id(1) == 0)
    def _():
        l = kx_ref.shape[0]
        r = lax.broadcasted_iota(jnp.int32, (l, LANES), 0).astype(F32)
        for p in range(RET_HEADS // 2):
            lgp = jnp.where(lane < RET_DK, lg_ref[2 * p], lg_ref[2 * p + 1])
            zc = jnp.exp((r if reverse else (l - 1.0 - r)) * lgp)
            kz = kx_ref[:, p * LANES:(p + 1) * LANES].astype(F32) * zc
            for hh in range(2):
                h = 2 * p + hh
                km = jnp.where((lane // RET_DK) == hh, kz, 0.0).astype(BF16)
                s_ref[h] = _dot_tn(km, vx_ref[:, h * RET_DV:(h + 1) * RET_DV])

    outs = []
    for p in range(RET_HEADS // 2):
        q2 = q_ref[:, p * LANES:(p + 1) * LANES]
        k2 = k_ref[:, p * LANES:(p + 1) * LANES]
        kz = k2.astype(F32) * zeta_ref[p]
        for hh in range(2):
            h = 2 * p + hh
            sel = (lane // RET_DK) == hh
            qm = jnp.where(sel, q2, jnp.zeros_like(q2))
            km = jnp.where(sel, k2, jnp.zeros_like(k2))
            vh = v_ref[:, h * RET_DV:(h + 1) * RET_DV]
            inner = (_dot_nt(qm, km) * d_ref[h]).astype(BF16)
            state = s_ref[h]
            o = _dot(inner, vh) + _dot(qm, state.astype(BF16)) * xi_ref[h]
            g_chunk = jnp.exp(jnp.full((1, LANES), c, F32) * lg_ref[h])
            s_ref[h] = state * g_chunk + _dot_tn(jnp.where(sel, kz, 0.0).astype(BF16), vh)
            outs.append(o)
    if reverse:
        for h in range(RET_HEADS):
            tot = outs[h] + of_ref[:, h * RET_DV:(h + 1) * RET_DV]
            ms = jnp.mean(tot * tot, axis=-1, keepdims=True)
            gate = gate_ref[:, h * RET_DV:(h + 1) * RET_DV].astype(F32)
            o_ref[:, h * RET_DV:(h + 1) * RET_DV] = (tot * lax.rsqrt(ms + NORM_EPS) * gate).astype(o_ref.dtype)
    else:
        for h in range(RET_HEADS):
            o_ref[:, h * RET_DV:(h + 1) * RET_DV] = outs[h]


def _retention_pass(reverse, log_g, rq, rk, rv, cx_rk, cx_rv, batch, n, ctx_len, o_fwd=None, gate=None):
    c = min(RET_CHUNK, n)
    nc = n // c
    if reverse:
        seq = lambda b, i: (b * nc + nc - 1 - i, 0)
    else:
        seq = lambda b, i: (b * nc + i, 0)
    ctx = lambda b, i: (b, 0)
    in_specs = [pl.BlockSpec(memory_space=pltpu.SMEM),
                pl.BlockSpec((c, RET_QK_WIDTH), seq),
                pl.BlockSpec((c, RET_QK_WIDTH), seq),
                pl.BlockSpec((c, RET_V_WIDTH), seq),
                pl.BlockSpec((ctx_len, RET_QK_WIDTH), ctx),
                pl.BlockSpec((ctx_len, RET_V_WIDTH), ctx)]
    args = [log_g, rq, rk, rv, cx_rk, cx_rv]
    if reverse:
        in_specs += [pl.BlockSpec((c, RET_V_WIDTH), seq), pl.BlockSpec((c, RET_V_WIDTH), seq)]
        args += [o_fwd, gate]
    return pl.pallas_call(
        functools.partial(_ret_kernel, reverse),
        grid=(batch, nc),
        in_specs=in_specs,
        out_specs=pl.BlockSpec((c, RET_V_WIDTH), seq),
        out_shape=jax.ShapeDtypeStruct((batch * n, RET_V_WIDTH), BF16 if reverse else F32),
        scratch_shapes=[pltpu.VMEM((RET_HEADS, LANES, RET_DV), F32),
                        pltpu.VMEM((RET_HEADS, c, c), F32),
                        pltpu.VMEM((RET_HEADS, c, RET_DV), F32),
                        pltpu.VMEM((RET_HEADS // 2, c, LANES), F32)],
        compiler_params=_cparams("arbitrary", "arbitrary"),
        name="retention_bwd" if reverse else "retention_fwd",
    )(*args)


def _outproj_kernel(x_ref, ona_ref, oret_ref, w_ref, mod_ref, gain_ref, x1_ref, h_ref):
    mix = _dot(ona_ref[...], w_ref[0:NA_WIDTH, :]) + _dot(oret_ref[...], w_ref[NA_WIDTH:, :])
    x1 = x_ref[...] + mod_ref[0, 2:3, :] * mix
    x1_ref[...] = x1
    h_ref[...] = _rms_mod(x1, gain_ref[...], mod_ref[0, 3:4, :], mod_ref[0, 4:5, :]).astype(BF16)


def _outproj(x2d, ona, oret, w_out_bf, mods, gain, tm, tiles_per_batch):
    t, d = x2d.shape
    row = lambda i: (i, 0)
    const = lambda i: (0, 0)
    return pl.pallas_call(
        _outproj_kernel,
        grid=(t // tm,),
        in_specs=[pl.BlockSpec((tm, d), row),
                  pl.BlockSpec((tm, NA_WIDTH), row),
                  pl.BlockSpec((tm, RET_V_WIDTH), row),
                  pl.BlockSpec((NA_WIDTH + RET_V_WIDTH, d), const),
                  pl.BlockSpec((1, 6, d), lambda i: (i // tiles_per_batch, 0, 0)),
                  pl.BlockSpec((1, d), const)],
        out_specs=[pl.BlockSpec((tm, d), row), pl.BlockSpec((tm, d), row)],
        out_shape=[jax.ShapeDtypeStruct((t, d), F32), jax.ShapeDtypeStruct((t, d), BF16)],
        compiler_params=_cparams("arbitrary"),
        name="outproj",
    )(x2d, ona, oret, w_out_bf, mods, gain)


def _swiglu_kernel(h_ref, x1_ref, wg_ref, wu_ref, wd_ref, mod_ref, o_ref, acc_ref):
    j = pl.program_id(1)

    @pl.when(j == 0)
    def _():
        acc_ref[...] = jnp.zeros_like(acc_ref)

    h = h_ref[...]
    g = _dot(h, wg_ref[...])
    u = _dot(h, wu_ref[...])
    a = (g * _sigmoid(g) * u).astype(BF16)
    acc_ref[...] += _dot(a, wd_ref[...])

    @pl.when(j == pl.num_programs(1) - 1)
    def _():
        o_ref[...] = x1_ref[...] + mod_ref[0, 5:6, :] * acc_ref[...]


def _swiglu(h_bf, x1, w_gate_up_bf, w_down_bf, mods, tm, tiles_per_batch):
    t, d = x1.shape
    f = w_down_bf.shape[0]
    nf = FFN_F_TILES if (f // FFN_F_TILES) % LANES == 0 else 1
    tf = f // nf
    wmode = dict(pipeline_mode=pl.Buffered(1)) if nf == 1 else {}
    return pl.pallas_call(
        _swiglu_kernel,
        grid=(t // tm, nf),
        in_specs=[pl.BlockSpec((tm, d), lambda i, j: (i, 0)),
                  pl.BlockSpec((tm, d), lambda i, j: (i, 0)),
                  pl.BlockSpec((d, tf), lambda i, j: (0, j), **wmode),
                  pl.BlockSpec((d, tf), lambda i, j: (0, nf + j), **wmode),
                  pl.BlockSpec((tf, d), lambda i, j: (j, 0), **wmode),
                  pl.BlockSpec((1, 6, d), lambda i, j: (i // tiles_per_batch, 0, 0))],
        out_specs=pl.BlockSpec((tm, d), lambda i, j: (i, 0)),
        out_shape=jax.ShapeDtypeStruct((t, d), F32),
        scratch_shapes=[pltpu.VMEM((tm, d), F32)],
        compiler_params=_cparams("arbitrary", "arbitrary"),
        name="swiglu",
    )(h_bf, x1, w_gate_up_bf, w_gate_up_bf, w_down_bf, mods)


def _dft_tables(n, d):
    n2 = FFT_N2
    n1 = n // n2
    s = SUBLANES
    eye = np.eye(s)
    ang1 = 2.0 * np.pi * (np.outer(np.arange(n1), np.arange(n1)) % n1) / n1
    f1 = np.stack([np.cos(ang1), -np.sin(ang1)]) / np.sqrt(n)
    f1k = np.einsum('ckn,jl->ckjnl', f1, eye).reshape(2 * n1 * s, n1 * s)
    angt = 2.0 * np.pi * (np.outer(np.arange(n1), np.arange(n2)) % n) / n
    tw = np.stack([np.cos(angt), -np.sin(angt)])
    tw = tw.reshape(2, n1, n2 // s, s).transpose(0, 2, 1, 3).reshape(2, n2 // s, n1 * s, 1)
    tw = np.broadcast_to(tw, (2, n2 // s, n1 * s, LANES))
    ang2 = 2.0 * np.pi * (np.outer(np.arange(n2), np.arange(n2)) % n2) / n2
    cr, ci = np.cos(ang2), -np.sin(ang2)
    m2 = np.array([[cr, -ci], [ci, cr]])
    m2k = np.einsum('abkn,jl->akjbln', m2, eye).reshape(2 * n2 * s, 2 * s * n2)
    gw = d // FOURIER_GROUPS
    angc = 2.0 * np.pi * (np.outer(np.arange(gw), np.arange(gw)) % gw) / gw
    cs = np.concatenate([np.cos(angc), np.sin(angc)], axis=0) / np.sqrt(gw)
    return (f1k.astype(np.float32), tw.astype(np.float32), m2k.astype(np.float32),
            cs.astype(np.float32))


def _fold_kernel(cs_ref, w_ref, o_ref):
    gw = w_ref.shape[0]
    o_ref[0] = jnp.dot(cs_ref[0:gw, :], w_ref[...], precision=HIGHEST, preferred_element_type=F32)
    o_ref[1] = jnp.dot(cs_ref[gw:, :], w_ref[...], precision=HIGHEST, preferred_element_type=F32)


def _fold_channel_dft(cs, w_fourier):
    d = w_fourier.shape[0]
    gw = d // FOURIER_GROUPS
    return pl.pallas_call(
        _fold_kernel,
        grid=(FOURIER_GROUPS,),
        in_specs=[pl.BlockSpec((2 * gw, gw), lambda g: (0, 0)),
                  pl.BlockSpec((gw, d), lambda g: (g, 0))],
        out_specs=pl.BlockSpec((2, gw, d), lambda g: (0, g, 0)),
        out_shape=jax.ShapeDtypeStruct((2, d, d), F32),
        compiler_params=_cparams("arbitrary"),
        name="fold_channel_dft",
    )(cs, w_fourier)


def _fft1_kernel(x_ref, mod_ref, gain_ref, f_ref, tw_ref, o_ref):
    _, n1, s, d = x_ref.shape
    x = x_ref[...].reshape(n1 * s, d)
    h = _rms_mod(x, gain_ref[...], mod_ref[0, 0:1, :], mod_ref[0, 1:2, :]).astype(BF16)
    a = _dot(f_ref[...], h)
    ar = a[:n1 * s]
    ai = a[n1 * s:]
    reps = d // LANES
    twr = jnp.tile(tw_ref[0, 0], (1, reps))
    twi = jnp.tile(tw_ref[1, 0], (1, reps))
    o_ref[0, 0] = (ar * twr - ai * twi).reshape(n1, s, d)
    o_ref[0, 1] = (ar * twi + ai * twr).reshape(n1, s, d)


def _fft_stage1(x2, mods, gain, f1k_bf, tw, batch, n):
    d = x2.shape[-1]
    n2 = FFT_N2
    n1 = n // n2
    s = SUBLANES
    xv = x2.reshape(batch, n1, n2, d)
    return pl.pallas_call(
        _fft1_kernel,
        grid=(n2 // s, batch),
        in_specs=[pl.BlockSpec((1, n1, s, d), lambda g, b: (b, 0, g, 0)),
                  pl.BlockSpec((1, 6, d), lambda g, b: (b, 0, 0)),
                  pl.BlockSpec((1, d), lambda g, b: (0, 0)),
                  pl.BlockSpec((2 * n1 * s, n1 * s), lambda g, b: (0, 0)),
                  pl.BlockSpec((2, 1, n1 * s, LANES), lambda g, b: (0, g, 0, 0))],
        out_specs=pl.BlockSpec((1, 2, n1, s, d), lambda g, b: (b, 0, 0, g, 0)),
        out_shape=jax.ShapeDtypeStruct((batch, 2, n1, n2, d), F32),
        compiler_params=_cparams("arbitrary", "arbitrary"),
        name="fft_stage1",
    )(xv, mods, gain, f1k_bf, tw)


def _fft2_kernel(n_experts, b_ref, x_ref, m_ref, wf_ref, mod_ref, gain_ref, wr_ref,
                 x3_ref, h2_ref, route_ref):
    _, _, s, n2, d = b_ref.shape
    rows = n2 * s
    bmat = b_ref[...].reshape(2 * s * n2, d).astype(BF16)
    z = _dot(m_ref[...], bmat)
    y = _dot(z[:rows].astype(BF16), wf_ref[0]) + _dot(z[rows:].astype(BF16), wf_ref[1])
    x3 = x_ref[...].reshape(rows, d) + mod_ref[0, 2:3, :] * y
    x3_ref[...] = x3.reshape(x3_ref.shape)
    h2 = _rms_mod(x3, gain_ref[...], mod_ref[0, 3:4, :], mod_ref[0, 4:5, :])
    h2_ref[...] = h2.reshape(rows, d // LANES, LANES).reshape(h2_ref.shape)
    h_hi = h2.astype(BF16)
    h_lo = (h2 - h_hi.astype(F32)).astype(BF16)
    wr = wr_ref[...]
    w_hi = wr.astype(BF16)
    w_lo = (wr - w_hi.astype(F32)).astype(BF16)
    logits = _dot(h_hi, w_hi) + (_dot(h_lo, w_hi) + _dot(h_hi, w_lo))
    lane = lax.broadcasted_iota(jnp.int32, logits.shape, 1).astype(F32)
    lg = jnp.where(lane < n_experts, logits, -jnp.inf)
    m1 = jnp.max(lg, axis=-1, keepdims=True)
    i1 = jnp.min(jnp.where(lg == m1, lane, float(LANES)), axis=-1, keepdims=True)
    lg2 = jnp.where(lane == i1, -jnp.inf, lg)
    m2 = jnp.max(lg2, axis=-1, keepdims=True)
    i2 = jnp.min(jnp.where(lg2 == m2, lane, float(LANES)), axis=-1, keepdims=True)
    e = jnp.exp(m2 - m1)
    g1 = 1.0 / (1.0 + e)
    g2 = e / (1.0 + e)
    route = jnp.where(lane == 0, g1, jnp.where(lane == 1, g2, jnp.where(lane == 2, i1,
                                                                       jnp.where(lane == 3, i2, 0.0))))
    route_ref[...] = route.reshape(route_ref.shape)


def _fft_stage2(bmid, x2, m2k_bf, wf_bf, mods, gain, wr_pad, n_experts, batch, n):
    d = x2.shape[-1]
    n2 = FFT_N2
    n1 = n // n2
    s = SUBLANES
    xv = x2.reshape(batch, n2, n1, d)
    pos = lambda g, b: (b, 0, g, 0)
    const2 = lambda g, b: (0, 0)
    x3, h2, route = pl.pallas_call(
        functools.partial(_fft2_kernel, n_experts),
        grid=(n1 // s, batch),
        in_specs=[pl.BlockSpec((1, 2, s, n2, d), lambda g, b: (b, 0, g, 0, 0)),
                  pl.BlockSpec((1, n2, s, d), pos),
                  pl.BlockSpec((2 * n2 * s, 2 * n2 * s), const2),
                  pl.BlockSpec((2, d, d), lambda g, b: (0, 0, 0)),
                  pl.BlockSpec((1, 6, d), lambda g, b: (b, 0, 0)),
                  pl.BlockSpec((1, d), const2),
                  pl.BlockSpec((d, LANES), const2)],
        out_specs=[pl.BlockSpec((1, n2, s, d), pos),
                   pl.BlockSpec((1, n2, s, d // LANES, LANES), lambda g, b: (b, 0, g, 0, 0)),
                   pl.BlockSpec((1, n2, s, LANES), pos)],
        out_shape=[jax.ShapeDtypeStruct((batch, n2, n1, d), F32),
                   jax.ShapeDtypeStruct((batch, n2, n1, d // LANES, LANES), F32),
                   jax.ShapeDtypeStruct((batch, n2, n1, LANES), F32)],
        compiler_params=_cparams("arbitrary", "arbitrary"),
        name="fft_stage2",
    )(bmid, xv, m2k_bf, wf_bf, mods, gain, wr_pad)
    t = batch * n
    return x3.reshape(t, d), h2.reshape(t, d // LANES, LANES), route.reshape(t, LANES)


def _dispatch_kernel(tm, slot_ref, pad_end_ref, h_ref, dst_ref, zero_ref, sem, zsem):
    base = pl.program_id(0) * tm * TOP_K

    @pl.when(pl.program_id(0) == 0)
    def _():
        zero_ref[...] = jnp.zeros_like(zero_ref)
        n_experts = pad_end_ref.shape[0]
        total = pad_end_ref[n_experts - 1]
        starts = []
        for e in range(n_experts):
            prev = pad_end_ref[e - 1] if e > 0 else 0
            starts.append((pad_end_ref[e] > prev, pad_end_ref[e] - MOE_BLOCK))
        for q in range(n_experts):
            starts.append((total + q * MOE_BLOCK < dst_ref.shape[0], total + q * MOE_BLOCK))
        for needed, start in starts:
            @pl.when(needed)
            def _():
                copies = [pltpu.make_async_copy(zero_ref, dst_ref.at[pl.ds(start + q * tm, tm)], zsem)
                          for q in range(MOE_BLOCK // tm)]
                for cp in copies:
                    cp.start()
                for cp in copies:
                    cp.wait()

    def body(r, carry):
        for k in range(TOP_K):
            s = slot_ref[base + TOP_K * r + k]
            pltpu.make_async_copy(h_ref.at[pl.ds(r, 1)], dst_ref.at[pl.ds(s, 1)], sem).start(priority=k % 2)
        return carry

    lax.fori_loop(0, tm, body, 0, unroll=8)

    def drain(r, carry):
        pltpu.make_async_copy(h_ref.at[pl.ds(0, 1)], dst_ref.at[pl.ds(0, 1)], sem).wait()
        return carry

    lax.fori_loop(0, tm * TOP_K, drain, 0, unroll=8)


def _dispatch(h2, slot, pad_end, n_slots, tm):
    t = h2.shape[0]
    tile = h2.shape[1:]
    assert MOE_BLOCK % tm == 0
    return pl.pallas_call(
        functools.partial(_dispatch_kernel, tm),
        grid_spec=pltpu.PrefetchScalarGridSpec(
            num_scalar_prefetch=2,
            grid=(t // tm,),
            in_specs=[pl.BlockSpec((tm,) + tile, lambda i, sl, pe: (i, 0, 0))],
            out_specs=pl.BlockSpec(memory_space=pl.ANY),
            scratch_shapes=[pltpu.VMEM((tm,) + tile, h2.dtype),
                            pltpu.SemaphoreType.DMA(()), pltpu.SemaphoreType.DMA(())]),
        out_shape=jax.ShapeDtypeStruct((n_slots,) + tile, h2.dtype),
        compiler_params=pltpu.CompilerParams(dimension_semantics=("arbitrary",), has_side_effects=True,
                                             vmem_limit_bytes=VMEM_LIMIT),
        name="dispatch",
    )(slot, pad_end, h2)


def _expert_kernel(bexp_ref, nused_ref, x_ref, wg_ref, wu_ref, wd_ref, o_ref, xb_ref, acc_ref):
    i = pl.program_id(0)
    j = pl.program_id(1)
    last = pl.num_programs(1) - 1
    used = i < nused_ref[0]

    @pl.when(used)
    def _():
        @pl.when(j == 0)
        def _():
            xb_ref[...] = x_ref[...].reshape(xb_ref.shape).astype(BF16)
            acc_ref[...] = jnp.zeros_like(acc_ref)

        xb = xb_ref[...]
        g = _dot(xb, wg_ref[0])
        u = _dot(xb, wu_ref[0])
        a = (g * _sigmoid(g) * u).astype(BF16)
        acc_ref[...] += _dot(a, wd_ref[0])

        @pl.when(j == last)
        def _():
            o_ref[...] = acc_ref[...].reshape(o_ref.shape)

    @pl.when(jnp.logical_not(used) & (j == last))
    def _():
        o_ref[...] = jnp.zeros_like(o_ref)


def _expert_ffn(xs, w_gu_bf, w_dn_bf, block_expert, n_used):
    n_slots, sub, _ = xs.shape
    e, f, d = w_dn_bf.shape
    nb = n_slots // MOE_BLOCK
    nf = EXPERT_F_TILES if (f // EXPERT_F_TILES) % LANES == 0 else 1
    tf = f // nf

    def xmap(i, j, be, nu):
        return (jnp.minimum(i, nu[0] - 1), 0, 0)

    def jeff(i, j, nu):
        return jnp.where(i < nu[0], j, nf - 1)

    wmode = dict(pipeline_mode=pl.Buffered(1)) if nf == 1 else {}

    return pl.pallas_call(
        _expert_kernel,
        grid_spec=pltpu.PrefetchScalarGridSpec(
            num_scalar_prefetch=2,
            grid=(nb, nf),
            in_specs=[pl.BlockSpec((MOE_BLOCK, sub, LANES), xmap),
                      pl.BlockSpec((1, d, tf), lambda i, j, be, nu: (be[i], 0, jeff(i, j, nu)), **wmode),
                      pl.BlockSpec((1, d, tf), lambda i, j, be, nu: (be[i], 0, nf + jeff(i, j, nu)), **wmode),
                      pl.BlockSpec((1, tf, d), lambda i, j, be, nu: (be[i], jeff(i, j, nu), 0), **wmode)],
            out_specs=pl.BlockSpec((MOE_BLOCK, sub, LANES), lambda i, j, be, nu: (i, 0, 0)),
            scratch_shapes=[pltpu.VMEM((MOE_BLOCK, d), BF16), pltpu.VMEM((MOE_BLOCK, d), F32)]),
        out_shape=jax.ShapeDtypeStruct((n_slots, sub, LANES), F32),
        compiler_params=pltpu.CompilerParams(dimension_semantics=("arbitrary", "arbitrary"),
                                             vmem_limit_bytes=EXPERT_VMEM_LIMIT),
        name="expert_ffn",
    )(block_expert, n_used, xs, w_gu_bf, w_gu_bf, w_dn_bf)


def _combine_kernel(tm, slot_ref, x_ref, route_ref, mod_ref, ys_ref, o_ref, ybuf, sem):
    i = pl.program_id(0)

    def issue(step, b):
        base = step * (tm * TOP_K)

        def body(r, carry):
            for k in range(TOP_K):
                s = slot_ref[base + TOP_K * r + k]
                pltpu.make_async_copy(ys_ref.at[pl.ds(s, 1)], ybuf.at[b, k, pl.ds(r, 1)],
                                      sem.at[b]).start(priority=k % 2)
            return carry

        lax.fori_loop(0, tm, body, 0, unroll=8)

    @pl.when(i == 0)
    def _():
        issue(0, 0)

    @pl.when(i + 1 < pl.num_programs(0))
    def _():
        issue(i + 1, (i + 1) % 2)

    b = i % 2

    def drain(r, carry):
        pltpu.make_async_copy(ys_ref.at[pl.ds(0, 1)], ybuf.at[b, 0, pl.ds(0, 1)], sem.at[b]).wait()
        return carry

    lax.fori_loop(0, tm * TOP_K, drain, 0, unroll=8)
    y0 = ybuf[b, 0].reshape(x_ref.shape)
    y1 = ybuf[b, 1].reshape(x_ref.shape)
    g0 = route_ref[:, 0:1]
    g1 = route_ref[:, 1:2]
    o_ref[...] = x_ref[...] + mod_ref[0, 5:6, :] * (g0 * y0 + g1 * y1)


def _combine(x3, ys, slot, route, mods, tm, tiles_per_batch):
    t, d = x3.shape
    tile = ys.shape[1:]
    row = lambda i, sl: (i, 0)
    return pl.pallas_call(
        functools.partial(_combine_kernel, tm),
        grid_spec=pltpu.PrefetchScalarGridSpec(
            num_scalar_prefetch=1,
            grid=(t // tm,),
            in_specs=[pl.BlockSpec((tm, d), row),
                      pl.BlockSpec((tm, LANES), row),
                      pl.BlockSpec((1, 6, d), lambda i, sl: (i // tiles_per_batch, 0, 0)),
                      pl.BlockSpec(memory_space=pl.ANY)],
            out_specs=pl.BlockSpec((tm, d), row),
            scratch_shapes=[pltpu.VMEM((2, TOP_K, tm) + tile, F32), pltpu.SemaphoreType.DMA((2,))]),
        out_shape=jax.ShapeDtypeStruct((t, d), F32),
        compiler_params=_cparams("arbitrary"),
        name="combine",
    )(slot, x3, route, mods, ys)


def _route_slots(route, n_experts):
    t = route.shape[0]
    expert = route[:, 2:2 + TOP_K].astype(jnp.int32).reshape(t * TOP_K)
    onehot = (expert[:, None] == jnp.arange(n_experts, dtype=jnp.int32)[None, :]).astype(jnp.int32)
    csum = jnp.cumsum(onehot, axis=0)
    counts = csum[-1]
    padded = (counts + MOE_BLOCK - 1) // MOE_BLOCK * MOE_BLOCK
    pad_end = jnp.cumsum(padded)
    pad_start = pad_end - padded
    slot = jnp.sum(onehot * (pad_start[None, :] + csum - 1), axis=1).astype(jnp.int32)
    n_slots = t * TOP_K + n_experts * MOE_BLOCK
    nb = n_slots // MOE_BLOCK
    block_expert = jnp.minimum(
        jnp.searchsorted(pad_end, jnp.arange(nb, dtype=jnp.int32) * MOE_BLOCK, side='right'),
        n_experts - 1).astype(jnp.int32)
    n_used = (pad_end[-1:] // MOE_BLOCK).astype(jnp.int32)
    return slot, block_expert, n_used, pad_end.astype(jnp.int32), n_slots


def _rope_tables(n):
    pos = np.arange(n)
    row = (pos // GRID_W).astype(np.float32)
    col = (pos % GRID_W).astype(np.float32)
    n_freq = RET_DK // 4
    inv_freq = (np.float32(ROPE_BASE) ** (-np.arange(n_freq, dtype=np.float32) / n_freq)).astype(np.float32)
    ang = np.concatenate([row[:, None] * inv_freq, col[:, None] * inv_freq], axis=-1)
    cos = np.cos(ang.astype(np.float64))
    sin = np.sin(ang.astype(np.float64))
    cos_h = np.repeat(cos, 2, axis=-1)
    sin_h = np.stack([-sin, sin], axis=-1).reshape(n, RET_DK)
    return (np.tile(cos_h, (1, RET_HEADS)).astype(np.float32),
            np.tile(sin_h, (1, RET_HEADS)).astype(np.float32))


def kernel(x, c, ctx, c_ctx, l0_norm_mix, l0_norm_ffn, l0_w_ada, l0_b_ada, l0_w_in, l0_q_gain, l0_k_gain, l0_rpb, l0_ret_decay_fwd, l0_ret_decay_bwd, l0_w_out, l0_w_gate_up, l0_w_down, l1_norm_mix, l1_norm_ffn, l1_w_ada, l1_b_ada, l1_w_fourier, l1_w_router, l1_w_exp_gate_up, l1_w_exp_down):
    batch, n, d = x.shape
    ctx_len = ctx.shape[1]
    t = batch * n
    rows = n // GRID_W
    n_experts = l1_w_router.shape[1]
    assert n % (NA_QROWS * GRID_W) == 0 and rows >= NA_KROWS and n % FFT_N2 == 0
    assert l0_w_in.shape[1] == _IN_WIDTH and d % LANES == 0

    x2d = x.reshape(t, d)
    ctx2d = ctx.reshape(batch * ctx_len, d)

    ada0 = _ada(jnp.concatenate([c, c_ctx[None, :]], axis=0), l0_w_ada, l0_b_ada)
    mods0 = ada0[:batch].reshape(batch, 6, d)
    mods_ctx = ada0[batch:batch + 1].reshape(1, 6, d)
    mods1 = _ada(c, l1_w_ada, l1_b_ada)[:batch].reshape(batch, 6, d)

    w_in_bf = l0_w_in.astype(BF16)
    qg = jnp.tile(l0_q_gain.astype(F32), NA_HEADS).reshape(1, NA_WIDTH)
    kg = jnp.tile(l0_k_gain.astype(F32), NA_HEADS).reshape(1, NA_WIDTH)
    gmat = jnp.asarray(np.kron(np.eye(NA_HEADS), np.full((NA_HEAD_DIM, NA_HEAD_DIM), 1.0 / NA_HEAD_DIM)),
                       BF16)
    cos_np, sin_np = _rope_tables(n)
    gain_mix0 = l0_norm_mix.reshape(1, d)
    tm = min(INPROJ_ROWS, n)
    naq, rq, rg, nak, nav, rk, rv = _inproj(
        x2d, mods0, gain_mix0, w_in_bf, qg, kg, gmat, jnp.asarray(cos_np), jnp.asarray(sin_np),
        tm, n // tm, n // tm)
    ctx_rows = batch * ctx_len
    _, _, _, cx_nak, cx_nav, cx_rk, cx_rv = _inproj(
        ctx2d, mods_ctx, gain_mix0, w_in_bf, qg, kg, gmat,
        jnp.ones((ctx_len, RET_QK_WIDTH), F32), jnp.zeros((ctx_len, RET_QK_WIDTH), F32),
        ctx_len, ctx_rows // ctx_len, 1)

    bias = _na_bias_table(l0_rpb, rows)
    o_na = _na_attention(naq, nak, nav, cx_nak, cx_nav, bias, batch, n, ctx_len)

    lg_f = jax.nn.log_sigmoid(l0_ret_decay_fwd.astype(F32))
    lg_b = jax.nn.log_sigmoid(l0_ret_decay_bwd.astype(F32))
    o_f = _retention_pass(False, lg_f, rq, rk, rv, cx_rk, cx_rv, batch, n, ctx_len)
    o_ret = _retention_pass(True, lg_b, rq, rk, rv, cx_rk, cx_rv, batch, n, ctx_len, o_fwd=o_f, gate=rg)

    x1, h_ffn = _outproj(x2d, o_na, o_ret, l0_w_out.astype(BF16), mods0, l0_norm_ffn.reshape(1, d),
                         tm, n // tm)
    tf_rows = min(FFN_ROWS, n)
    x2 = _swiglu(h_ffn, x1, l0_w_gate_up.astype(BF16), l0_w_down.astype(BF16), mods0,
                 tf_rows, n // tf_rows)

    f1k, tw, m2k, cs = _dft_tables(n, d)
    wf = _fold_channel_dft(jnp.asarray(cs), l1_w_fourier).astype(BF16)
    bmid = _fft_stage1(x2, mods1, l1_norm_mix.reshape(1, d), jnp.asarray(f1k, BF16), jnp.asarray(tw),
                       batch, n)
    wr_pad = jnp.zeros((d, LANES), F32).at[:, :n_experts].set(l1_w_router)
    x3, h2, route = _fft_stage2(bmid, x2, jnp.asarray(m2k, BF16), wf, mods1, l1_norm_ffn.reshape(1, d),
                                wr_pad, n_experts, batch, n)

    slot, block_expert, n_used, pad_end, n_slots = _route_slots(route, n_experts)
    xs = _dispatch(h2, slot, pad_end, n_slots, tm)
    ys = _expert_ffn(xs, l1_w_exp_gate_up.astype(BF16), l1_w_exp_down.astype(BF16), block_expert, n_used)
    out = _combine(x3, ys, slot, route, mods1, tm, n // tm)
    return out.reshape(batch, n, d)
```

```python
import functools

import numpy as np
import jax
import jax.numpy as jnp
from jax import lax
from jax.experimental import pallas as pl
from jax.experimental.pallas import tpu as pltpu

F32 = jnp.float32
BF16 = jnp.bfloat16
HIGHEST = lax.Precision.HIGHEST

GRID_W = 64
NA_HEADS = 8
NA_HEAD_DIM = 64
NA_WIN_ROWS = 8
NA_WIN_COLS = 16
RET_HEADS = 4
RET_DK = 64
RET_DV = 128
ROPE_BASE = 10000.0
FOURIER_GROUPS = 8
TOP_K = 2
NORM_EPS = 1e-6

NA_WIDTH = NA_HEADS * NA_HEAD_DIM
RET_QK_WIDTH = RET_HEADS * RET_DK
RET_V_WIDTH = RET_HEADS * RET_DV

LANES = 128
SUBLANES = 8
VMEM_LIMIT = 48 * 1024 * 1024

NEG_BIG = -1e30
LOG2E = 1.4426950408889634

INPROJ_ROWS = 512
NA_QROWS = 8
NA_KROWS = 16
NA_PAIRS_PER_STEP = 4
RET_CHUNK = 512
FFN_ROWS = 512
FFN_F_TILES = 1
FFT_N2 = 64
MOE_BLOCK = 512
EXPERT_F_TILES = 1
EXPERT_VMEM_LIMIT = 56 * 1024 * 1024


def _cparams(*sem):
    return pltpu.CompilerParams(dimension_semantics=sem, vmem_limit_bytes=VMEM_LIMIT)


def _sigmoid(x):
    return 0.5 * jnp.tanh(0.5 * x) + 0.5


def _dot(a, b):
    return jnp.dot(a, b, preferred_element_type=F32)


def _dot_nt(a, b):
    return lax.dot_general(a, b, (((1,), (1,)), ((), ())), preferred_element_type=F32)


def _dot_tn(a, b):
    return lax.dot_general(a, b, (((0,), (0,)), ((), ())), preferred_element_type=F32)


def _rms_mod(x, gain, shift, scale):
    ms = jnp.mean(x * x, axis=-1, keepdims=True)
    return (x * lax.rsqrt(ms + NORM_EPS) * gain) * (1.0 + scale) + shift


def _ada_kernel(c_ref, w_ref, b_ref, o_ref):
    c = c_ref[...]
    s = c * _sigmoid(c)
    o_ref[...] = jnp.dot(s, w_ref[...], precision=HIGHEST, preferred_element_type=F32) + b_ref[...]


def _ada(cond, w_ada, b_ada):
    d, n = w_ada.shape
    cond8 = jnp.zeros((SUBLANES, d), F32).at[:cond.shape[0]].set(cond)
    tn = 1024
    return pl.pallas_call(
        _ada_kernel,
        grid=(n // tn,),
        in_specs=[pl.BlockSpec((SUBLANES, d), lambda j: (0, 0)),
                  pl.BlockSpec((d, tn), lambda j: (0, j)),
                  pl.BlockSpec((1, tn), lambda j: (0, j))],
        out_specs=pl.BlockSpec((SUBLANES, tn), lambda j: (0, j)),
        out_shape=jax.ShapeDtypeStruct((SUBLANES, n), F32),
        compiler_params=_cparams("arbitrary"),
        name="ada",
    )(cond8, w_ada, b_ada.reshape(1, n))


_O_NAQ = 0
_O_RQ = _O_NAQ + NA_WIDTH
_O_RG = _O_RQ + RET_QK_WIDTH
_O_NAK = _O_RG + RET_V_WIDTH
_O_NAV = _O_NAK + NA_WIDTH
_O_RK = _O_NAV + NA_WIDTH
_O_RV = _O_RK + RET_QK_WIDTH
_IN_WIDTH = _O_RV + RET_V_WIDTH


def _inproj_kernel(x_ref, mod_ref, gain_ref, w_ref, qg_ref, kg_ref, gmat_ref, cos_ref, sin_ref,
                   naq_ref, rq_ref, rg_ref, nak_ref, nav_ref, rk_ref, rv_ref):
    h = _rms_mod(x_ref[...], gain_ref[...], mod_ref[0, 0:1, :], mod_ref[0, 1:2, :]).astype(BF16)
    p = _dot(h, w_ref[...])
    gmat = gmat_ref[...]

    def headnorm(t, g):
        ms = _dot((t * t).astype(BF16), gmat)
        return t * lax.rsqrt(ms + NORM_EPS) * g

    cos = cos_ref[...]
    sin = sin_ref[...]
    lane = lax.broadcasted_iota(jnp.int32, (1, RET_QK_WIDTH), 1)
    even = (lane % 2) == 0

    def rope(t):
        partner = jnp.where(even, pltpu.roll(t, RET_QK_WIDTH - 1, axis=1), pltpu.roll(t, 1, axis=1))
        return t * cos + partner * sin

    naq_ref[...] = (headnorm(p[:, _O_NAQ:_O_RQ], qg_ref[...]) * (NA_HEAD_DIM ** -0.5 * LOG2E)).astype(BF16)
    rq_ref[...] = rope(p[:, _O_RQ:_O_RG]).astype(BF16)
    g = p[:, _O_RG:_O_NAK]
    rg_ref[...] = (g * _sigmoid(g)).astype(BF16)
    nak_ref[...] = headnorm(p[:, _O_NAK:_O_NAV], kg_ref[...]).astype(BF16)
    nav_ref[...] = p[:, _O_NAV:_O_RK].astype(BF16)
    rk_ref[...] = (rope(p[:, _O_RK:_O_RV]) * (RET_DK ** -0.5)).astype(BF16)
    rv_ref[...] = p[:, _O_RV:_IN_WIDTH].astype(BF16)


def _inproj(x2d, mods, gain, w_bf, qg, kg, gmat, cos, sin, tm, tiles_per_mod, pos_tiles):
    t, d = x2d.shape
    widths = (NA_WIDTH, RET_QK_WIDTH, RET_V_WIDTH, NA_WIDTH, NA_WIDTH, RET_QK_WIDTH, RET_V_WIDTH)
    row = lambda i: (i, 0)
    const = lambda i: (0, 0)
    return pl.pallas_call(
        _inproj_kernel,
        grid=(t // tm,),
        in_specs=[pl.BlockSpec((tm, d), row),
                  pl.BlockSpec((1, 6, d), lambda i: (i // tiles_per_mod, 0, 0)),
                  pl.BlockSpec((1, d), const),
                  pl.BlockSpec((d, _IN_WIDTH), const),
                  pl.BlockSpec((1, NA_WIDTH), const),
                  pl.BlockSpec((1, NA_WIDTH), const),
                  pl.BlockSpec((NA_WIDTH, NA_WIDTH), const),
                  pl.BlockSpec((tm, RET_QK_WIDTH), lambda i: (i % pos_tiles, 0)),
                  pl.BlockSpec((tm, RET_QK_WIDTH), lambda i: (i % pos_tiles, 0))],
        out_specs=[pl.BlockSpec((tm, w), row) for w in widths],
        out_shape=[jax.ShapeDtypeStruct((t, w), BF16) for w in widths],
        compiler_params=_cparams("arbitrary"),
        name="inproj",
    )(x2d, mods, gain, w_bf, qg, kg, gmat, cos, sin)


def _na_kernel(q_ref, k0, k1, k2, k3, v0, v1, v2, v3, kx_ref, vx_ref, bias_ref, o_ref):
    lane = lax.broadcasted_iota(jnp.int32, (1, LANES), 1)
    ks = (k0, k1, k2, k3)
    vs = (v0, v1, v2, v3)
    kt = k0.shape[0] // LANES
    for pp in range(q_ref.shape[1] // LANES):
        cols = slice(pp * LANES, (pp + 1) * LANES)
        q2 = q_ref[:, cols]
        outs = []
        for hh in range(2):
            head = 2 * pp + hh
            sel = (lane // NA_HEAD_DIM) == hh
            qm = jnp.where(sel, q2, jnp.zeros_like(q2))
            s = []
            for j in range(4):
                bias = jnp.concatenate([bias_ref[0, head, kt * j + jj] for jj in range(kt)], axis=1)
                s.append(_dot_nt(qm, ks[j][:, cols]) + bias.astype(F32))
            s.append(_dot_nt(qm, kx_ref[:, cols]))
            m = s[0]
            for t in s[1:-1]:
                m = jnp.maximum(m, t)
            m = jnp.maximum(jnp.max(m, axis=-1, keepdims=True), jnp.max(s[-1], axis=-1, keepdims=True))
            den = jnp.zeros_like(m)
            acc = jnp.zeros((q2.shape[0], LANES), F32)
            vals = vs + (vx_ref,)
            for j in range(5):
                p = jnp.exp2(s[j] - m)
                den = den + jnp.sum(p, axis=-1, keepdims=True)
                acc = acc + _dot(p.astype(BF16), vals[j][:, cols])
            outs.append(acc / den)
        o_ref[:, cols] = jnp.where((lane // NA_HEAD_DIM) == 0, outs[0], outs[1]).astype(o_ref.dtype)


def _na_bias_table(rpb, rows):
    h = rpb.shape[0]
    nqb = rows // NA_QROWS
    row_sel = np.zeros((3, NA_QROWS, NA_KROWS, 2 * NA_WIN_ROWS - 1), np.float32)
    row_ok = np.zeros((3, NA_QROWS, NA_KROWS), bool)
    for v, qb in enumerate((0, min(1, nqb - 1), nqb - 1)):
        start = int(np.clip(NA_QROWS * qb - NA_WIN_ROWS // 2, 0, rows - NA_KROWS))
        for rq in range(NA_QROWS):
            r = NA_QROWS * qb + rq
            r0 = int(np.clip(r - NA_WIN_ROWS // 2, 0, rows - NA_WIN_ROWS))
            for rk in range(NA_KROWS):
                kr = start + rk
                if r0 <= kr < r0 + NA_WIN_ROWS:
                    row_ok[v, rq, rk] = True
                    row_sel[v, rq, rk, kr - r + NA_WIN_ROWS - 1] = 1.0
    col_sel = np.zeros((GRID_W, GRID_W, 2 * NA_WIN_COLS - 1), np.float32)
    col_ok = np.zeros((GRID_W, GRID_W), bool)
    for c in range(GRID_W):
        c0 = int(np.clip(c - NA_WIN_COLS // 2, 0, GRID_W - NA_WIN_COLS))
        for kc in range(c0, c0 + NA_WIN_COLS):
            col_ok[c, kc] = True
            col_sel[c, kc, kc - c + NA_WIN_COLS - 1] = 1.0
    g = LANES // GRID_W
    ncol = 2 * NA_WIN_COLS - 1
    col_sel2 = np.zeros((GRID_W, g * ncol, g * GRID_W), np.float32)
    for kk in range(g):
        col_sel2[:, kk * ncol:(kk + 1) * ncol, kk * GRID_W:(kk + 1) * GRID_W] = col_sel.transpose(0, 2, 1)
    t = jnp.einsum('vqkr,hrc->vhqkc', jnp.asarray(row_sel), rpb.astype(F32), precision=HIGHEST)
    t = t.reshape(3, h, NA_QROWS, NA_KROWS // g, g * ncol)
    t = jnp.einsum('vhqpe,xez->vhpqxz', t, jnp.asarray(col_sel2), precision=HIGHEST)
    ok = (row_ok.reshape(3, NA_QROWS, NA_KROWS // g, g).transpose(0, 2, 1, 3)[:, :, :, None, :, None]
          & col_ok[None, None, None, :, None, :])
    ok = ok.reshape(3, 1, NA_KROWS // g, NA_QROWS, GRID_W, LANES)
    t = jnp.where(jnp.asarray(ok), t * LOG2E, NEG_BIG)
    return t.reshape(3, h, NA_KROWS // g, NA_QROWS * GRID_W, LANES).astype(BF16)


def _na_attention(naq, nak, nav, cx_nak, cx_nav, bias, batch, n, ctx_len):
    rows = n // GRID_W
    qn = NA_QROWS * GRID_W
    kw = NA_KROWS * GRID_W // 4
    nqb = rows // NA_QROWS
    kblocks = n // kw
    width = NA_PAIRS_PER_STEP * LANES
    heads_per_step = width // NA_HEAD_DIM

    def kmap(j):
        def f(hp, qb, b):
            start = jnp.clip(2 * qb - 1, 0, kblocks - 4)
            return (b * kblocks + start + j, hp)
        return f

    def bias_map(hp, qb, b):
        v = jnp.where(qb == 0, 0, jnp.where(qb == nqb - 1, 2, 1))
        return (v, hp, 0, 0, 0)

    kv_specs = [pl.BlockSpec((kw, width), kmap(j)) for j in range(4)]
    return pl.pallas_call(
        _na_kernel,
        grid=(NA_WIDTH // width, nqb, batch),
        in_specs=[pl.BlockSpec((qn, width), lambda hp, qb, b: (b * nqb + qb, hp))]
                 + kv_specs + kv_specs
                 + [pl.BlockSpec((ctx_len, width), lambda hp, qb, b: (b, hp)),
                    pl.BlockSpec((ctx_len, width), lambda hp, qb, b: (b, hp)),
                    pl.BlockSpec((1, heads_per_step, 4 * kw // LANES, qn, LANES), bias_map)],
        out_specs=pl.BlockSpec((qn, width), lambda hp, qb, b: (b * nqb + qb, hp)),
        out_shape=jax.ShapeDtypeStruct((batch * n, NA_WIDTH), BF16),
        compiler_params=_cparams("arbitrary", "arbitrary", "arbitrary"),
        name="na_attention",
    )(naq, nak, nak, nak, nak, nav, nav, nav, nav, cx_nak, cx_nav, bias)


def _ret_kernel(reverse, lg_ref, q_ref, k_ref, v_ref, kx_ref, vx_ref, *rest):
    if reverse:
        of_ref, gate_ref, o_ref, s_ref, d_ref, xi_ref, zeta_ref = rest
    else:
        o_ref, s_ref, d_ref, xi_ref, zeta_ref = rest
    c = q_ref.shape[0]
    first = (pl.program_id(0) == 0) & (pl.program_id(1) == 0)
    lane = lax.broadcasted_iota(jnp.int32, (1, LANES), 1)

    @pl.when(first)
    def _():
        i = lax.broadcasted_iota(jnp.int32, (c, c), 0)
        j = lax.broadcasted_iota(jnp.int32, (c, c), 1)
        diff = (j - i) if reverse else (i - j)
        difff = jnp.maximum(diff, 0).astype(F32)
        r = lax.broadcasted_iota(jnp.int32, (c, LANES), 0).astype(F32)
        for h in range(RET_HEADS):
            lg = lg_ref[h]
            d_ref[h] = jnp.where(diff >= 0, jnp.exp(difff * lg), 0.0)
            xi_ref[h] = jnp.exp(((c - r) if reverse else (r + 1.0)) * lg)
        for p in range(RET_HEADS // 2):
            lgp = jnp.where(lane < RET_DK, lg_ref[2 * p], lg_ref[2 * p + 1])
            zeta_ref[p] = jnp.exp((r if reverse else (c - 1.0 - r)) * lgp)

    @pl.when(pl.program_id(1) == 0)
    def _():
        l = kx_ref.shape[0]
        r = lax.broadcasted_iota(jnp.int32, (l, LANES), 0).astype(F32)
        for p in range(RET_HEADS // 2):
            lgp = jnp.where(lane < RET_DK, lg_ref[2 * p], lg_ref[2 * p + 1])
            zc = jnp.exp((r if reverse else (l - 1.0 - r)) * lgp)
            kz = kx_ref[:, p * LANES:(p + 1) * LANES].astype(F32) * zc
            for hh in range(2):
                h = 2 * p + hh
                km = jnp.where((lane // RET_DK) == hh, kz, 0.0).astype(BF16)
                s_ref[h] = _dot_tn(km, vx_ref[:, h * RET_DV:(h + 1) * RET_DV])

    outs = []
    for p in range(RET_HEADS // 2):
        q2 = q_ref[:, p * LANES:(p + 1) * LANES]
        k2 = k_ref[:, p * LANES:(p + 1) * LANES]
        kz = k2.astype(F32) * zeta_ref[p]
        for hh in range(2):
            h = 2 * p + hh
            sel = (lane // RET_DK) == hh
            qm = jnp.where(sel, q2, jnp.zeros_like(q2))
            km = jnp.where(sel, k2, jnp.zeros_like(k2))
            vh = v_ref[:, h * RET_DV:(h + 1) * RET_DV]
            inner = (_dot_nt(qm, km) * d_ref[h]).astype(BF16)
            state = s_ref[h]
            o = _dot(inner, vh) + _dot(qm, state.astype(BF16)) * xi_ref[h]
            g_chunk = jnp.exp(jnp.full((1, LANES), c, F32) * lg_ref[h])
            s_ref[h] = state * g_chunk + _dot_tn(jnp.where(sel, kz, 0.0).astype(BF16), vh)
            outs.append(o)
    if reverse:
        for h in range(RET_HEADS):
            tot = outs[h] + of_ref[:, h * RET_DV:(h + 1) * RET_DV]
            ms = jnp.mean(tot * tot, axis=-1, keepdims=True)
            gate = gate_ref[:, h * RET_DV:(h + 1) * RET_DV].astype(F32)
            o_ref[:, h * RET_DV:(h + 1) * RET_DV] = (tot * lax.rsqrt(ms + NORM_EPS) * gate).astype(o_ref.dtype)
    else:
        for h in range(RET_HEADS):
            o_ref[:, h * RET_DV:(h + 1) * RET_DV] = outs[h]


def _retention_pass(reverse, log_g, rq, rk, rv, cx_rk, cx_rv, batch, n, ctx_len, o_fwd=None, gate=None):
    c = min(RET_CHUNK, n)
    nc = n // c
    if reverse:
        seq = lambda b, i: (b * nc + nc - 1 - i, 0)
    else:
        seq = lambda b, i: (b * nc + i, 0)
    ctx = lambda b, i: (b, 0)
    in_specs = [pl.BlockSpec(memory_space=pltpu.SMEM),
                pl.BlockSpec((c, RET_QK_WIDTH), seq),
                pl.BlockSpec((c, RET_QK_WIDTH), seq),
                pl.BlockSpec((c, RET_V_WIDTH), seq),
                pl.BlockSpec((ctx_len, RET_QK_WIDTH), ctx),
                pl.BlockSpec((ctx_len, RET_V_WIDTH), ctx)]
    args = [log_g, rq, rk, rv, cx_rk, cx_rv]
    if reverse:
        in_specs += [pl.BlockSpec((c, RET_V_WIDTH), seq), pl.BlockSpec((c, RET_V_WIDTH), seq)]
        args += [o_fwd, gate]
    return pl.pallas_call(
        functools.partial(_ret_kernel, reverse),
        grid=(batch, nc),
        in_specs=in_specs,
        out_specs=pl.BlockSpec((c, RET_V_WIDTH), seq),
        out_shape=jax.ShapeDtypeStruct((batch * n, RET_V_WIDTH), BF16 if reverse else F32),
        scratch_shapes=[pltpu.VMEM((RET_HEADS, LANES, RET_DV), F32),
                        pltpu.VMEM((RET_HEADS, c, c), F32),
                        pltpu.VMEM((RET_HEADS, c, RET_DV), F32),
                        pltpu.VMEM((RET_HEADS // 2, c, LANES), F32)],
        compiler_params=_cparams("arbitrary", "arbitrary"),
        name="retention_bwd" if reverse else "retention_fwd",
    )(*args)


def _outproj_kernel(x_ref, ona_ref, oret_ref, w_ref, mod_ref, gain_ref, x1_ref, h_ref):
    mix = _dot(ona_ref[...], w_ref[0:NA_WIDTH, :]) + _dot(oret_ref[...], w_ref[NA_WIDTH:, :])
    x1 = x_ref[...] + mod_ref[0, 2:3, :] * mix
    x1_ref[...] = x1
    h_ref[...] = _rms_mod(x1, gain_ref[...], mod_ref[0, 3:4, :], mod_ref[0, 4:5, :]).astype(BF16)


def _outproj(x2d, ona, oret, w_out_bf, mods, gain, tm, tiles_per_batch):
    t, d = x2d.shape
    row = lambda i: (i, 0)
    const = lambda i: (0, 0)
    return pl.pallas_call(
        _outproj_kernel,
        grid=(t // tm,),
        in_specs=[pl.BlockSpec((tm, d), row),
                  pl.BlockSpec((tm, NA_WIDTH), row),
                  pl.BlockSpec((tm, RET_V_WIDTH), row),
                  pl.BlockSpec((NA_WIDTH + RET_V_WIDTH, d), const),
                  pl.BlockSpec((1, 6, d), lambda i: (i // tiles_per_batch, 0, 0)),
                  pl.BlockSpec((1, d), const)],
        out_specs=[pl.BlockSpec((tm, d), row), pl.BlockSpec((tm, d), row)],
        out_shape=[jax.ShapeDtypeStruct((t, d), F32), jax.ShapeDtypeStruct((t, d), BF16)],
        compiler_params=_cparams("arbitrary"),
        name="outproj",
    )(x2d, ona, oret, w_out_bf, mods, gain)


def _swiglu_kernel(h_ref, x1_ref, wg_ref, wu_ref, wd_ref, mod_ref, o_ref, *scratch):
    h = h_ref[...]
    g = _dot(h, wg_ref[...])
    u = _dot(h, wu_ref[...])
    a = (g * _sigmoid(g) * u).astype(BF16)
    y = _dot(a, wd_ref[...])
    if not scratch:
        o_ref[...] = x1_ref[...] + mod_ref[0, 5:6, :] * y
        return
    acc_ref, = scratch
    j = pl.program_id(1)

    @pl.when(j == 0)
    def _():
        acc_ref[...] = jnp.zeros_like(acc_ref)

    acc_ref[...] += y

    @pl.when(j == pl.num_programs(1) - 1)
    def _():
        o_ref[...] = x1_ref[...] + mod_ref[0, 5:6, :] * acc_ref[...]


def _swiglu(h_bf, x1, w_gate_up_bf, w_down_bf, mods, tm, tiles_per_batch):
    t, d = x1.shape
    f = w_down_bf.shape[0]
    nf = FFN_F_TILES if (f // FFN_F_TILES) % LANES == 0 else 1
    tf = f // nf
    wmode = dict(pipeline_mode=pl.Buffered(1)) if nf == 1 else {}
    return pl.pallas_call(
        _swiglu_kernel,
        grid=(t // tm, nf),
        in_specs=[pl.BlockSpec((tm, d), lambda i, j: (i, 0)),
                  pl.BlockSpec((tm, d), lambda i, j: (i, 0)),
                  pl.BlockSpec((d, tf), lambda i, j: (0, j), **wmode),
                  pl.BlockSpec((d, tf), lambda i, j: (0, nf + j), **wmode),
                  pl.BlockSpec((tf, d), lambda i, j: (j, 0), **wmode),
                  pl.BlockSpec((1, 6, d), lambda i, j: (i // tiles_per_batch, 0, 0))],
        out_specs=pl.BlockSpec((tm, d), lambda i, j: (i, 0)),
        out_shape=jax.ShapeDtypeStruct((t, d), F32),
        scratch_shapes=[] if nf == 1 else [pltpu.VMEM((tm, d), F32)],
        compiler_params=_cparams("arbitrary", "arbitrary"),
        name="swiglu",
    )(h_bf, x1, w_gate_up_bf, w_gate_up_bf, w_down_bf, mods)


def _dft_tables(n, d):
    n2 = FFT_N2
    n1 = n // n2
    s = SUBLANES
    eye = np.eye(s)
    ang1 = 2.0 * np.pi * (np.outer(np.arange(n1), np.arange(n1)) % n1) / n1
    f1 = np.stack([np.cos(ang1), -np.sin(ang1)]) / np.sqrt(n)
    f1k = np.einsum('ckn,jl->ckjnl', f1, eye).reshape(2 * n1 * s, n1 * s)
    angt = 2.0 * np.pi * (np.outer(np.arange(n1), np.arange(n2)) % n) / n
    tw = np.stack([np.cos(angt), -np.sin(angt)])
    tw = tw.reshape(2, n1, n2 // s, s).transpose(0, 2, 1, 3).reshape(2, n2 // s, n1 * s, 1)
    tw = np.broadcast_to(tw, (2, n2 // s, n1 * s, LANES))
    ang2 = 2.0 * np.pi * (np.outer(np.arange(n2), np.arange(n2)) % n2) / n2
    cr, ci = np.cos(ang2), -np.sin(ang2)
    m2 = np.array([[cr, -ci], [ci, cr]])
    m2k = np.einsum('abkn,jl->akjbln', m2, eye).reshape(2 * n2 * s, 2 * s * n2)
    gw = d // FOURIER_GROUPS
    angc = 2.0 * np.pi * (np.outer(np.arange(gw), np.arange(gw)) % gw) / gw
    cs = np.concatenate([np.cos(angc), np.sin(angc)], axis=0) / np.sqrt(gw)
    return (f1k.astype(np.float32), tw.astype(np.float32), m2k.astype(np.float32),
            cs.astype(np.float32))


def _fold_kernel(cs_ref, w_ref, o_ref):
    gw = w_ref.shape[0]
    o_ref[0] = jnp.dot(cs_ref[0:gw, :], w_ref[...], precision=HIGHEST, preferred_element_type=F32)
    o_ref[1] = jnp.dot(cs_ref[gw:, :], w_ref[...], precision=HIGHEST, preferred_element_type=F32)


def _fold_channel_dft(cs, w_fourier):
    d = w_fourier.shape[0]
    gw = d // FOURIER_GROUPS
    return pl.pallas_call(
        _fold_kernel,
        grid=(FOURIER_GROUPS,),
        in_specs=[pl.BlockSpec((2 * gw, gw), lambda g: (0, 0)),
                  pl.BlockSpec((gw, d), lambda g: (g, 0))],
        out_specs=pl.BlockSpec((2, gw, d), lambda g: (0, g, 0)),
        out_shape=jax.ShapeDtypeStruct((2, d, d), F32),
        compiler_params=_cparams("arbitrary"),
        name="fold_channel_dft",
    )(cs, w_fourier)


def _fft1_kernel(x_ref, mod_ref, gain_ref, f_ref, tw_ref, o_ref):
    _, n1, s, d = x_ref.shape
    x = x_ref[...].reshape(n1 * s, d)
    h = _rms_mod(x, gain_ref[...], mod_ref[0, 0:1, :], mod_ref[0, 1:2, :]).astype(BF16)
    a = _dot(f_ref[...], h)
    ar = a[:n1 * s]
    ai = a[n1 * s:]
    reps = d // LANES
    twr = jnp.tile(tw_ref[0, 0], (1, reps))
    twi = jnp.tile(tw_ref[1, 0], (1, reps))
    o_ref[0, 0] = (ar * twr - ai * twi).reshape(n1, s, d)
    o_ref[0, 1] = (ar * twi + ai * twr).reshape(n1, s, d)


def _fft_stage1(x2, mods, gain, f1k_bf, tw, batch, n):
    d = x2.shape[-1]
    n2 = FFT_N2
    n1 = n // n2
    s = SUBLANES
    xv = x2.reshape(batch, n1, n2, d)
    return pl.pallas_call(
        _fft1_kernel,
        grid=(n2 // s, batch),
        in_specs=[pl.BlockSpec((1, n1, s, d), lambda g, b: (b, 0, g, 0)),
                  pl.BlockSpec((1, 6, d), lambda g, b: (b, 0, 0)),
                  pl.BlockSpec((1, d), lambda g, b: (0, 0)),
                  pl.BlockSpec((2 * n1 * s, n1 * s), lambda g, b: (0, 0)),
                  pl.BlockSpec((2, 1, n1 * s, LANES), lambda g, b: (0, g, 0, 0))],
        out_specs=pl.BlockSpec((1, 2, n1, s, d), lambda g, b: (b, 0, 0, g, 0)),
        out_shape=jax.ShapeDtypeStruct((batch, 2, n1, n2, d), F32),
        compiler_params=_cparams("arbitrary", "arbitrary"),
        name="fft_stage1",
    )(xv, mods, gain, f1k_bf, tw)


def _fft2_kernel(n_experts, b_ref, x_ref, m_ref, wf_ref, mod_ref, gain_ref, wr_ref,
                 x3_ref, h2_ref, route_ref):
    _, _, s, n2, d = b_ref.shape
    rows = n2 * s
    bmat = b_ref[...].reshape(2 * s * n2, d).astype(BF16)
    z = _dot(m_ref[...], bmat)
    y = _dot(z[:rows].astype(BF16), wf_ref[0]) + _dot(z[rows:].astype(BF16), wf_ref[1])
    x3 = x_ref[...].reshape(rows, d) + mod_ref[0, 2:3, :] * y
    x3_ref[...] = x3.reshape(x3_ref.shape)
    h2 = _rms_mod(x3, gain_ref[...], mod_ref[0, 3:4, :], mod_ref[0, 4:5, :])
    h2_ref[...] = h2.reshape(rows, d // LANES, LANES).reshape(h2_ref.shape)
    h_hi = h2.astype(BF16)
    h_lo = (h2 - h_hi.astype(F32)).astype(BF16)
    wr = wr_ref[...]
    w_hi = wr.astype(BF16)
    w_lo = (wr - w_hi.astype(F32)).astype(BF16)
    logits = _dot(h_hi, w_hi) + (_dot(h_lo, w_hi) + _dot(h_hi, w_lo))
    lane = lax.broadcasted_iota(jnp.int32, logits.shape, 1).astype(F32)
    lg = jnp.where(lane < n_experts, logits, -jnp.inf)
    m1 = jnp.max(lg, axis=-1, keepdims=True)
    i1 = jnp.min(jnp.where(lg == m1, lane, float(LANES)), axis=-1, keepdims=True)
    lg2 = jnp.where(lane == i1, -jnp.inf, lg)
    m2 = jnp.max(lg2, axis=-1, keepdims=True)
    i2 = jnp.min(jnp.where(lg2 == m2, lane, float(LANES)), axis=-1, keepdims=True)
    e = jnp.exp(m2 - m1)
    g1 = 1.0 / (1.0 + e)
    g2 = e / (1.0 + e)
    route = jnp.where(lane == 0, g1, jnp.where(lane == 1, g2, jnp.where(lane == 2, i1,
                                                                       jnp.where(lane == 3, i2, 0.0))))
    route_ref[...] = route.reshape(route_ref.shape)


def _fft_stage2(bmid, x2, m2k_bf, wf_bf, mods, gain, wr_pad, n_experts, batch, n):
    d = x2.shape[-1]
    n2 = FFT_N2
    n1 = n // n2
    s = SUBLANES
    xv = x2.reshape(batch, n2, n1, d)
    pos = lambda g, b: (b, 0, g, 0)
    const2 = lambda g, b: (0, 0)
    x3, h2, route = pl.pallas_call(
        functools.partial(_fft2_kernel, n_experts),
        grid=(n1 // s, batch),
        in_specs=[pl.BlockSpec((1, 2, s, n2, d), lambda g, b: (b, 0, g, 0, 0)),
                  pl.BlockSpec((1, n2, s, d), pos),
                  pl.BlockSpec((2 * n2 * s, 2 * n2 * s), const2),
                  pl.BlockSpec((2, d, d), lambda g, b: (0, 0, 0)),
                  pl.BlockSpec((1, 6, d), lambda g, b: (b, 0, 0)),
                  pl.BlockSpec((1, d), const2),
                  pl.BlockSpec((d, LANES), const2)],
        out_specs=[pl.BlockSpec((1, n2, s, d), pos),
                   pl.BlockSpec((1, n2, s, d // LANES, LANES), lambda g, b: (b, 0, g, 0, 0)),
                   pl.BlockSpec((1, n2, s, LANES), pos)],
        out_shape=[jax.ShapeDtypeStruct((batch, n2, n1, d), F32),
                   jax.ShapeDtypeStruct((batch, n2, n1, d // LANES, LANES), F32),
                   jax.ShapeDtypeStruct((batch, n2, n1, LANES), F32)],
        compiler_params=_cparams("arbitrary", "arbitrary"),
        name="fft_stage2",
    )(bmid, xv, m2k_bf, wf_bf, mods, gain, wr_pad)
    t = batch * n
    return x3.reshape(t, d), h2.reshape(t, d // LANES, LANES), route.reshape(t, LANES)


def _dispatch_kernel(tm, slot_ref, pad_end_ref, h_ref, dst_ref, zero_ref, sem, zsem):
    base = pl.program_id(0) * tm * TOP_K

    @pl.when(pl.program_id(0) == 0)
    def _():
        zero_ref[...] = jnp.zeros_like(zero_ref)
        n_experts = pad_end_ref.shape[0]
        total = pad_end_ref[n_experts - 1]
        starts = []
        for e in range(n_experts):
            prev = pad_end_ref[e - 1] if e > 0 else 0
            starts.append((pad_end_ref[e] > prev, pad_end_ref[e] - MOE_BLOCK))
        for q in range(n_experts):
            starts.append((total + q * MOE_BLOCK < dst_ref.shape[0], total + q * MOE_BLOCK))
        for needed, start in starts:
            @pl.when(needed)
            def _():
                copies = [pltpu.make_async_copy(zero_ref, dst_ref.at[pl.ds(start + q * tm, tm)], zsem)
                          for q in range(MOE_BLOCK // tm)]
                for cp in copies:
                    cp.start()
                for cp in copies:
                    cp.wait()

    def body(r, carry):
        for k in range(TOP_K):
            s = slot_ref[base + TOP_K * r + k]
            pltpu.make_async_copy(h_ref.at[pl.ds(r, 1)], dst_ref.at[pl.ds(s, 1)], sem).start(priority=k % 2)
        return carry

    lax.fori_loop(0, tm, body, 0, unroll=8)

    def drain(r, carry):
        pltpu.make_async_copy(h_ref.at[pl.ds(0, 1)], dst_ref.at[pl.ds(0, 1)], sem).wait()
        return carry

    lax.fori_loop(0, tm * TOP_K, drain, 0, unroll=8)


def _dispatch(h2, slot, pad_end, n_slots, tm):
    t = h2.shape[0]
    tile = h2.shape[1:]
    assert MOE_BLOCK % tm == 0
    return pl.pallas_call(
        functools.partial(_dispatch_kernel, tm),
        grid_spec=pltpu.PrefetchScalarGridSpec(
            num_scalar_prefetch=2,
            grid=(t // tm,),
            in_specs=[pl.BlockSpec((tm,) + tile, lambda i, sl, pe: (i, 0, 0))],
            out_specs=pl.BlockSpec(memory_space=pl.ANY),
            scratch_shapes=[pltpu.VMEM((tm,) + tile, h2.dtype),
                            pltpu.SemaphoreType.DMA(()), pltpu.SemaphoreType.DMA(())]),
        out_shape=jax.ShapeDtypeStruct((n_slots,) + tile, h2.dtype),
        compiler_params=pltpu.CompilerParams(dimension_semantics=("arbitrary",), has_side_effects=True,
                                             vmem_limit_bytes=VMEM_LIMIT),
        name="dispatch",
    )(slot, pad_end, h2)


def _expert_kernel(bexp_ref, nused_ref, x_ref, wg_ref, wu_ref, wd_ref, o_ref, *scratch):
    i = pl.program_id(0)
    j = pl.program_id(1)
    last = pl.num_programs(1) - 1
    used = i < nused_ref[0]

    if not scratch:
        @pl.when(used)
        def _():
            rows = x_ref.shape[0]
            xb = x_ref[...].reshape(rows, wg_ref.shape[1]).astype(BF16)
            g = _dot(xb, wg_ref[0])
            u = _dot(xb, wu_ref[0])
            a = (g * _sigmoid(g) * u).astype(BF16)
            o_ref[...] = _dot(a, wd_ref[0]).reshape(o_ref.shape)

        @pl.when(jnp.logical_not(used))
        def _():
            o_ref[...] = jnp.zeros_like(o_ref)

        return
    xb_ref, acc_ref = scratch

    @pl.when(used)
    def _():
        @pl.when(j == 0)
        def _():
            xb_ref[...] = x_ref[...].reshape(xb_ref.shape).astype(BF16)
            acc_ref[...] = jnp.zeros_like(acc_ref)

        xb = xb_ref[...]
        g = _dot(xb, wg_ref[0])
        u = _dot(xb, wu_ref[0])
        a = (g * _sigmoid(g) * u).astype(BF16)
        acc_ref[...] += _dot(a, wd_ref[0])

        @pl.when(j == last)
        def _():
            o_ref[...] = acc_ref[...].reshape(o_ref.shape)

    @pl.when(jnp.logical_not(used) & (j == last))
    def _():
        o_ref[...] = jnp.zeros_like(o_ref)


def _expert_ffn(xs, w_gu_bf, w_dn_bf, block_expert, n_used):
    n_slots, sub, _ = xs.shape
    e, f, d = w_dn_bf.shape
    nb = n_slots // MOE_BLOCK
    nf = EXPERT_F_TILES if (f // EXPERT_F_TILES) % LANES == 0 else 1
    tf = f // nf

    def xmap(i, j, be, nu):
        return (jnp.minimum(i, nu[0] - 1), 0, 0)

    def jeff(i, j, nu):
        return jnp.where(i < nu[0], j, nf - 1)

    wmode = dict(pipeline_mode=pl.Buffered(1)) if nf == 1 else {}

    return pl.pallas_call(
        _expert_kernel,
        grid_spec=pltpu.PrefetchScalarGridSpec(
            num_scalar_prefetch=2,
            grid=(nb, nf),
            in_specs=[pl.BlockSpec((MOE_BLOCK, sub, LANES), xmap),
                      pl.BlockSpec((1, d, tf), lambda i, j, be, nu: (be[i], 0, jeff(i, j, nu)), **wmode),
                      pl.BlockSpec((1, d, tf), lambda i, j, be, nu: (be[i], 0, nf + jeff(i, j, nu)), **wmode),
                      pl.BlockSpec((1, tf, d), lambda i, j, be, nu: (be[i], jeff(i, j, nu), 0), **wmode)],
            out_specs=pl.BlockSpec((MOE_BLOCK, sub, LANES), lambda i, j, be, nu: (i, 0, 0)),
            scratch_shapes=[] if nf == 1 else [pltpu.VMEM((MOE_BLOCK, d), BF16),
                                               pltpu.VMEM((MOE_BLOCK, d), F32)]),
        out_shape=jax.ShapeDtypeStruct((n_slots, sub, LANES), F32),
        compiler_params=pltpu.CompilerParams(dimension_semantics=("arbitrary", "arbitrary"),
                                             vmem_limit_bytes=EXPERT_VMEM_LIMIT),
        name="expert_ffn",
    )(block_expert, n_used, xs, w_gu_bf, w_gu_bf, w_dn_bf)


def _combine_kernel(tm, slot_ref, x_ref, route_ref, mod_ref, ys_ref, o_ref, ybuf, sem):
    i = pl.program_id(0)

    def issue(step, b):
        base = step * (tm * TOP_K)

        def body(r, carry):
            for k in range(TOP_K):
                s = slot_ref[base + TOP_K * r + k]
                pltpu.make_async_copy(ys_ref.at[pl.ds(s, 1)], ybuf.at[b, k, pl.ds(r, 1)],
                                      sem.at[b]).start(priority=k % 2)
            return carry

        lax.fori_loop(0, tm, body, 0, unroll=8)

    @pl.when(i == 0)
    def _():
        issue(0, 0)

    @pl.when(i + 1 < pl.num_programs(0))
    def _():
        issue(i + 1, (i + 1) % 2)

    b = i % 2

    def drain(r, carry):
        pltpu.make_async_copy(ys_ref.at[pl.ds(0, 1)], ybuf.at[b, 0, pl.ds(0, 1)], sem.at[b]).wait()
        return carry

    lax.fori_loop(0, tm * TOP_K, drain, 0, unroll=8)
    y0 = ybuf[b, 0].reshape(x_ref.shape)
    y1 = ybuf[b, 1].reshape(x_ref.shape)
    g0 = route_ref[:, 0:1]
    g1 = route_ref[:, 1:2]
    o_ref[...] = x_ref[...] + mod_ref[0, 5:6, :] * (g0 * y0 + g1 * y1)


def _combine(x3, ys, slot, route, mods, tm, tiles_per_batch):
    t, d = x3.shape
    tile = ys.shape[1:]
    row = lambda i, sl: (i, 0)
    return pl.pallas_call(
        functools.partial(_combine_kernel, tm),
        grid_spec=pltpu.PrefetchScalarGridSpec(
            num_scalar_prefetch=1,
            grid=(t // tm,),
            in_specs=[pl.BlockSpec((tm, d), row),
                      pl.BlockSpec((tm, LANES), row),
                      pl.BlockSpec((1, 6, d), lambda i, sl: (i // tiles_per_batch, 0, 0)),
                      pl.BlockSpec(memory_space=pl.ANY)],
            out_specs=pl.BlockSpec((tm, d), row),
            scratch_shapes=[pltpu.VMEM((2, TOP_K, tm) + tile, F32), pltpu.SemaphoreType.DMA((2,))]),
        out_shape=jax.ShapeDtypeStruct((t, d), F32),
        compiler_params=_cparams("arbitrary"),
        name="combine",
    )(slot, x3, route, mods, ys)


def _route_slots(route, n_experts):
    t = route.shape[0]
    expert = route[:, 2:2 + TOP_K].astype(jnp.int32).reshape(t * TOP_K)
    onehot = (expert[:, None] == jnp.arange(n_experts, dtype=jnp.int32)[None, :]).astype(jnp.int32)
    csum = jnp.cumsum(onehot, axis=0)
    counts = csum[-1]
    padded = (counts + MOE_BLOCK - 1) // MOE_BLOCK * MOE_BLOCK
    pad_end = jnp.cumsum(padded)
    pad_start = pad_end - padded
    slot = jnp.sum(onehot * (pad_start[None, :] + csum - 1), axis=1).astype(jnp.int32)
    n_slots = t * TOP_K + n_experts * MOE_BLOCK
    nb = n_slots // MOE_BLOCK
    block_expert = jnp.minimum(
        jnp.searchsorted(pad_end, jnp.arange(nb, dtype=jnp.int32) * MOE_BLOCK, side='right'),
        n_experts - 1).astype(jnp.int32)
    n_used = (pad_end[-1:] // MOE_BLOCK).astype(jnp.int32)
    return slot, block_expert, n_used, pad_end.astype(jnp.int32), n_slots


def _rope_tables(n):
    pos = np.arange(n)
    row = (pos // GRID_W).astype(np.float32)
    col = (pos % GRID_W).astype(np.float32)
    n_freq = RET_DK // 4
    inv_freq = (np.float32(ROPE_BASE) ** (-np.arange(n_freq, dtype=np.float32) / n_freq)).astype(np.float32)
    ang = np.concatenate([row[:, None] * inv_freq, col[:, None] * inv_freq], axis=-1)
    cos = np.cos(ang.astype(np.float64))
    sin = np.sin(ang.astype(np.float64))
    cos_h = np.repeat(cos, 2, axis=-1)
    sin_h = np.stack([-sin, sin], axis=-1).reshape(n, RET_DK)
    return (np.tile(cos_h, (1, RET_HEADS)).astype(np.float32),
            np.tile(sin_h, (1, RET_HEADS)).astype(np.float32))


def kernel(x, c, ctx, c_ctx, l0_norm_mix, l0_norm_ffn, l0_w_ada, l0_b_ada, l0_w_in, l0_q_gain, l0_k_gain, l0_rpb, l0_ret_decay_fwd, l0_ret_decay_bwd, l0_w_out, l0_w_gate_up, l0_w_down, l1_norm_mix, l1_norm_ffn, l1_w_ada, l1_b_ada, l1_w_fourier, l1_w_router, l1_w_exp_gate_up, l1_w_exp_down):
    batch, n, d = x.shape
    ctx_len = ctx.shape[1]
    t = batch * n
    rows = n // GRID_W
    n_experts = l1_w_router.shape[1]
    assert n % (NA_QROWS * GRID_W) == 0 and rows >= NA_KROWS and n % FFT_N2 == 0
    assert l0_w_in.shape[1] == _IN_WIDTH and d % LANES == 0

    x2d = x.reshape(t, d)
    ctx2d = ctx.reshape(batch * ctx_len, d)

    ada0 = _ada(jnp.concatenate([c, c_ctx[None, :]], axis=0), l0_w_ada, l0_b_ada)
    mods0 = ada0[:batch].reshape(batch, 6, d)
    mods_ctx = ada0[batch:batch + 1].reshape(1, 6, d)
    mods1 = _ada(c, l1_w_ada, l1_b_ada)[:batch].reshape(batch, 6, d)

    w_in_bf = l0_w_in.astype(BF16)
    qg = jnp.tile(l0_q_gain.astype(F32), NA_HEADS).reshape(1, NA_WIDTH)
    kg = jnp.tile(l0_k_gain.astype(F32), NA_HEADS).reshape(1, NA_WIDTH)
    gmat = jnp.asarray(np.kron(np.eye(NA_HEADS), np.full((NA_HEAD_DIM, NA_HEAD_DIM), 1.0 / NA_HEAD_DIM)),
                       BF16)
    cos_np, sin_np = _rope_tables(n)
    gain_mix0 = l0_norm_mix.reshape(1, d)
    tm = min(INPROJ_ROWS, n)
    naq, rq, rg, nak, nav, rk, rv = _inproj(
        x2d, mods0, gain_mix0, w_in_bf, qg, kg, gmat, jnp.asarray(cos_np), jnp.asarray(sin_np),
        tm, n // tm, n // tm)
    ctx_rows = batch * ctx_len
    _, _, _, cx_nak, cx_nav, cx_rk, cx_rv = _inproj(
        ctx2d, mods_ctx, gain_mix0, w_in_bf, qg, kg, gmat,
        jnp.ones((ctx_len, RET_QK_WIDTH), F32), jnp.zeros((ctx_len, RET_QK_WIDTH), F32),
        ctx_len, ctx_rows // ctx_len, 1)

    bias = _na_bias_table(l0_rpb, rows)
    o_na = _na_attention(naq, nak, nav, cx_nak, cx_nav, bias, batch, n, ctx_len)

    lg_f = jax.nn.log_sigmoid(l0_ret_decay_fwd.astype(F32))
    lg_b = jax.nn.log_sigmoid(l0_ret_decay_bwd.astype(F32))
    o_f = _retention_pass(False, lg_f, rq, rk, rv, cx_rk, cx_rv, batch, n, ctx_len)
    o_ret = _retention_pass(True, lg_b, rq, rk, rv, cx_rk, cx_rv, batch, n, ctx_len, o_fwd=o_f, gate=rg)

    x1, h_ffn = _outproj(x2d, o_na, o_ret, l0_w_out.astype(BF16), mods0, l0_norm_ffn.reshape(1, d),
                         tm, n // tm)
    tf_rows = min(FFN_ROWS, n)
    x2 = _swiglu(h_ffn, x1, l0_w_gate_up.astype(BF16), l0_w_down.astype(BF16), mods0,
                 tf_rows, n // tf_rows)

    f1k, tw, m2k, cs = _dft_tables(n, d)
    wf = _fold_channel_dft(jnp.asarray(cs), l1_w_fourier).astype(BF16)
    bmid = _fft_stage1(x2, mods1, l1_norm_mix.reshape(1, d), jnp.asarray(f1k, BF16), jnp.asarray(tw),
                       batch, n)
    wr_pad = jnp.zeros((d, LANES), F32).at[:, :n_experts].set(l1_w_router)
    x3, h2, route = _fft_stage2(bmid, x2, jnp.asarray(m2k, BF16), wf, mods1, l1_norm_ffn.reshape(1, d),
                                wr_pad, n_experts, batch, n)

    slot, block_expert, n_used, pad_end, n_slots = _route_slots(route, n_experts)
    xs = _dispatch(h2, slot, pad_end, n_slots, tm)
    ys = _expert_ffn(xs, l1_w_exp_gate_up.astype(BF16), l1_w_exp_down.astype(BF16), block_expert, n_used)
    out = _combine(x3, ys, slot, route, mods1, tm, n // tm)
    return out.reshape(batch, n, d)
```

```python
import functools

import numpy as np
import jax
import jax.numpy as jnp
from jax import lax
from jax.experimental import pallas as pl
from jax.experimental.pallas import tpu as pltpu

F32 = jnp.float32
BF16 = jnp.bfloat16
HIGHEST = lax.Precision.HIGHEST

GRID_W = 64
NA_HEADS = 8
NA_HEAD_DIM = 64
NA_WIN_ROWS = 8
NA_WIN_COLS = 16
RET_HEADS = 4
RET_DK = 64
RET_DV = 128
ROPE_BASE = 10000.0
FOURIER_GROUPS = 8
TOP_K = 2
NORM_EPS = 1e-6

NA_WIDTH = NA_HEADS * NA_HEAD_DIM
RET_QK_WIDTH = RET_HEADS * RET_DK
RET_V_WIDTH = RET_HEADS * RET_DV

LANES = 128
SUBLANES = 8
VMEM_LIMIT = 48 * 1024 * 1024

NEG_BIG = -1e30
LOG2E = 1.4426950408889634

INPROJ_ROWS = 512
NA_QROWS = 8
NA_KROWS = 16
NA_PAIRS_PER_STEP = 4
RET_CHUNK = 512
FFN_ROWS = 512
FFN_F_TILES = 1
FFT_N2 = 64
MOE_BLOCK = 512
EXPERT_F_TILES = 1
EXPERT_VMEM_LIMIT = 56 * 1024 * 1024


def _cparams(*sem):
    return pltpu.CompilerParams(dimension_semantics=sem, vmem_limit_bytes=VMEM_LIMIT)


def _sigmoid(x):
    return 0.5 * jnp.tanh(0.5 * x) + 0.5


def _dot(a, b):
    return jnp.dot(a, b, preferred_element_type=F32)


def _dot_nt(a, b):
    return lax.dot_general(a, b, (((1,), (1,)), ((), ())), preferred_element_type=F32)


def _dot_tn(a, b):
    return lax.dot_general(a, b, (((0,), (0,)), ((), ())), preferred_element_type=F32)


def _rms_mod(x, gain, shift, scale):
    ms = jnp.mean(x * x, axis=-1, keepdims=True)
    return (x * lax.rsqrt(ms + NORM_EPS) * gain) * (1.0 + scale) + shift


def _ada_kernel(c_ref, w_ref, b_ref, o_ref):
    c = c_ref[...]
    s = c * _sigmoid(c)
    o_ref[...] = jnp.dot(s, w_ref[...], precision=HIGHEST, preferred_element_type=F32) + b_ref[...]


def _ada(cond, w_ada, b_ada):
    d, n = w_ada.shape
    cond8 = jnp.zeros((SUBLANES, d), F32).at[:cond.shape[0]].set(cond)
    tn = 1024
    return pl.pallas_call(
        _ada_kernel,
        grid=(n // tn,),
        in_specs=[pl.BlockSpec((SUBLANES, d), lambda j: (0, 0)),
                  pl.BlockSpec((d, tn), lambda j: (0, j)),
                  pl.BlockSpec((1, tn), lambda j: (0, j))],
        out_specs=pl.BlockSpec((SUBLANES, tn), lambda j: (0, j)),
        out_shape=jax.ShapeDtypeStruct((SUBLANES, n), F32),
        compiler_params=_cparams("arbitrary"),
        name="ada",
    )(cond8, w_ada, b_ada.reshape(1, n))


_O_NAQ = 0
_O_RQ = _O_NAQ + NA_WIDTH
_O_RG = _O_RQ + RET_QK_WIDTH
_O_NAK = _O_RG + RET_V_WIDTH
_O_NAV = _O_NAK + NA_WIDTH
_O_RK = _O_NAV + NA_WIDTH
_O_RV = _O_RK + RET_QK_WIDTH
_IN_WIDTH = _O_RV + RET_V_WIDTH


def _inproj_kernel(x_ref, mod_ref, gain_ref, w_ref, qg_ref, kg_ref, gmat_ref, cos_ref, sin_ref,
                   naq_ref, rq_ref, rg_ref, nak_ref, nav_ref, rk_ref, rv_ref):
    h = _rms_mod(x_ref[...], gain_ref[...], mod_ref[0, 0:1, :], mod_ref[0, 1:2, :]).astype(BF16)
    p = _dot(h, w_ref[...])
    gmat = gmat_ref[...]

    def headnorm(t, g):
        ms = _dot((t * t).astype(BF16), gmat)
        return t * lax.rsqrt(ms + NORM_EPS) * g

    cos = cos_ref[...]
    sin = sin_ref[...]
    lane = lax.broadcasted_iota(jnp.int32, (1, RET_QK_WIDTH), 1)
    even = (lane % 2) == 0

    def rope(t):
        partner = jnp.where(even, pltpu.roll(t, RET_QK_WIDTH - 1, axis=1), pltpu.roll(t, 1, axis=1))
        return t * cos + partner * sin

    naq_ref[...] = (headnorm(p[:, _O_NAQ:_O_RQ], qg_ref[...]) * (NA_HEAD_DIM ** -0.5 * LOG2E)).astype(BF16)
    rq_ref[...] = rope(p[:, _O_RQ:_O_RG]).astype(BF16)
    g = p[:, _O_RG:_O_NAK]
    rg_ref[...] = (g * _sigmoid(g)).astype(BF16)
    nak_ref[...] = headnorm(p[:, _O_NAK:_O_NAV], kg_ref[...]).astype(BF16)
    nav_ref[...] = p[:, _O_NAV:_O_RK].astype(BF16)
    rk_ref[...] = (rope(p[:, _O_RK:_O_RV]) * (RET_DK ** -0.5)).astype(BF16)
    rv_ref[...] = p[:, _O_RV:_IN_WIDTH].astype(BF16)


def _inproj(x2d, mods, gain, w_bf, qg, kg, gmat, cos, sin, tm, tiles_per_mod, pos_tiles):
    t, d = x2d.shape
    widths = (NA_WIDTH, RET_QK_WIDTH, RET_V_WIDTH, NA_WIDTH, NA_WIDTH, RET_QK_WIDTH, RET_V_WIDTH)
    row = lambda i: (i, 0)
    const = lambda i: (0, 0)
    return pl.pallas_call(
        _inproj_kernel,
        grid=(t // tm,),
        in_specs=[pl.BlockSpec((tm, d), row),
                  pl.BlockSpec((1, 6, d), lambda i: (i // tiles_per_mod, 0, 0)),
                  pl.BlockSpec((1, d), const),
                  pl.BlockSpec((d, _IN_WIDTH), const),
                  pl.BlockSpec((1, NA_WIDTH), const),
                  pl.BlockSpec((1, NA_WIDTH), const),
                  pl.BlockSpec((NA_WIDTH, NA_WIDTH), const),
                  pl.BlockSpec((tm, RET_QK_WIDTH), lambda i: (i % pos_tiles, 0)),
                  pl.BlockSpec((tm, RET_QK_WIDTH), lambda i: (i % pos_tiles, 0))],
        out_specs=[pl.BlockSpec((tm, w), row) for w in widths],
        out_shape=[jax.ShapeDtypeStruct((t, w), BF16) for w in widths],
        compiler_params=_cparams("arbitrary"),
        name="inproj",
    )(x2d, mods, gain, w_bf, qg, kg, gmat, cos, sin)


def _na_kernel(q_ref, k0, k1, k2, k3, v0, v1, v2, v3, kx_ref, vx_ref, bias_ref, o_ref):
    lane = lax.broadcasted_iota(jnp.int32, (1, LANES), 1)
    ks = (k0, k1, k2, k3)
    vs = (v0, v1, v2, v3)
    kt = k0.shape[0] // LANES
    for pp in range(q_ref.shape[1] // LANES):
        cols = slice(pp * LANES, (pp + 1) * LANES)
        q2 = q_ref[:, cols]
        outs = []
        for hh in range(2):
            head = 2 * pp + hh
            sel = (lane // NA_HEAD_DIM) == hh
            qm = jnp.where(sel, q2, jnp.zeros_like(q2))
            s = []
            for j in range(4):
                bias = jnp.concatenate([bias_ref[0, head, kt * j + jj] for jj in range(kt)], axis=1)
                s.append(_dot_nt(qm, ks[j][:, cols]) + bias.astype(F32))
            s.append(_dot_nt(qm, kx_ref[:, cols]))
            m = s[0]
            for t in s[1:-1]:
                m = jnp.maximum(m, t)
            m = jnp.maximum(jnp.max(m, axis=-1, keepdims=True), jnp.max(s[-1], axis=-1, keepdims=True))
            den = jnp.zeros_like(m)
            acc = jnp.zeros((q2.shape[0], LANES), F32)
            vals = vs + (vx_ref,)
            for j in range(5):
                p = jnp.exp2(s[j] - m)
                den = den + jnp.sum(p, axis=-1, keepdims=True)
                acc = acc + _dot(p.astype(BF16), vals[j][:, cols])
            outs.append(acc / den)
        o_ref[:, cols] = jnp.where((lane // NA_HEAD_DIM) == 0, outs[0], outs[1]).astype(o_ref.dtype)


def _na_bias_table(rpb, rows):
    h = rpb.shape[0]
    nqb = rows // NA_QROWS
    row_sel = np.zeros((3, NA_QROWS, NA_KROWS, 2 * NA_WIN_ROWS - 1), np.float32)
    row_ok = np.zeros((3, NA_QROWS, NA_KROWS), bool)
    for v, qb in enumerate((0, min(1, nqb - 1), nqb - 1)):
        start = int(np.clip(NA_QROWS * qb - NA_WIN_ROWS // 2, 0, rows - NA_KROWS))
        for rq in range(NA_QROWS):
            r = NA_QROWS * qb + rq
            r0 = int(np.clip(r - NA_WIN_ROWS // 2, 0, rows - NA_WIN_ROWS))
            for rk in range(NA_KROWS):
                kr = start + rk
                if r0 <= kr < r0 + NA_WIN_ROWS:
                    row_ok[v, rq, rk] = True
                    row_sel[v, rq, rk, kr - r + NA_WIN_ROWS - 1] = 1.0
    col_sel = np.zeros((GRID_W, GRID_W, 2 * NA_WIN_COLS - 1), np.float32)
    col_ok = np.zeros((GRID_W, GRID_W), bool)
    for c in range(GRID_W):
        c0 = int(np.clip(c - NA_WIN_COLS // 2, 0, GRID_W - NA_WIN_COLS))
        for kc in range(c0, c0 + NA_WIN_COLS):
            col_ok[c, kc] = True
            col_sel[c, kc, kc - c + NA_WIN_COLS - 1] = 1.0
    g = LANES // GRID_W
    ncol = 2 * NA_WIN_COLS - 1
    col_sel2 = np.zeros((GRID_W, g * ncol, g * GRID_W), np.float32)
    for kk in range(g):
        col_sel2[:, kk * ncol:(kk + 1) * ncol, kk * GRID_W:(kk + 1) * GRID_W] = col_sel.transpose(0, 2, 1)
    t = jnp.einsum('vqkr,hrc->vhqkc', jnp.asarray(row_sel), rpb.astype(F32), precision=HIGHEST)
    t = t.reshape(3, h, NA_QROWS, NA_KROWS // g, g * ncol)
    t = jnp.einsum('vhqpe,xez->vhpqxz', t, jnp.asarray(col_sel2), precision=HIGHEST)
    ok = (row_ok.reshape(3, NA_QROWS, NA_KROWS // g, g).transpose(0, 2, 1, 3)[:, :, :, None, :, None]
          & col_ok[None, None, None, :, None, :])
    ok = ok.reshape(3, 1, NA_KROWS // g, NA_QROWS, GRID_W, LANES)
    t = jnp.where(jnp.asarray(ok), t * LOG2E, NEG_BIG)
    return t.reshape(3, h, NA_KROWS // g, NA_QROWS * GRID_W, LANES).astype(BF16)


def _na_attention(naq, nak, nav, cx_nak, cx_nav, bias, batch, n, ctx_len):
    rows = n // GRID_W
    qn = NA_QROWS * GRID_W
    kw = NA_KROWS * GRID_W // 4
    nqb = rows // NA_QROWS
    kblocks = n // kw
    width = NA_PAIRS_PER_STEP * LANES
    heads_per_step = width // NA_HEAD_DIM

    def kmap(j):
        def f(hp, qb, b):
            start = jnp.clip(2 * qb - 1, 0, kblocks - 4)
            return (b * kblocks + start + j, hp)
        return f

    def bias_map(hp, qb, b):
        v = jnp.where(qb == 0, 0, jnp.where(qb == nqb - 1, 2, 1))
        return (v, hp, 0, 0, 0)

    kv_specs = [pl.BlockSpec((kw, width), kmap(j)) for j in range(4)]
    return pl.pallas_call(
        _na_kernel,
        grid=(NA_WIDTH // width, nqb, batch),
        in_specs=[pl.BlockSpec((qn, width), lambda hp, qb, b: (b * nqb + qb, hp))]
                 + kv_specs + kv_specs
                 + [pl.BlockSpec((ctx_len, width), lambda hp, qb, b: (b, hp)),
                    pl.BlockSpec((ctx_len, width), lambda hp, qb, b: (b, hp)),
                    pl.BlockSpec((1, heads_per_step, 4 * kw // LANES, qn, LANES), bias_map)],
        out_specs=pl.BlockSpec((qn, width), lambda hp, qb, b: (b * nqb + qb, hp)),
        out_shape=jax.ShapeDtypeStruct((batch * n, NA_WIDTH), BF16),
        compiler_params=_cparams("arbitrary", "arbitrary", "arbitrary"),
        name="na_attention",
    )(naq, nak, nak, nak, nak, nav, nav, nav, nav, cx_nak, cx_nav, bias)


def _ret_kernel(reverse, lg_ref, q_ref, k_ref, v_ref, kx_ref, vx_ref, *rest):
    if reverse:
        of_ref, gate_ref, o_ref, s_ref, d_ref, xi_ref, zeta_ref = rest
    else:
        o_ref, s_ref, d_ref, xi_ref, zeta_ref = rest
    c = q_ref.shape[0]
    first = (pl.program_id(0) == 0) & (pl.program_id(1) == 0)
    lane = lax.broadcasted_iota(jnp.int32, (1, LANES), 1)

    @pl.when(first)
    def _():
        i = lax.broadcasted_iota(jnp.int32, (c, c), 0)
        j = lax.broadcasted_iota(jnp.int32, (c, c), 1)
        diff = (j - i) if reverse else (i - j)
        difff = jnp.maximum(diff, 0).astype(F32)
        r = lax.broadcasted_iota(jnp.int32, (c, LANES), 0).astype(F32)
        for h in range(RET_HEADS):
            lg = lg_ref[h]
            d_ref[h] = jnp.where(diff >= 0, jnp.exp(difff * lg), 0.0)
            xi_ref[h] = jnp.exp(((c - r) if reverse else (r + 1.0)) * lg)
        for p in range(RET_HEADS // 2):
            lgp = jnp.where(lane < RET_DK, lg_ref[2 * p], lg_ref[2 * p + 1])
            zeta_ref[p] = jnp.exp((r if reverse else (c - 1.0 - r)) * lgp)

    @pl.when(pl.program_id(1) == 0)
    def _():
        l = kx_ref.shape[0]
        r = lax.broadcasted_iota(jnp.int32, (l, LANES), 0).astype(F32)
        for p in range(RET_HEADS // 2):
            lgp = jnp.where(lane < RET_DK, lg_ref[2 * p], lg_ref[2 * p + 1])
            zc = jnp.exp((r if reverse else (l - 1.0 - r)) * lgp)
            kz = kx_ref[:, p * LANES:(p + 1) * LANES].astype(F32) * zc
            for hh in range(2):
                h = 2 * p + hh
                km = jnp.where((lane // RET_DK) == hh, kz, 0.0).astype(BF16)
                s_ref[h] = _dot_tn(km, vx_ref[:, h * RET_DV:(h + 1) * RET_DV])

    outs = []
    for p in range(RET_HEADS // 2):
        q2 = q_ref[:, p * LANES:(p + 1) * LANES]
        k2 = k_ref[:, p * LANES:(p + 1) * LANES]
        kz = k2.astype(F32) * zeta_ref[p]
        for hh in range(2):
            h = 2 * p + hh
            sel = (lane // RET_DK) == hh
            qm = jnp.where(sel, q2, jnp.zeros_like(q2))
            km = jnp.where(sel, k2, jnp.zeros_like(k2))
            vh = v_ref[:, h * RET_DV:(h + 1) * RET_DV]
            inner = (_dot_nt(qm, km) * d_ref[h]).astype(BF16)
            state = s_ref[h]
            o = _dot(inner, vh) + _dot(qm, state.astype(BF16)) * xi_ref[h]
            g_chunk = jnp.exp(jnp.full((1, LANES), c, F32) * lg_ref[h])
            s_ref[h] = state * g_chunk + _dot_tn(jnp.where(sel, kz, 0.0).astype(BF16), vh)
            outs.append(o)
    if reverse:
        for h in range(RET_HEADS):
            tot = outs[h] + of_ref[:, h * RET_DV:(h + 1) * RET_DV]
            ms = jnp.mean(tot * tot, axis=-1, keepdims=True)
            gate = gate_ref[:, h * RET_DV:(h + 1) * RET_DV].astype(F32)
            o_ref[:, h * RET_DV:(h + 1) * RET_DV] = (tot * lax.rsqrt(ms + NORM_EPS) * gate).astype(o_ref.dtype)
    else:
        for h in range(RET_HEADS):
            o_ref[:, h * RET_DV:(h + 1) * RET_DV] = outs[h]


def _retention_pass(reverse, log_g, rq, rk, rv, cx_rk, cx_rv, batch, n, ctx_len, o_fwd=None, gate=None):
    c = min(RET_CHUNK, n)
    nc = n // c
    if reverse:
        seq = lambda b, i: (b * nc + nc - 1 - i, 0)
    else:
        seq = lambda b, i: (b * nc + i, 0)
    ctx = lambda b, i: (b, 0)
    in_specs = [pl.BlockSpec(memory_space=pltpu.SMEM),
                pl.BlockSpec((c, RET_QK_WIDTH), seq),
                pl.BlockSpec((c, RET_QK_WIDTH), seq),
                pl.BlockSpec((c, RET_V_WIDTH), seq),
                pl.BlockSpec((ctx_len, RET_QK_WIDTH), ctx),
                pl.BlockSpec((ctx_len, RET_V_WIDTH), ctx)]
    args = [log_g, rq, rk, rv, cx_rk, cx_rv]
    if reverse:
        in_specs += [pl.BlockSpec((c, RET_V_WIDTH), seq), pl.BlockSpec((c, RET_V_WIDTH), seq)]
        args += [o_fwd, gate]
    return pl.pallas_call(
        functools.partial(_ret_kernel, reverse),
        grid=(batch, nc),
        in_specs=in_specs,
        out_specs=pl.BlockSpec((c, RET_V_WIDTH), seq),
        out_shape=jax.ShapeDtypeStruct((batch * n, RET_V_WIDTH), BF16 if reverse else F32),
        scratch_shapes=[pltpu.VMEM((RET_HEADS, LANES, RET_DV), F32),
                        pltpu.VMEM((RET_HEADS, c, c), F32),
                        pltpu.VMEM((RET_HEADS, c, RET_DV), F32),
                        pltpu.VMEM((RET_HEADS // 2, c, LANES), F32)],
        compiler_params=_cparams("arbitrary", "arbitrary"),
        name="retention_bwd" if reverse else "retention_fwd",
    )(*args)


def _outproj_kernel(x_ref, ona_ref, oret_ref, w_ref, mod_ref, gain_ref, x1_ref, h_ref):
    mix = _dot(ona_ref[...], w_ref[0:NA_WIDTH, :]) + _dot(oret_ref[...], w_ref[NA_WIDTH:, :])
    x1 = x_ref[...] + mod_ref[0, 2:3, :] * mix
    x1_ref[...] = x1
    h_ref[...] = _rms_mod(x1, gain_ref[...], mod_ref[0, 3:4, :], mod_ref[0, 4:5, :]).astype(BF16)


def _outproj(x2d, ona, oret, w_out_bf, mods, gain, tm, tiles_per_batch):
    t, d = x2d.shape
    row = lambda i: (i, 0)
    const = lambda i: (0, 0)
    return pl.pallas_call(
        _outproj_kernel,
        grid=(t // tm,),
        in_specs=[pl.BlockSpec((tm, d), row),
                  pl.BlockSpec((tm, NA_WIDTH), row),
                  pl.BlockSpec((tm, RET_V_WIDTH), row),
                  pl.BlockSpec((NA_WIDTH + RET_V_WIDTH, d), const),
                  pl.BlockSpec((1, 6, d), lambda i: (i // tiles_per_batch, 0, 0)),
                  pl.BlockSpec((1, d), const)],
        out_specs=[pl.BlockSpec((tm, d), row), pl.BlockSpec((tm, d), row)],
        out_shape=[jax.ShapeDtypeStruct((t, d), F32), jax.ShapeDtypeStruct((t, d), BF16)],
        compiler_params=_cparams("arbitrary"),
        name="outproj",
    )(x2d, ona, oret, w_out_bf, mods, gain)


def _swiglu_kernel(h_ref, x1_ref, wg_ref, wu_ref, wd_ref, mod_ref, o_ref, *scratch):
    h = h_ref[...]
    g = _dot(h, wg_ref[...])
    u = _dot(h, wu_ref[...])
    a = (g * _sigmoid(g) * u).astype(BF16)
    y = _dot(a, wd_ref[...])
    if not scratch:
        o_ref[...] = x1_ref[...] + mod_ref[0, 5:6, :] * y
        return
    acc_ref, = scratch
    j = pl.program_id(1)

    @pl.when(j == 0)
    def _():
        acc_ref[...] = jnp.zeros_like(acc_ref)

    acc_ref[...] += y

    @pl.when(j == pl.num_programs(1) - 1)
    def _():
        o_ref[...] = x1_ref[...] + mod_ref[0, 5:6, :] * acc_ref[...]


def _swiglu(h_bf, x1, w_gate_up_bf, w_down_bf, mods, tm, tiles_per_batch):
    t, d = x1.shape
    f = w_down_bf.shape[0]
    nf = FFN_F_TILES if (f // FFN_F_TILES) % LANES == 0 else 1
    tf = f // nf
    wmode = dict(pipeline_mode=pl.Buffered(1)) if nf == 1 else {}
    return pl.pallas_call(
        _swiglu_kernel,
        grid=(t // tm, nf),
        in_specs=[pl.BlockSpec((tm, d), lambda i, j: (i, 0)),
                  pl.BlockSpec((tm, d), lambda i, j: (i, 0)),
                  pl.BlockSpec((d, tf), lambda i, j: (0, j), **wmode),
                  pl.BlockSpec((d, tf), lambda i, j: (0, nf + j), **wmode),
                  pl.BlockSpec((tf, d), lambda i, j: (j, 0), **wmode),
                  pl.BlockSpec((1, 6, d), lambda i, j: (i // tiles_per_batch, 0, 0))],
        out_specs=pl.BlockSpec((tm, d), lambda i, j: (i, 0)),
        out_shape=jax.ShapeDtypeStruct((t, d), F32),
        scratch_shapes=[] if nf == 1 else [pltpu.VMEM((tm, d), F32)],
        compiler_params=_cparams("arbitrary", "arbitrary"),
        name="swiglu",
    )(h_bf, x1, w_gate_up_bf, w_gate_up_bf, w_down_bf, mods)


def _mix_ffn_kernel(x_ref, ona_ref, oret_ref, wout_ref, gain_ref, wg_ref, wu_ref, wd_ref, mod_ref, o_ref):
    mix = _dot(ona_ref[...], wout_ref[0:NA_WIDTH, :]) + _dot(oret_ref[...], wout_ref[NA_WIDTH:, :])
    x1 = x_ref[...] + mod_ref[0, 2:3, :] * mix
    h = _rms_mod(x1, gain_ref[...], mod_ref[0, 3:4, :], mod_ref[0, 4:5, :]).astype(BF16)
    g = _dot(h, wg_ref[...])
    u = _dot(h, wu_ref[...])
    a = (g * _sigmoid(g) * u).astype(BF16)
    o_ref[...] = x1 + mod_ref[0, 5:6, :] * _dot(a, wd_ref[...])


def _mix_ffn(x2d, ona, oret, w_out_bf, gain, w_gate_up_bf, w_down_bf, mods, tm, tiles_per_batch):
    t, d = x2d.shape
    f = w_down_bf.shape[0]
    if FFN_F_TILES != 1 or f % LANES != 0:
        x1, h = _outproj(x2d, ona, oret, w_out_bf, mods, gain, tm, tiles_per_batch)
        return _swiglu(h, x1, w_gate_up_bf, w_down_bf, mods, tm, tiles_per_batch)
    row = lambda i: (i, 0)
    resident = dict(pipeline_mode=pl.Buffered(1))
    return pl.pallas_call(
        _mix_ffn_kernel,
        grid=(t // tm,),
        in_specs=[pl.BlockSpec((tm, d), row),
                  pl.BlockSpec((tm, NA_WIDTH), row),
                  pl.BlockSpec((tm, RET_V_WIDTH), row),
                  pl.BlockSpec((NA_WIDTH + RET_V_WIDTH, d), lambda i: (0, 0), **resident),
                  pl.BlockSpec((1, d), lambda i: (0, 0)),
                  pl.BlockSpec((d, f), lambda i: (0, 0), **resident),
                  pl.BlockSpec((d, f), lambda i: (0, 1), **resident),
                  pl.BlockSpec((f, d), lambda i: (0, 0), **resident),
                  pl.BlockSpec((1, 6, d), lambda i: (i // tiles_per_batch, 0, 0))],
        out_specs=pl.BlockSpec((tm, d), row),
        out_shape=jax.ShapeDtypeStruct((t, d), F32),
        compiler_params=pltpu.CompilerParams(dimension_semantics=("arbitrary",),
                                             vmem_limit_bytes=EXPERT_VMEM_LIMIT),
        name="mix_ffn",
    )(x2d, ona, oret, w_out_bf, gain, w_gate_up_bf, w_gate_up_bf, w_down_bf, mods)


def _dft_tables(n, d):
    n2 = FFT_N2
    n1 = n // n2
    s = SUBLANES
    eye = np.eye(s)
    ang1 = 2.0 * np.pi * (np.outer(np.arange(n1), np.arange(n1)) % n1) / n1
    f1 = np.stack([np.cos(ang1), -np.sin(ang1)]) / np.sqrt(n)
    f1k = np.einsum('ckn,jl->ckjnl', f1, eye).reshape(2 * n1 * s, n1 * s)
    angt = 2.0 * np.pi * (np.outer(np.arange(n1), np.arange(n2)) % n) / n
    tw = np.stack([np.cos(angt), -np.sin(angt)])
    tw = tw.reshape(2, n1, n2 // s, s).transpose(0, 2, 1, 3).reshape(2, n2 // s, n1 * s, 1)
    tw = np.broadcast_to(tw, (2, n2 // s, n1 * s, LANES))
    ang2 = 2.0 * np.pi * (np.outer(np.arange(n2), np.arange(n2)) % n2) / n2
    cr, ci = np.cos(ang2), -np.sin(ang2)
    m2 = np.array([[cr, -ci], [ci, cr]])
    m2k = np.einsum('abkn,jl->akjbln', m2, eye).reshape(2 * n2 * s, 2 * s * n2)
    gw = d // FOURIER_GROUPS
    angc = 2.0 * np.pi * (np.outer(np.arange(gw), np.arange(gw)) % gw) / gw
    cs = np.concatenate([np.cos(angc), np.sin(angc)], axis=0) / np.sqrt(gw)
    return (f1k.astype(np.float32), tw.astype(np.float32), m2k.astype(np.float32),
            cs.astype(np.float32))


def _fold_kernel(cs_ref, w_ref, o_ref):
    gw = w_ref.shape[0]
    o_ref[0] = jnp.dot(cs_ref[0:gw, :], w_ref[...], precision=HIGHEST, preferred_element_type=F32)
    o_ref[1] = jnp.dot(cs_ref[gw:, :], w_ref[...], precision=HIGHEST, preferred_element_type=F32)


def _fold_channel_dft(cs, w_fourier):
    d = w_fourier.shape[0]
    gw = d // FOURIER_GROUPS
    return pl.pallas_call(
        _fold_kernel,
        grid=(FOURIER_GROUPS,),
        in_specs=[pl.BlockSpec((2 * gw, gw), lambda g: (0, 0)),
                  pl.BlockSpec((gw, d), lambda g: (g, 0))],
        out_specs=pl.BlockSpec((2, gw, d), lambda g: (0, g, 0)),
        out_shape=jax.ShapeDtypeStruct((2, d, d), F32),
        compiler_params=_cparams("arbitrary"),
        name="fold_channel_dft",
    )(cs, w_fourier)


def _fft1_kernel(x_ref, mod_ref, gain_ref, f_ref, tw_ref, o_ref):
    _, n1, s, d = x_ref.shape
    x = x_ref[...].reshape(n1 * s, d)
    h = _rms_mod(x, gain_ref[...], mod_ref[0, 0:1, :], mod_ref[0, 1:2, :]).astype(BF16)
    a = _dot(f_ref[...], h)
    ar = a[:n1 * s]
    ai = a[n1 * s:]
    reps = d // LANES
    twr = jnp.tile(tw_ref[0, 0], (1, reps))
    twi = jnp.tile(tw_ref[1, 0], (1, reps))
    o_ref[0, 0] = (ar * twr - ai * twi).reshape(n1, s, d)
    o_ref[0, 1] = (ar * twi + ai * twr).reshape(n1, s, d)


def _fft_stage1(x2, mods, gain, f1k_bf, tw, batch, n):
    d = x2.shape[-1]
    n2 = FFT_N2
    n1 = n // n2
    s = SUBLANES
    xv = x2.reshape(batch, n1, n2, d)
    return pl.pallas_call(
        _fft1_kernel,
        grid=(n2 // s, batch),
        in_specs=[pl.BlockSpec((1, n1, s, d), lambda g, b: (b, 0, g, 0)),
                  pl.BlockSpec((1, 6, d), lambda g, b: (b, 0, 0)),
                  pl.BlockSpec((1, d), lambda g, b: (0, 0)),
                  pl.BlockSpec((2 * n1 * s, n1 * s), lambda g, b: (0, 0)),
                  pl.BlockSpec((2, 1, n1 * s, LANES), lambda g, b: (0, g, 0, 0))],
        out_specs=pl.BlockSpec((1, 2, n1, s, d), lambda g, b: (b, 0, 0, g, 0)),
        out_shape=jax.ShapeDtypeStruct((batch, 2, n1, n2, d), F32),
        compiler_params=_cparams("arbitrary", "arbitrary"),
        name="fft_stage1",
    )(xv, mods, gain, f1k_bf, tw)


def _fft2_kernel(n_experts, b_ref, x_ref, m_ref, wf_ref, mod_ref, gain_ref, wr_ref,
                 x3_ref, h2_ref, route_ref):
    _, _, s, n2, d = b_ref.shape
    rows = n2 * s
    bmat = b_ref[...].reshape(2 * s * n2, d).astype(BF16)
    z = _dot(m_ref[...], bmat)
    y = _dot(z[:rows].astype(BF16), wf_ref[0]) + _dot(z[rows:].astype(BF16), wf_ref[1])
    x3 = x_ref[...].reshape(rows, d) + mod_ref[0, 2:3, :] * y
    x3_ref[...] = x3.reshape(x3_ref.shape)
    h2 = _rms_mod(x3, gain_ref[...], mod_ref[0, 3:4, :], mod_ref[0, 4:5, :])
    h2_ref[...] = h2.reshape(rows, d // LANES, LANES).reshape(h2_ref.shape)
    h_hi = h2.astype(BF16)
    h_lo = (h2 - h_hi.astype(F32)).astype(BF16)
    wr = wr_ref[...]
    w_hi = wr.astype(BF16)
    w_lo = (wr - w_hi.astype(F32)).astype(BF16)
    logits = _dot(h_hi, w_hi) + (_dot(h_lo, w_hi) + _dot(h_hi, w_lo))
    lane = lax.broadcasted_iota(jnp.int32, logits.shape, 1).astype(F32)
    lg = jnp.where(lane < n_experts, logits, -jnp.inf)
    m1 = jnp.max(lg, axis=-1, keepdims=True)
    i1 = jnp.min(jnp.where(lg == m1, lane, float(LANES)), axis=-1, keepdims=True)
    lg2 = jnp.where(lane == i1, -jnp.inf, lg)
    m2 = jnp.max(lg2, axis=-1, keepdims=True)
    i2 = jnp.min(jnp.where(lg2 == m2, lane, float(LANES)), axis=-1, keepdims=True)
    e = jnp.exp(m2 - m1)
    g1 = 1.0 / (1.0 + e)
    g2 = e / (1.0 + e)
    route = jnp.where(lane == 0, g1, jnp.where(lane == 1, g2, jnp.where(lane == 2, i1,
                                                                       jnp.where(lane == 3, i2, 0.0))))
    route_ref[...] = route.reshape(route_ref.shape)


def _fft_stage2(bmid, x2, m2k_bf, wf_bf, mods, gain, wr_pad, n_experts, batch, n):
    d = x2.shape[-1]
    n2 = FFT_N2
    n1 = n // n2
    s = SUBLANES
    xv = x2.reshape(batch, n2, n1, d)
    pos = lambda g, b: (b, 0, g, 0)
    const2 = lambda g, b: (0, 0)
    x3, h2, route = pl.pallas_call(
        functools.partial(_fft2_kernel, n_experts),
        grid=(n1 // s, batch),
        in_specs=[pl.BlockSpec((1, 2, s, n2, d), lambda g, b: (b, 0, g, 0, 0)),
                  pl.BlockSpec((1, n2, s, d), pos),
                  pl.BlockSpec((2 * n2 * s, 2 * n2 * s), const2),
                  pl.BlockSpec((2, d, d), lambda g, b: (0, 0, 0)),
                  pl.BlockSpec((1, 6, d), lambda g, b: (b, 0, 0)),
                  pl.BlockSpec((1, d), const2),
                  pl.BlockSpec((d, LANES), const2)],
        out_specs=[pl.BlockSpec((1, n2, s, d), pos),
                   pl.BlockSpec((1, n2, s, d // LANES, LANES), lambda g, b: (b, 0, g, 0, 0)),
                   pl.BlockSpec((1, n2, s, LANES), pos)],
        out_shape=[jax.ShapeDtypeStruct((batch, n2, n1, d), F32),
                   jax.ShapeDtypeStruct((batch, n2, n1, d // LANES, LANES), F32),
                   jax.ShapeDtypeStruct((batch, n2, n1, LANES), F32)],
        compiler_params=_cparams("arbitrary", "arbitrary"),
        name="fft_stage2",
    )(bmid, xv, m2k_bf, wf_bf, mods, gain, wr_pad)
    t = batch * n
    return x3.reshape(t, d), h2.reshape(t, d // LANES, LANES), route.reshape(t, LANES)


def _dispatch_kernel(tm, slot_ref, pad_end_ref, h_ref, dst_ref, zero_ref, sem, zsem):
    base = pl.program_id(0) * tm * TOP_K

    @pl.when(pl.program_id(0) == 0)
    def _():
        zero_ref[...] = jnp.zeros_like(zero_ref)
        n_experts = pad_end_ref.shape[0]
        total = pad_end_ref[n_experts - 1]
        starts = []
        for e in range(n_experts):
            prev = pad_end_ref[e - 1] if e > 0 else 0
            starts.append((pad_end_ref[e] > prev, pad_end_ref[e] - MOE_BLOCK))
        for q in range(n_experts):
            starts.append((total + q * MOE_BLOCK < dst_ref.shape[0], total + q * MOE_BLOCK))
        for needed, start in starts:
            @pl.when(needed)
            def _():
                copies = [pltpu.make_async_copy(zero_ref, dst_ref.at[pl.ds(start + q * tm, tm)], zsem)
                          for q in range(MOE_BLOCK // tm)]
                for cp in copies:
                    cp.start()
                for cp in copies:
                    cp.wait()

    def body(r, carry):
        for k in range(TOP_K):
            s = slot_ref[base + TOP_K * r + k]
            pltpu.make_async_copy(h_ref.at[pl.ds(r, 1)], dst_ref.at[pl.ds(s, 1)], sem).start(priority=k % 2)
        return carry

    lax.fori_loop(0, tm, body, 0, unroll=8)

    def drain(r, carry):
        pltpu.make_async_copy(h_ref.at[pl.ds(0, 1)], dst_ref.at[pl.ds(0, 1)], sem).wait()
        return carry

    lax.fori_loop(0, tm * TOP_K, drain, 0, unroll=8)


def _dispatch(h2, slot, pad_end, n_slots, tm):
    t = h2.shape[0]
    tile = h2.shape[1:]
    assert MOE_BLOCK % tm == 0
    return pl.pallas_call(
        functools.partial(_dispatch_kernel, tm),
        grid_spec=pltpu.PrefetchScalarGridSpec(
            num_scalar_prefetch=2,
            grid=(t // tm,),
            in_specs=[pl.BlockSpec((tm,) + tile, lambda i, sl, pe: (i, 0, 0))],
            out_specs=pl.BlockSpec(memory_space=pl.ANY),
            scratch_shapes=[pltpu.VMEM((tm,) + tile, h2.dtype),
                            pltpu.SemaphoreType.DMA(()), pltpu.SemaphoreType.DMA(())]),
        out_shape=jax.ShapeDtypeStruct((n_slots,) + tile, h2.dtype),
        compiler_params=pltpu.CompilerParams(dimension_semantics=("arbitrary",), has_side_effects=True,
                                             vmem_limit_bytes=VMEM_LIMIT),
        name="dispatch",
    )(slot, pad_end, h2)


def _expert_kernel(bexp_ref, nused_ref, x_ref, wg_ref, wu_ref, wd_ref, o_ref, *scratch):
    i = pl.program_id(0)
    j = pl.program_id(1)
    last = pl.num_programs(1) - 1
    used = i < nused_ref[0]

    if not scratch:
        @pl.when(used)
        def _():
            rows = x_ref.shape[0]
            xb = x_ref[...].reshape(rows, wg_ref.shape[1]).astype(BF16)
            g = _dot(xb, wg_ref[0])
            u = _dot(xb, wu_ref[0])
            a = (g * _sigmoid(g) * u).astype(BF16)
            o_ref[...] = _dot(a, wd_ref[0]).reshape(o_ref.shape)

        @pl.when(jnp.logical_not(used))
        def _():
            o_ref[...] = jnp.zeros_like(o_ref)

        return
    xb_ref, acc_ref = scratch

    @pl.when(used)
    def _():
        @pl.when(j == 0)
        def _():
            xb_ref[...] = x_ref[...].reshape(xb_ref.shape).astype(BF16)
            acc_ref[...] = jnp.zeros_like(acc_ref)

        xb = xb_ref[...]
        g = _dot(xb, wg_ref[0])
        u = _dot(xb, wu_ref[0])
        a = (g * _sigmoid(g) * u).astype(BF16)
        acc_ref[...] += _dot(a, wd_ref[0])

        @pl.when(j == last)
        def _():
            o_ref[...] = acc_ref[...].reshape(o_ref.shape)

    @pl.when(jnp.logical_not(used) & (j == last))
    def _():
        o_ref[...] = jnp.zeros_like(o_ref)


def _expert_ffn(xs, w_gu_bf, w_dn_bf, block_expert, n_used):
    n_slots, sub, _ = xs.shape
    e, f, d = w_dn_bf.shape
    nb = n_slots // MOE_BLOCK
    nf = EXPERT_F_TILES if (f // EXPERT_F_TILES) % LANES == 0 else 1
    tf = f // nf

    def xmap(i, j, be, nu):
        return (jnp.minimum(i, nu[0] - 1), 0, 0)

    def jeff(i, j, nu):
        return jnp.where(i < nu[0], j, nf - 1)

    wmode = dict(pipeline_mode=pl.Buffered(1)) if nf == 1 else {}

    return pl.pallas_call(
        _expert_kernel,
        grid_spec=pltpu.PrefetchScalarGridSpec(
            num_scalar_prefetch=2,
            grid=(nb, nf),
            in_specs=[pl.BlockSpec((MOE_BLOCK, sub, LANES), xmap),
                      pl.BlockSpec((1, d, tf), lambda i, j, be, nu: (be[i], 0, jeff(i, j, nu)), **wmode),
                      pl.BlockSpec((1, d, tf), lambda i, j, be, nu: (be[i], 0, nf + jeff(i, j, nu)), **wmode),
                      pl.BlockSpec((1, tf, d), lambda i, j, be, nu: (be[i], jeff(i, j, nu), 0), **wmode)],
            out_specs=pl.BlockSpec((MOE_BLOCK, sub, LANES), lambda i, j, be, nu: (i, 0, 0)),
            scratch_shapes=[] if nf == 1 else [pltpu.VMEM((MOE_BLOCK, d), BF16),
                                               pltpu.VMEM((MOE_BLOCK, d), F32)]),
        out_shape=jax.ShapeDtypeStruct((n_slots, sub, LANES), F32),
        compiler_params=pltpu.CompilerParams(dimension_semantics=("arbitrary", "arbitrary"),
                                             vmem_limit_bytes=EXPERT_VMEM_LIMIT),
        name="expert_ffn",
    )(block_expert, n_used, xs, w_gu_bf, w_gu_bf, w_dn_bf)


def _combine_kernel(tm, slot_ref, x_ref, route_ref, mod_ref, ys_ref, o_ref, ybuf, sem):
    i = pl.program_id(0)

    def issue(step, b):
        base = step * (tm * TOP_K)

        def body(r, carry):
            for k in range(TOP_K):
                s = slot_ref[base + TOP_K * r + k]
                pltpu.make_async_copy(ys_ref.at[pl.ds(s, 1)], ybuf.at[b, k, pl.ds(r, 1)],
                                      sem.at[b]).start(priority=k % 2)
            return carry

        lax.fori_loop(0, tm, body, 0, unroll=8)

    @pl.when(i == 0)
    def _():
        issue(0, 0)

    @pl.when(i + 1 < pl.num_programs(0))
    def _():
        issue(i + 1, (i + 1) % 2)

    b = i % 2

    def drain(r, carry):
        pltpu.make_async_copy(ys_ref.at[pl.ds(0, 1)], ybuf.at[b, 0, pl.ds(0, 1)], sem.at[b]).wait()
        return carry

    lax.fori_loop(0, tm * TOP_K, drain, 0, unroll=8)
    y0 = ybuf[b, 0].reshape(x_ref.shape)
    y1 = ybuf[b, 1].reshape(x_ref.shape)
    g0 = route_ref[:, 0:1]
    g1 = route_ref[:, 1:2]
    o_ref[...] = x_ref[...] + mod_ref[0, 5:6, :] * (g0 * y0 + g1 * y1)


def _combine(x3, ys, slot, route, mods, tm, tiles_per_batch):
    t, d = x3.shape
    tile = ys.shape[1:]
    row = lambda i, sl: (i, 0)
    return pl.pallas_call(
        functools.partial(_combine_kernel, tm),
        grid_spec=pltpu.PrefetchScalarGridSpec(
            num_scalar_prefetch=1,
            grid=(t // tm,),
            in_specs=[pl.BlockSpec((tm, d), row),
                      pl.BlockSpec((tm, LANES), row),
                      pl.BlockSpec((1, 6, d), lambda i, sl: (i // tiles_per_batch, 0, 0)),
                      pl.BlockSpec(memory_space=pl.ANY)],
            out_specs=pl.BlockSpec((tm, d), row),
            scratch_shapes=[pltpu.VMEM((2, TOP_K, tm) + tile, F32), pltpu.SemaphoreType.DMA((2,))]),
        out_shape=jax.ShapeDtypeStruct((t, d), F32),
        compiler_params=_cparams("arbitrary"),
        name="combine",
    )(slot, x3, route, mods, ys)


def _route_slots(route, n_experts):
    t = route.shape[0]
    expert = route[:, 2:2 + TOP_K].astype(jnp.int32).reshape(t * TOP_K)
    onehot = (expert[:, None] == jnp.arange(n_experts, dtype=jnp.int32)[None, :]).astype(jnp.int32)
    csum = jnp.cumsum(onehot, axis=0)
    counts = csum[-1]
    padded = (counts + MOE_BLOCK - 1) // MOE_BLOCK * MOE_BLOCK
    pad_end = jnp.cumsum(padded)
    pad_start = pad_end - padded
    slot = jnp.sum(onehot * (pad_start[None, :] + csum - 1), axis=1).astype(jnp.int32)
    n_slots = t * TOP_K + n_experts * MOE_BLOCK
    nb = n_slots // MOE_BLOCK
    block_expert = jnp.minimum(
        jnp.searchsorted(pad_end, jnp.arange(nb, dtype=jnp.int32) * MOE_BLOCK, side='right'),
        n_experts - 1).astype(jnp.int32)
    n_used = (pad_end[-1:] // MOE_BLOCK).astype(jnp.int32)
    return slot, block_expert, n_used, pad_end.astype(jnp.int32), n_slots


def _rope_tables(n):
    pos = np.arange(n)
    row = (pos // GRID_W).astype(np.float32)
    col = (pos % GRID_W).astype(np.float32)
    n_freq = RET_DK // 4
    inv_freq = (np.float32(ROPE_BASE) ** (-np.arange(n_freq, dtype=np.float32) / n_freq)).astype(np.float32)
    ang = np.concatenate([row[:, None] * inv_freq, col[:, None] * inv_freq], axis=-1)
    cos = np.cos(ang.astype(np.float64))
    sin = np.sin(ang.astype(np.float64))
    cos_h = np.repeat(cos, 2, axis=-1)
    sin_h = np.stack([-sin, sin], axis=-1).reshape(n, RET_DK)
    return (np.tile(cos_h, (1, RET_HEADS)).astype(np.float32),
            np.tile(sin_h, (1, RET_HEADS)).astype(np.float32))


def kernel(x, c, ctx, c_ctx, l0_norm_mix, l0_norm_ffn, l0_w_ada, l0_b_ada, l0_w_in, l0_q_gain, l0_k_gain, l0_rpb, l0_ret_decay_fwd, l0_ret_decay_bwd, l0_w_out, l0_w_gate_up, l0_w_down, l1_norm_mix, l1_norm_ffn, l1_w_ada, l1_b_ada, l1_w_fourier, l1_w_router, l1_w_exp_gate_up, l1_w_exp_down):
    batch, n, d = x.shape
    ctx_len = ctx.shape[1]
    t = batch * n
    rows = n // GRID_W
    n_experts = l1_w_router.shape[1]
    assert n % (NA_QROWS * GRID_W) == 0 and rows >= NA_KROWS and n % FFT_N2 == 0
    assert l0_w_in.shape[1] == _IN_WIDTH and d % LANES == 0

    x2d = x.reshape(t, d)
    ctx2d = ctx.reshape(batch * ctx_len, d)

    ada0 = _ada(jnp.concatenate([c, c_ctx[None, :]], axis=0), l0_w_ada, l0_b_ada)
    mods0 = ada0[:batch].reshape(batch, 6, d)
    mods_ctx = ada0[batch:batch + 1].reshape(1, 6, d)
    mods1 = _ada(c, l1_w_ada, l1_b_ada)[:batch].reshape(batch, 6, d)

    w_in_bf = l0_w_in.astype(BF16)
    qg = jnp.tile(l0_q_gain.astype(F32), NA_HEADS).reshape(1, NA_WIDTH)
    kg = jnp.tile(l0_k_gain.astype(F32), NA_HEADS).reshape(1, NA_WIDTH)
    gmat = jnp.asarray(np.kron(np.eye(NA_HEADS), np.full((NA_HEAD_DIM, NA_HEAD_DIM), 1.0 / NA_HEAD_DIM)),
                       BF16)
    cos_np, sin_np = _rope_tables(n)
    gain_mix0 = l0_norm_mix.reshape(1, d)
    tm = min(INPROJ_ROWS, n)
    naq, rq, rg, nak, nav, rk, rv = _inproj(
        x2d, mods0, gain_mix0, w_in_bf, qg, kg, gmat, jnp.asarray(cos_np), jnp.asarray(sin_np),
        tm, n // tm, n // tm)
    ctx_rows = batch * ctx_len
    _, _, _, cx_nak, cx_nav, cx_rk, cx_rv = _inproj(
        ctx2d, mods_ctx, gain_mix0, w_in_bf, qg, kg, gmat,
        jnp.ones((ctx_len, RET_QK_WIDTH), F32), jnp.zeros((ctx_len, RET_QK_WIDTH), F32),
        ctx_len, ctx_rows // ctx_len, 1)

    bias = _na_bias_table(l0_rpb, rows)
    o_na = _na_attention(naq, nak, nav, cx_nak, cx_nav, bias, batch, n, ctx_len)

    lg_f = jax.nn.log_sigmoid(l0_ret_decay_fwd.astype(F32))
    lg_b = jax.nn.log_sigmoid(l0_ret_decay_bwd.astype(F32))
    o_f = _retention_pass(False, lg_f, rq, rk, rv, cx_rk, cx_rv, batch, n, ctx_len)
    o_ret = _retention_pass(True, lg_b, rq, rk, rv, cx_rk, cx_rv, batch, n, ctx_len, o_fwd=o_f, gate=rg)

    tf_rows = min(FFN_ROWS, n)
    x2 = _mix_ffn(x2d, o_na, o_ret, l0_w_out.astype(BF16), l0_norm_ffn.reshape(1, d),
                  l0_w_gate_up.astype(BF16), l0_w_down.astype(BF16), mods0, tf_rows, n // tf_rows)

    f1k, tw, m2k, cs = _dft_tables(n, d)
    wf = _fold_channel_dft(jnp.asarray(cs), l1_w_fourier).astype(BF16)
    bmid = _fft_stage1(x2, mods1, l1_norm_mix.reshape(1, d), jnp.asarray(f1k, BF16), jnp.asarray(tw),
                       batch, n)
    wr_pad = jnp.zeros((d, LANES), F32).at[:, :n_experts].set(l1_w_router)
    x3, h2, route = _fft_stage2(bmid, x2, jnp.asarray(m2k, BF16), wf, mods1, l1_norm_ffn.reshape(1, d),
                                wr_pad, n_experts, batch, n)

    slot, block_expert, n_used, pad_end, n_slots = _route_slots(route, n_experts)
    xs = _dispatch(h2, slot, pad_end, n_slots, tm)
    ys = _expert_ffn(xs, l1_w_exp_gate_up.astype(BF16), l1_w_exp_down.astype(BF16), block_expert, n_used)
    out = _combine(x3, ys, slot, route, mods1, tm, n // tm)
    return out.reshape(batch, n, d)
```
